```python
import math
import jax, jax.numpy as jnp
from jax import lax
import numpy as np

D_MODEL = 1024
BATCH = 4
SEQ = 8192
DEPTH = 4

GRID_W = 64
HEAD_DIM = D_MODEL // 16
Q_BLOCK = 128
RMS_EPS = 1e-6
NEG_INF = -1e30

NA_HEADS = 4
NA_WIN_H = 8
NA_WIN_W = 16
DIFF_HEADS = 4
DIFF_QK_DIM = HEAD_DIM // 2
DIFF_V_DIM = HEAD_DIM
GQA_Q_HEADS = 4
GQA_KV_HEADS = 2
AXIAL_THETA = 10000.0
DIL_HEADS = 4
DIL_PAIRS = ((128, 1), (512, 4), (2048, 16))
ROPE_THETA = 500000.0
ROPE_FRACTION = 4
D_FF = ((8 * D_MODEL // 3 + 255) // 256) * 256

NA_W = NA_HEADS * HEAD_DIM
DIFF_QK_W = DIFF_HEADS * 2 * DIFF_QK_DIM
DIFF_V_W = DIFF_HEADS * DIFF_V_DIM
GQA_Q_W = GQA_Q_HEADS * HEAD_DIM
GQA_KV_W = GQA_KV_HEADS * HEAD_DIM
DIL_W = DIL_HEADS * HEAD_DIM
IN_SIZES = (NA_W, NA_W, NA_W, DIFF_QK_W, DIFF_QK_W, DIFF_V_W, GQA_Q_W, GQA_KV_W, GQA_KV_W, DIL_W, DIL_W, DIL_W)
IN_WIDTH = sum(IN_SIZES)
MIX_WIDTH = NA_W + DIFF_V_W + GQA_Q_W + DIL_W

kernel_name = 'hybrid_parallel_head_encoder'


def rms_norm(x, gain):
    xf = x.astype(jnp.float32)
    y = xf * lax.rsqrt(jnp.mean(xf * xf, axis=-1, keepdims=True) + RMS_EPS)
    return (y * gain.astype(jnp.float32)).astype(x.dtype)


def rope_tables(pos, dim, theta):
    inv_freq = 1.0 / (theta ** (jnp.arange(0, dim, 2, dtype=jnp.float32) / dim))
    ang = pos.astype(jnp.float32)[:, None] * inv_freq[None, :]
    return jnp.cos(ang), jnp.sin(ang)


def apply_rope(x, cos, sin):
    x1, x2 = jnp.split(x.astype(jnp.float32), 2, axis=-1)
    c = cos[None, :, None, :]
    s = sin[None, :, None, :]
    return jnp.concatenate([x1 * c - x2 * s, x1 * s + x2 * c], axis=-1).astype(x.dtype)


def partial_rope(x, cos, sin, n_rot):
    return jnp.concatenate([apply_rope(x[..., :n_rot], cos, sin), x[..., n_rot:]], axis=-1)


def swiglu(h, w_gate, w_up, w_down):
    return (jax.nn.silu(h @ w_gate) * (h @ w_up)) @ w_down


def split_in_proj(h):
    points = np.cumsum(IN_SIZES)[:-1].tolist()
    return jnp.split(h, points, axis=-1)


def neighborhood_attention(q, k, v, q_gain, k_gain, rel_bias, rows):
    B, S, _ = q.shape
    H, d = NA_HEADS, HEAD_DIM
    q = rms_norm(q.reshape(B, S, H, d), q_gain)
    k = rms_norm(k.reshape(B, S, H, d), k_gain)
    v = v.reshape(B, S, H, d)
    to_grid = lambda t: t.reshape(B, rows, GRID_W, H, d).transpose(0, 3, 1, 2, 4)
    qg, kg, vg = to_grid(q), to_grid(k), to_grid(v)
    kh = min(NA_WIN_H, rows)
    cols = jnp.arange(GRID_W)
    col_idx = jnp.clip(cols - NA_WIN_W // 2, 0, GRID_W - NA_WIN_W)[:, None] + jnp.arange(NA_WIN_W)[None, :]
    col_off = col_idx - cols[:, None] + (NA_WIN_W - 1)
    scale = d ** -0.5

    def row_fn(r):
        rs = jnp.clip(r - kh // 2, 0, rows - kh)
        q_r = lax.dynamic_index_in_dim(qg, r, axis=2, keepdims=False)
        k_rows = lax.dynamic_slice_in_dim(kg, rs, kh, axis=2)
        v_rows = lax.dynamic_slice_in_dim(vg, rs, kh, axis=2)
        k_win = k_rows[:, :, :, col_idx]
        v_win = v_rows[:, :, :, col_idx]
        row_off = rs + jnp.arange(kh) - r + (NA_WIN_H - 1)
        bias = rel_bias[:, row_off][:, :, col_off].transpose(0, 2, 1, 3)
        s = jnp.einsum('bhcd,bhacjd->bhcaj', q_r, k_win).astype(jnp.float32) * scale
        s = s + bias[None].astype(jnp.float32)
        p = jax.nn.softmax(s.reshape(B, H, GRID_W, kh * NA_WIN_W), axis=-1).reshape(s.shape)
        return jnp.einsum('bhcaj,bhacjd->bhcd', p.astype(v.dtype), v_win)

    out = lax.map(row_fn, jnp.arange(rows))
    return out.transpose(1, 0, 3, 2, 4).reshape(B, S, H * d)


def diff_attention(q, k, v, q_gain, k_gain, lq1, lk1, lq2, lk2, out_gain, lambda_init, cos, sin):
    B, S, _ = q.shape
    H, d = DIFF_HEADS, DIFF_QK_DIM
    n_rot = d // ROPE_FRACTION

    def prep(t, g):
        t = partial_rope(rms_norm(t.reshape(B, S, 2 * H, d), g), cos, sin, n_rot)
        return t.reshape(B, S, H, 2, d).transpose(3, 0, 2, 1, 4)

    q12 = prep(q, q_gain)
    k12 = prep(k, k_gain)
    vt = v.reshape(B, S, H, DIFF_V_DIM).transpose(0, 2, 1, 3)
    lam = (jnp.exp(jnp.sum(lq1.astype(jnp.float32) * lk1.astype(jnp.float32)))
           - jnp.exp(jnp.sum(lq2.astype(jnp.float32) * lk2.astype(jnp.float32))) + lambda_init)
    scale = d ** -0.5
    nb = S // Q_BLOCK
    qb = q12.reshape(2, B, H, nb, Q_BLOCK, d).transpose(3, 0, 1, 2, 4, 5)

    def block_fn(qblk):
        s = jnp.einsum('ibhqd,ibhkd->ibhqk', qblk, k12).astype(jnp.float32) * scale
        p = jax.nn.softmax(s, axis=-1)
        a = p[0] - lam * p[1]
        return jnp.einsum('bhqk,bhkd->bhqd', a.astype(vt.dtype), vt)

    o = lax.map(block_fn, qb)
    o = o.transpose(1, 2, 0, 3, 4).reshape(B, H, S, DIFF_V_DIM)
    o = rms_norm(o, out_gain) * (1.0 - lambda_init)
    return o.transpose(0, 2, 1, 3).reshape(B, S, H * DIFF_V_DIM)


def gqa_axial_attention(q, k, v, q_gain, k_gain, row_cs, col_cs):
    B, S, _ = q.shape
    Hq, Hkv, d = GQA_Q_HEADS, GQA_KV_HEADS, HEAD_DIM
    G = Hq // Hkv
    half = d // 2

    def axial(t):
        return jnp.concatenate([apply_rope(t[..., :half], *row_cs), apply_rope(t[..., half:], *col_cs)], axis=-1)

    q = axial(rms_norm(q.reshape(B, S, Hq, d), q_gain))
    k = axial(rms_norm(k.reshape(B, S, Hkv, d), k_gain))
    kt = k.transpose(0, 2, 1, 3)
    vt = v.reshape(B, S, Hkv, d).transpose(0, 2, 1, 3)
    nb = S // Q_BLOCK
    qb = q.reshape(B, nb, Q_BLOCK, Hkv, G, d).transpose(1, 0, 3, 4, 2, 5)
    scale = d ** -0.5

    def block_fn(qblk):
        s = jnp.einsum('bngqd,bnkd->bngqk', qblk, kt).astype(jnp.float32) * scale
        p = jax.nn.softmax(s, axis=-1)
        return jnp.einsum('bngqk,bnkd->bngqd', p.astype(vt.dtype), vt)

    o = lax.map(block_fn, qb)
    return o.transpose(1, 0, 4, 2, 3, 5).reshape(B, S, Hq * d)


def dilated_attention(q, k, v, q_gain, k_gain, cos, sin):
    B, S, _ = q.shape
    H, d = DIL_HEADS, HEAD_DIM
    n_rot = d // ROPE_FRACTION
    q = partial_rope(rms_norm(q.reshape(B, S, H, d), q_gain), cos, sin, n_rot).transpose(0, 2, 1, 3)
    k = partial_rope(rms_norm(k.reshape(B, S, H, d), k_gain), cos, sin, n_rot).transpose(0, 2, 1, 3)
    v = v.reshape(B, S, H, d).transpose(0, 2, 1, 3)
    nb = S // Q_BLOCK
    scale = d ** -0.5

    def block_fn(b):
        t = b * Q_BLOCK + jnp.arange(Q_BLOCK)
        q_b = lax.dynamic_slice_in_dim(q, b * Q_BLOCK, Q_BLOCK, axis=2)
        outs, lses = [], []
        for window, dil in DIL_PAIRS:
            n_side = (window // 2) // dil
            idx = t[:, None] + dil * jnp.arange(-n_side, n_side + 1)[None, :]
            valid = (idx >= 0) & (idx < S)
            idx = jnp.clip(idx, 0, S - 1)
            k_g = jnp.take(k, idx, axis=2)
            v_g = jnp.take(v, idx, axis=2)
            s = jnp.einsum('bhqd,bhqnd->bhqn', q_b, k_g).astype(jnp.float32) * scale
            s = jnp.where(valid[None, None], s, NEG_INF)
            lse = jax.nn.logsumexp(s, axis=-1, keepdims=True)
            p = jnp.exp(s - lse)
            outs.append(jnp.einsum('bhqn,bhqnd->bhqd', p.astype(v.dtype), v_g).astype(jnp.float32))
            lses.append(lse)
        w = jax.nn.softmax(jnp.stack(lses, axis=0), axis=0)
        return jnp.sum(w * jnp.stack(outs, axis=0), axis=0).astype(v.dtype)

    o = lax.map(block_fn, jnp.arange(nb))
    return o.transpose(1, 0, 3, 2, 4).reshape(B, S, H * d)


def setup_inputs(seed: int = 0) -> dict:
    key = jax.random.key(seed)
    ks = jax.random.split(key, 32)
    L, D, F = DEPTH, D_MODEL, D_FF

    def w(i, shape, fan_in):
        return jax.random.normal(ks[i], shape, jnp.float32) * (fan_in ** -0.5)

    def gain(i, shape):
        return 1.0 + 0.02 * jax.random.normal(ks[i], shape, jnp.float32)

    def small(i, shape, sd):
        return sd * jax.random.normal(ks[i], shape, jnp.float32)

    return {
        'x': jax.random.normal(ks[0], (BATCH, SEQ, D), jnp.float32),
        'ffn1_norm': gain(1, (L, D)),
        'ffn1_w_gate': w(2, (L, D, F), D),
        'ffn1_w_up': w(3, (L, D, F), D),
        'ffn1_w_down': w(4, (L, F, D), F),
        'mix_norm': gain(5, (L, D)),
        'w_in': w(6, (L, D, IN_WIDTH), D),
        'w_out': w(7, (L, MIX_WIDTH, D), MIX_WIDTH),
        'na_q_norm': gain(8, (L, HEAD_DIM)),
        'na_k_norm': gain(9, (L, HEAD_DIM)),
        'na_rel_bias': small(10, (L, NA_HEADS, 2 * NA_WIN_H - 1, 2 * NA_WIN_W - 1), 0.02),
        'diff_q_norm': gain(11, (L, DIFF_QK_DIM)),
        'diff_k_norm': gain(12, (L, DIFF_QK_DIM)),
        'diff_lambda_q1': small(13, (L, DIFF_QK_DIM), 0.1),
        'diff_lambda_k1': small(14, (L, DIFF_QK_DIM), 0.1),
        'diff_lambda_q2': small(15, (L, DIFF_QK_DIM), 0.1),
        'diff_lambda_k2': small(16, (L, DIFF_QK_DIM), 0.1),
        'diff_out_norm': gain(17, (L, DIFF_V_DIM)),
        'gqa_q_norm': gain(18, (L, HEAD_DIM)),
        'gqa_k_norm': gain(19, (L, HEAD_DIM)),
        'dil_q_norm': gain(20, (L, HEAD_DIM)),
        'dil_k_norm': gain(21, (L, HEAD_DIM)),
        'ffn2_norm': gain(22, (L, D)),
        'ffn2_w_gate': w(23, (L, D, F), D),
        'ffn2_w_up': w(24, (L, D, F), D),
        'ffn2_w_down': w(25, (L, F, D), F),
    }


def reference(x, ffn1_norm, ffn1_w_gate, ffn1_w_up, ffn1_w_down, mix_norm, w_in, w_out,
              na_q_norm, na_k_norm, na_rel_bias, diff_q_norm, diff_k_norm,
              diff_lambda_q1, diff_lambda_k1, diff_lambda_q2, diff_lambda_k2, diff_out_norm,
              gqa_q_norm, gqa_k_norm, dil_q_norm, dil_k_norm,
              ffn2_norm, ffn2_w_gate, ffn2_w_up, ffn2_w_down):
    B, S, _ = x.shape
    rows = S // GRID_W
    pos = jnp.arange(S, dtype=jnp.int32)
    diff_cs = rope_tables(pos, DIFF_QK_DIM // ROPE_FRACTION, ROPE_THETA)
    dil_cs = rope_tables(pos, HEAD_DIM // ROPE_FRACTION, ROPE_THETA)
    row_cs = rope_tables(pos // GRID_W, HEAD_DIM // 2, AXIAL_THETA)
    col_cs = rope_tables(pos % GRID_W, HEAD_DIM // 2, AXIAL_THETA)

    for l in range(DEPTH):
        h = rms_norm(x, ffn1_norm[l])
        x = x + 0.5 * swiglu(h, ffn1_w_gate[l], ffn1_w_up[l], ffn1_w_down[l])

        h = rms_norm(x, mix_norm[l])
        (na_q, na_k, na_v, df_q, df_k, df_v,
         gq_q, gq_k, gq_v, dl_q, dl_k, dl_v) = split_in_proj(h @ w_in[l])
        o_a = neighborhood_attention(na_q, na_k, na_v, na_q_norm[l], na_k_norm[l], na_rel_bias[l], rows)
        lambda_init = 0.8 - 0.6 * math.exp(-0.3 * l)
        o_b = diff_attention(df_q, df_k, df_v, diff_q_norm[l], diff_k_norm[l],
                             diff_lambda_q1[l], diff_lambda_k1[l], diff_lambda_q2[l], diff_lambda_k2[l],
                             diff_out_norm[l], lambda_init, *diff_cs)
        o_c = gqa_axial_attention(gq_q, gq_k, gq_v, gqa_q_norm[l], gqa_k_norm[l], row_cs, col_cs)
        o_d = dilated_attention(dl_q, dl_k, dl_v, dil_q_norm[l], dil_k_norm[l], *dil_cs)
        o = jnp.concatenate([o_a, o_b, o_c, o_d], axis=-1)
        x = x + o @ w_out[l]

        h = rms_norm(x, ffn2_norm[l])
        x = x + 0.5 * swiglu(h, ffn2_w_gate[l], ffn2_w_up[l], ffn2_w_down[l])
    return x
```

```python
import functools
import math

import numpy as np
import jax
import jax.numpy as jnp
from jax import lax
from jax.experimental import pallas as pl
from jax.experimental.pallas import tpu as pltpu

F32 = jnp.float32
BF16 = jnp.bfloat16

GRID_W = 64
HEAD_DIM = 64
RMS_EPS = 1e-6
MASK_VALUE = -1e30
LOG2E = math.log2(math.e)

NA_HEADS = 4
NA_WIN_H = 8
NA_WIN_W = 16
DIFF_HEADS = 4
DIFF_QK_DIM = HEAD_DIM // 2
GQA_Q_HEADS = 4
GQA_KV_HEADS = 2
AXIAL_THETA = 10000.0
DIL_HEADS = 4
DIL_PAIRS = ((128, 1), (512, 4), (2048, 16))
ROPE_THETA = 500000.0
ROPE_FRACTION = 4

SEG = 256
(SEG_NA_Q, SEG_NA_K, SEG_NA_V, SEG_DF_Q, SEG_DF_K, SEG_DF_V,
 SEG_GQ_Q, SEG_GQ_K, SEG_GQ_V, SEG_DL_Q, SEG_DL_K, SEG_DL_V) = range(12)
N_SEG = 12
EXT_WIDTH = N_SEG * SEG

VMEM_LIMIT = 52 * 1024 * 1024

NA_QROWS = 8
NA_KROWS = 16


def _cparams(sem):
    return pltpu.CompilerParams(dimension_semantics=sem, vmem_limit_bytes=VMEM_LIMIT)


def _resident(shape):
    nd = len(shape)
    return pl.BlockSpec(shape, lambda *_: (0,) * nd, pipeline_mode=pl.Buffered(1))


def _rms_rows(x, gain):
    ms = jnp.mean(x * x, axis=-1, keepdims=True)
    return x * lax.rsqrt(ms + RMS_EPS) * gain


def _group_mean_sq(y, gmat):
    sq = y * y
    hi = sq.astype(BF16)
    lo = (sq - hi.astype(F32)).astype(BF16)
    return (jnp.dot(hi, gmat, preferred_element_type=F32)
            + jnp.dot(lo, gmat, preferred_element_type=F32))


def _dot_nt(a, b):
    return lax.dot_general(a, b, (((1,), (1,)), ((), ())), preferred_element_type=F32)


def _ffn_kernel(x_ref, g_ref, wg_ref, wu_ref, wd_ref, o_ref, *, chunks):
    x = x_ref[...]
    h = _rms_rows(x, g_ref[...]).astype(BF16)
    acc = None
    for c0, c1 in chunks:
        g = jnp.dot(h, wg_ref[:, c0:c1], preferred_element_type=F32)
        u = jnp.dot(h, wu_ref[:, c0:c1], preferred_element_type=F32)
        a = (g * (1.0 / (1.0 + jnp.exp(-g))) * u).astype(BF16)
        d = jnp.dot(a, wd_ref[c0:c1, :], preferred_element_type=F32)
        acc = d if acc is None else acc + d
    o_ref[...] = x + 0.5 * acc


def _ffn(x, gain, wg, wu, wd, *, tm=512, fc=512):
    B, S, D = x.shape
    F = wg.shape[1]
    chunks = tuple((c, min(c + fc, F)) for c in range(0, F, fc))
    xspec = pl.BlockSpec((None, tm, D), lambda b, i: (b, i, 0))
    return pl.pallas_call(
        functools.partial(_ffn_kernel, chunks=chunks),
        out_shape=jax.ShapeDtypeStruct(x.shape, F32),
        grid=(B, S // tm),
        in_specs=[xspec, _resident((1, D)), _resident((D, F)), _resident((D, F)),
                  _resident((F, D))],
        out_specs=xspec,
        compiler_params=_cparams(("parallel", "parallel")),
        name="ffn",
    )(x, gain.reshape(1, D), wg, wu, wd)


_SEG_PLAN = {
    SEG_NA_Q: (0, 0, None), SEG_NA_K: (1, 0, None),
    SEG_DF_Q: (2, 1, "df"), SEG_DF_K: (3, 1, "df"),
    SEG_GQ_Q: (4, 0, "gq"), SEG_GQ_K: (5, 0, "gq"),
    SEG_DL_Q: (6, 0, "dl"), SEG_DL_K: (7, 0, "dl"),
}
_ROPE_HALF = {"df": DIFF_QK_DIM // ROPE_FRACTION // 2, "gq": HEAD_DIM // 4,
              "dl": HEAD_DIM // ROPE_FRACTION // 2}


def _inproj_kernel(x_ref, gn_ref, w_ref, gmat_ref, gains_ref,
                   cdf_ref, sdf_ref, cgq_ref, sgq_ref, cdl_ref, sdl_ref, o_ref):
    h = _rms_rows(x_ref[...], gn_ref[...]).astype(BF16)
    lane = lax.broadcasted_iota(jnp.int32, (1, SEG), 1)
    tables = {"df": (cdf_ref, sdf_ref), "gq": (cgq_ref, sgq_ref), "dl": (cdl_ref, sdl_ref)}
    for seg in range(N_SEG):
        y = jnp.dot(h, w_ref[:, seg * SEG:(seg + 1) * SEG], preferred_element_type=F32)
        plan = _SEG_PLAN.get(seg)
        if plan is not None:
            gi, mi, rope = plan
            ms = _group_mean_sq(y, gmat_ref[mi])
            y = y * lax.rsqrt(ms + RMS_EPS) * gains_ref[gi:gi + 1, :]
            if rope is not None:
                half = _ROPE_HALF[rope]
                c_ref, s_ref = tables[rope]
                first = (lane & (2 * half - 1)) < half
                partner = jnp.where(first, pltpu.roll(y, SEG - half, 1), pltpu.roll(y, half, 1))
                y = y * c_ref[...] + partner * s_ref[...]
        o_ref[:, seg * SEG:(seg + 1) * SEG] = y.astype(BF16)


def _inproj(x, gain, w_ext, gmats, gains, tables, *, tm=512):
    B, S, D = x.shape
    tspec = pl.BlockSpec((tm, SEG), lambda i, b: (i, 0))
    return pl.pallas_call(
        _inproj_kernel,
        out_shape=jax.ShapeDtypeStruct((B, S, EXT_WIDTH), BF16),
        grid=(S // tm, B),
        in_specs=[pl.BlockSpec((None, tm, D), lambda i, b: (b, i, 0)),
                  _resident((1, D)), _resident((D, EXT_WIDTH)),
                  _resident(gmats.shape), _resident(gains.shape)] + [tspec] * 6,
        out_specs=pl.BlockSpec((None, tm, EXT_WIDTH), lambda i, b: (b, i, 0)),
        compiler_params=_cparams(("parallel", "parallel")),
        name="inproj",
    )(x, gain.reshape(1, D), w_ext, gmats, gains, *tables)


def _flash_kernel(q_ref, k_ref, v_ref, gmat_ref, ogain_ref, lamp_ref, li_ref, o_ref,
                  qm_ref, m_ref, l_ref, acc_ref, *, groups, tq, tk, diff):
    gw = SEG // groups
    q = q_ref[...]
    lane_g = lax.broadcasted_iota(jnp.int32, (1, SEG), 1) >> int(math.log2(gw))
    for r in range(groups):
        qm_ref[r * tq:(r + 1) * tq, :] = jnp.where(lane_g == r, q, jnp.zeros_like(q))
    m_ref[...] = jnp.full(m_ref.shape, MASK_VALUE, F32)
    l_ref[...] = jnp.zeros(l_ref.shape, F32)
    acc_ref[...] = jnp.zeros(acc_ref.shape, F32)

    def body(kv, carry):
        off = pl.multiple_of(kv * tk, tk)
        k = k_ref[pl.ds(off, tk), :]
        v = v_ref[pl.ds(off, tk), :]
        s = _dot_nt(qm_ref[...], k)
        m_prev = m_ref[...]
        m_new = jnp.maximum(m_prev, jnp.max(s, axis=1, keepdims=True))
        alpha = jnp.exp2(m_prev - m_new)
        p = jnp.exp2(s - m_new)
        l_ref[...] = alpha * l_ref[...] + jnp.sum(p, axis=1, keepdims=True)
        acc_ref[...] = alpha * acc_ref[...] + jnp.dot(p.astype(BF16), v,
                                                      preferred_element_type=F32)
        m_ref[...] = m_new
        return carry

    lax.fori_loop(0, k_ref.shape[0] // tk, body, 0)

    norm = acc_ref[...] * (1.0 / l_ref[...])
    if diff:
        lp = lamp_ref[...]
        lam = (jnp.exp(jnp.sum(lp[0:1] * lp[1:2], axis=1, keepdims=True))
               - jnp.exp(jnp.sum(lp[2:3] * lp[3:4], axis=1, keepdims=True)) + li_ref[...])
        lane_h = lax.broadcasted_iota(jnp.int32, (1, SEG), 1) >> int(math.log2(2 * gw))
        out = jnp.zeros((tq, SEG), F32)
        for hd in range(groups // 2):
            a = norm[2 * hd * tq:(2 * hd + 1) * tq] - lam * norm[(2 * hd + 1) * tq:(2 * hd + 2) * tq]
            out = jnp.where(lane_h == hd, a, out)
        ms = _group_mean_sq(out, gmat_ref[...])
        out = out * lax.rsqrt(ms + RMS_EPS) * ogain_ref[...] * (1.0 - li_ref[...])
    else:
        out = jnp.zeros((tq, SEG), F32)
        for r in range(groups):
            out = jnp.where(lane_g == r, norm[r * tq:(r + 1) * tq], out)
    o_ref[...] = out.astype(BF16)


def _flash(qkv, segs, gmat, ogain, lamp, li, *, groups, diff, tk=512, rows=2048):
    B, S, _ = qkv.shape
    tq = rows // groups
    qs, ks, vs = segs
    return pl.pallas_call(
        functools.partial(_flash_kernel, groups=groups, tq=tq, tk=tk, diff=diff),
        out_shape=jax.ShapeDtypeStruct((B, S, SEG), BF16),
        grid=(B, S // tq),
        in_specs=[pl.BlockSpec((None, tq, SEG), lambda b, i: (b, i, qs)),
                  pl.BlockSpec((None, S, SEG), lambda b, i: (b, 0, ks)),
                  pl.BlockSpec((None, S, SEG), lambda b, i: (b, 0, vs)),
                  _resident(gmat.shape), _resident(ogain.shape), _resident(lamp.shape),
                  _resident(li.shape)],
        out_specs=pl.BlockSpec((None, tq, SEG), lambda b, i: (b, i, 0)),
        scratch_shapes=[pltpu.VMEM((rows, SEG), BF16), pltpu.VMEM((rows, 1), F32),
                        pltpu.VMEM((rows, 1), F32), pltpu.VMEM((rows, SEG), F32)],
        compiler_params=_cparams(("parallel", "parallel")),
        name="diff_attn" if diff else "gqa_attn",
    )(qkv, qkv, qkv, gmat, ogain, lamp, li)


def _na_kernel(q_ref, k_ref, v_ref, tab_ref, o_ref, *, nq, nk):
    i = pl.program_id(2)
    seq = k_ref.shape[0]
    start = jnp.clip(i * nq - (nk - nq) // 2, 0, seq - nk)
    start = pl.multiple_of(start, (nk - nq) // 2)
    kw = k_ref[pl.ds(start, nk), :]
    vw = v_ref[pl.ds(start, nk), :]
    q = q_ref[...]
    lane_h = lax.broadcasted_iota(jnp.int32, (1, 2 * HEAD_DIM), 1) >> int(math.log2(HEAD_DIM))
    outs = []
    for j in range(2):
        qm = jnp.where(lane_h == j, q, jnp.zeros_like(q))
        s = _dot_nt(qm, kw) + tab_ref[j]
        m = jnp.max(s, axis=1, keepdims=True)
        p = jnp.exp2(s - m)
        l = jnp.sum(p, axis=1, keepdims=True)
        outs.append(jnp.dot(p.astype(BF16), vw, preferred_element_type=F32) * (1.0 / l))
    o_ref[...] = jnp.where(lane_h == 0, outs[0], outs[1]).astype(BF16)


def _na_table(rel_bias, rows):
    kh = min(NA_WIN_H, rows)
    variants = []
    for r0 in (0, NA_QROWS, rows - NA_QROWS):
        kr0 = int(np.clip(r0 - (NA_KROWS - NA_QROWS) // 2, 0, rows - NA_KROWS))
        r = r0 + np.arange(NA_QROWS)[:, None, None, None]
        c = np.arange(GRID_W)[None, :, None, None]
        kr = kr0 + np.arange(NA_KROWS)[None, None, :, None]
        kc = np.arange(GRID_W)[None, None, None, :]
        rs = np.clip(r - kh // 2, 0, rows - kh)
        cs = np.clip(c - NA_WIN_W // 2, 0, GRID_W - NA_WIN_W)
        valid = (kr >= rs) & (kr < rs + kh) & (kc >= cs) & (kc < cs + NA_WIN_W)
        ro = np.clip(kr - r + (NA_WIN_H - 1), 0, 2 * NA_WIN_H - 2)
        co = np.clip(kc - c + (NA_WIN_W - 1), 0, 2 * NA_WIN_W - 2)
        shape = (NA_QROWS * GRID_W, NA_KROWS * GRID_W)
        full = (NA_QROWS, GRID_W, NA_KROWS, GRID_W)
        variants.append((np.broadcast_to(valid, full).reshape(shape),
                         np.broadcast_to(ro, full).reshape(shape),
                         np.broadcast_to(co, full).reshape(shape)))
    valid = np.stack([v[0] for v in variants])
    ro = np.stack([v[1] for v in variants])
    co = np.stack([v[2] for v in variants])
    bias = rel_bias[:, ro, co] * LOG2E
    return jnp.where(valid[None], bias, MASK_VALUE).transpose(1, 0, 2, 3)


def _na(qkv, table):
    B, S, _ = qkv.shape
    nq, nk = NA_QROWS * GRID_W, NA_KROWS * GRID_W
    nblk = S // nq
    hp = 2 * HEAD_DIM

    def variant(i):
        return jnp.where(i == 0, 0, jnp.where(i == nblk - 1, 2, 1))

    return pl.pallas_call(
        functools.partial(_na_kernel, nq=nq, nk=nk),
        out_shape=jax.ShapeDtypeStruct((B, S, SEG), BF16),
        grid=(2, B, nblk),
        in_specs=[pl.BlockSpec((None, nq, hp), lambda p, b, i: (b, i, 2 * SEG_NA_Q + p)),
                  pl.BlockSpec((None, S, hp), lambda p, b, i: (b, 0, 2 * SEG_NA_K + p)),
                  pl.BlockSpec((None, S, hp), lambda p, b, i: (b, 0, 2 * SEG_NA_V + p)),
                  pl.BlockSpec((None, 2, nq, nk), lambda p, b, i: (variant(i), p, 0, 0))],
        out_specs=pl.BlockSpec((None, nq, hp), lambda p, b, i: (b, i, p)),
        compiler_params=_cparams(("parallel", "parallel", "arbitrary")),
        name="na_attn",
    )(qkv, qkv, qkv, table)


def _dil_kernel(q_ref, k_ref, v_ref, o_ref, lse_ref, *, tq, halo):
    j = pl.program_id(2)
    length = k_ref.shape[0]
    win = tq + 2 * halo
    start = pl.multiple_of(jnp.clip(j * tq - halo, 0, length - win), halo)
    kw = k_ref[pl.ds(start, win), :]
    vw = v_ref[pl.ds(start, win), :]
    q = q_ref[...]
    qpos = j * tq + lax.broadcasted_iota(jnp.int32, (tq, win), 0)
    kpos = start + lax.broadcasted_iota(jnp.int32, (tq, win), 1)
    bias = jnp.where(jnp.abs(kpos - qpos) <= halo, 0.0, MASK_VALUE)
    lane_h = lax.broadcasted_iota(jnp.int32, (1, SEG), 1) >> int(math.log2(HEAD_DIM))
    out = jnp.zeros((tq, SEG), F32)
    lse = jnp.zeros((tq, SEG), F32)
    for hd in range(DIL_HEADS):
        qm = jnp.where(lane_h == hd, q, jnp.zeros_like(q))
        s = _dot_nt(qm, kw) + bias
        m = jnp.max(s, axis=1, keepdims=True)
        p = jnp.exp2(s - m)
        l = jnp.sum(p, axis=1, keepdims=True)
        o = jnp.dot(p.astype(BF16), vw, preferred_element_type=F32) * (1.0 / l)
        out = jnp.where(lane_h == hd, o, out)
        lse = jnp.where(lane_h == hd, m + jnp.log(l) * LOG2E, lse)
    o_ref[...] = out
    lse_ref[...] = lse


def _dilated(qkv, dil, halo, *, tq=256):
    B, S, _ = qkv.shape
    length = S // dil
    view = qkv.reshape(B, length, dil * EXT_WIDTH)
    oshape = jax.ShapeDtypeStruct((B, length, dil * SEG), F32)
    ospec = pl.BlockSpec((None, tq, SEG), lambda b, r, j: (b, j, r))
    out, lse = pl.pallas_call(
        functools.partial(_dil_kernel, tq=tq, halo=halo),
        out_shape=(oshape, oshape),
        grid=(B, dil, length // tq),
        in_specs=[pl.BlockSpec((None, tq, SEG), lambda b, r, j: (b, j, r * N_SEG + SEG_DL_Q)),
                  pl.BlockSpec((None, length, SEG), lambda b, r, j: (b, 0, r * N_SEG + SEG_DL_K)),
                  pl.BlockSpec((None, length, SEG), lambda b, r, j: (b, 0, r * N_SEG + SEG_DL_V))],
        out_specs=(ospec, ospec),
        compiler_params=_cparams(("parallel", "parallel", "arbitrary")),
        name=f"dil_attn_{dil}",
    )(view, view, view)
    return out.reshape(B, S, SEG), lse.reshape(B, S, SEG)


def _outproj_kernel(x_ref, oa_ref, ob_ref, oc_ref, d0_ref, d1_ref, d2_ref,
                    l0_ref, l1_ref, l2_ref, w_ref, o_ref, cat_ref):
    l0, l1, l2 = l0_ref[...], l1_ref[...], l2_ref[...]
    m = jnp.maximum(jnp.maximum(l0, l1), l2)
    e0, e1, e2 = jnp.exp2(l0 - m), jnp.exp2(l1 - m), jnp.exp2(l2 - m)
    od = (e0 * d0_ref[...] + e1 * d1_ref[...] + e2 * d2_ref[...]) * (1.0 / (e0 + e1 + e2))
    cat_ref[:, 0 * SEG:1 * SEG] = oa_ref[...]
    cat_ref[:, 1 * SEG:2 * SEG] = ob_ref[...]
    cat_ref[:, 2 * SEG:3 * SEG] = oc_ref[...]
    cat_ref[:, 3 * SEG:4 * SEG] = od.astype(BF16)
    o_ref[...] = x_ref[...] + jnp.dot(cat_ref[...], w_ref[...], preferred_element_type=F32)


def _outproj(x, oa, ob, oc, dils, w_out, *, tm=512):
    B, S, D = x.shape
    xspec = pl.BlockSpec((None, tm, D), lambda b, i: (b, i, 0))
    sspec = pl.BlockSpec((None, tm, SEG), lambda b, i: (b, i, 0))
    outs = [d[0] for d in dils]
    lses = [d[1] for d in dils]
    return pl.pallas_call(
        _outproj_kernel,
        out_shape=jax.ShapeDtypeStruct(x.shape, F32),
        grid=(B, S // tm),
        in_specs=[xspec] + [sspec] * 9 + [_resident(w_out.shape)],
        out_specs=xspec,
        scratch_shapes=[pltpu.VMEM((tm, 4 * SEG), BF16)],
        compiler_params=_cparams(("parallel", "parallel")),
        name="outproj",
    )(x, oa, ob, oc, *outs, *lses, w_out)


def _rope_cs(pos, dim, theta):
    inv_freq = 1.0 / (theta ** (jnp.arange(0, dim, 2, dtype=F32) / dim))
    ang = pos.astype(F32)[:, None] * inv_freq[None, :]
    return jnp.cos(ang), jnp.sin(ang)


def _rope_tables(S):
    pos = jnp.arange(S, dtype=jnp.int32)

    def expand(parts_c, parts_s, reps):
        return (jnp.tile(jnp.concatenate(parts_c, axis=1), (1, reps)),
                jnp.tile(jnp.concatenate(parts_s, axis=1), (1, reps)))

    c, s = _rope_cs(pos, DIFF_QK_DIM // ROPE_FRACTION, ROPE_THETA)
    rest = DIFF_QK_DIM - 2 * c.shape[1]
    df = expand([c, c, jnp.ones((S, rest), F32)], [-s, s, jnp.zeros((S, rest), F32)],
                SEG // DIFF_QK_DIM)
    c, s = _rope_cs(pos, HEAD_DIM // ROPE_FRACTION, ROPE_THETA)
    rest = HEAD_DIM - 2 * c.shape[1]
    dl = expand([c, c, jnp.ones((S, rest), F32)], [-s, s, jnp.zeros((S, rest), F32)],
                SEG // HEAD_DIM)
    cr, sr = _rope_cs(pos // GRID_W, HEAD_DIM // 2, AXIAL_THETA)
    cc, sc = _rope_cs(pos % GRID_W, HEAD_DIM // 2, AXIAL_THETA)
    gq = expand([cr, cr, cc, cc], [-sr, sr, -sc, sc], SEG // HEAD_DIM)
    return df + gq + dl


def _group_matrices():
    lane = np.arange(SEG)
    mats = [(lane[:, None] // g == lane[None, :] // g).astype(np.float32) / g
            for g in (HEAD_DIM, DIFF_QK_DIM)]
    return jnp.asarray(np.stack(mats), BF16)


def _extend_w_in(w_in):
    kv0 = 7 * SEG
    hd = HEAD_DIM
    cols = np.concatenate([
        np.arange(0, kv0),
        kv0 + np.array([0, 0, 1, 1]).repeat(hd) * hd + np.tile(np.arange(hd), 4),
        kv0 + 2 * hd + np.array([0, 0, 1, 1]).repeat(hd) * hd + np.tile(np.arange(hd), 4),
        np.arange(kv0 + 4 * hd, w_in.shape[1]),
    ])
    return w_in[:, cols].astype(BF16)


def _gain_rows(na_q, na_k, df_q, df_k, gq_q, gq_k, dl_q, dl_k):
    def row(g, scale):
        return jnp.tile(g.astype(F32), SEG // g.shape[0]) * scale
    sc64 = HEAD_DIM ** -0.5 * LOG2E
    sc32 = DIFF_QK_DIM ** -0.5 * LOG2E
    return jnp.stack([row(na_q, sc64), row(na_k, 1.0), row(df_q, sc32), row(df_k, 1.0),
                      row(gq_q, sc64), row(gq_k, 1.0), row(dl_q, sc64), row(dl_k, 1.0)])


def kernel(x, ffn1_norm, ffn1_w_gate, ffn1_w_up, ffn1_w_down, mix_norm, w_in, w_out, na_q_norm, na_k_norm, na_rel_bias, diff_q_norm, diff_k_norm, diff_lambda_q1, diff_lambda_k1, diff_lambda_q2, diff_lambda_k2, diff_out_norm, gqa_q_norm, gqa_k_norm, dil_q_norm, dil_k_norm, ffn2_norm, ffn2_w_gate, ffn2_w_up, ffn2_w_down):
    B, S, D = x.shape
    depth = w_in.shape[0]
    rows = S // GRID_W
    tables = _rope_tables(S)
    gmats = _group_matrices()
    ones_row = jnp.ones((1, SEG), F32)
    zero_lamp = jnp.zeros((4, DIFF_QK_DIM), F32)
    zero_li = jnp.zeros((1, 1), F32)

    for l in range(depth):
        x = _ffn(x, ffn1_norm[l], ffn1_w_gate[l].astype(BF16), ffn1_w_up[l].astype(BF16),
                 ffn1_w_down[l].astype(BF16))

        gains = _gain_rows(na_q_norm[l], na_k_norm[l], diff_q_norm[l], diff_k_norm[l],
                           gqa_q_norm[l], gqa_k_norm[l], dil_q_norm[l], dil_k_norm[l])
        qkv = _inproj(x, mix_norm[l], _extend_w_in(w_in[l]), gmats, gains, tables)

        o_a = _na(qkv, _na_table(na_rel_bias[l], rows))
        lamp = jnp.stack([diff_lambda_q1[l], diff_lambda_k1[l], diff_lambda_q2[l],
                          diff_lambda_k2[l]]).astype(F32)
        li = jnp.full((1, 1), 0.8 - 0.6 * math.exp(-0.3 * l), F32)
        ogain = jnp.tile(diff_out_norm[l].astype(F32), SEG // HEAD_DIM).reshape(1, SEG)
        o_b = _flash(qkv, (SEG_DF_Q, SEG_DF_K, SEG_DF_V), gmats[0], ogain, lamp, li,
                     groups=2 * DIFF_HEADS, diff=True)
        o_c = _flash(qkv, (SEG_GQ_Q, SEG_GQ_K, SEG_GQ_V), gmats[0], ones_row, zero_lamp,
                     zero_li, groups=GQA_Q_HEADS, diff=False)
        dils = [_dilated(qkv, dil, (window // 2) // dil) for window, dil in DIL_PAIRS]
        x = _outproj(x, o_a, o_b, o_c, dils, w_out[l].astype(BF16))

        x = _ffn(x, ffn2_norm[l], ffn2_w_gate[l].astype(BF16), ffn2_w_up[l].astype(BF16),
                 ffn2_w_down[l].astype(BF16))
    return x
```

```python
import functools
import math

import numpy as np
import jax
import jax.numpy as jnp
from jax import lax
from jax.experimental import pallas as pl
from jax.experimental.pallas import tpu as pltpu

F32 = jnp.float32
BF16 = jnp.bfloat16

GRID_W = 64
HEAD_DIM = 64
RMS_EPS = 1e-6
MASK_VALUE = -1e30
LOG2E = math.log2(math.e)

NA_HEADS = 4
NA_WIN_H = 8
NA_WIN_W = 16
DIFF_HEADS = 4
DIFF_QK_DIM = HEAD_DIM // 2
GQA_Q_HEADS = 4
GQA_KV_HEADS = 2
AXIAL_THETA = 10000.0
DIL_HEADS = 4
DIL_PAIRS = ((128, 1), (512, 4), (2048, 16))
ROPE_THETA = 500000.0
ROPE_FRACTION = 4

SEG = 256
(SEG_NA_Q, SEG_NA_K, SEG_NA_V, SEG_DF_Q, SEG_DF_K, SEG_DF_V,
 SEG_GQ_Q, SEG_GQ_K, SEG_GQ_V, SEG_DL_Q, SEG_DL_K, SEG_DL_V) = range(12)
N_SEG = 12
EXT_WIDTH = N_SEG * SEG

VMEM_LIMIT = 52 * 1024 * 1024

VT_ROWS = HEAD_DIM + 16
COL_BLOCK = 256
QK_AHEAD = 4

NA_QROWS = 8
NA_KROWS = 16


def _cparams(sem):
    return pltpu.CompilerParams(dimension_semantics=sem, vmem_limit_bytes=VMEM_LIMIT)


def _resident(shape):
    nd = len(shape)
    return pl.BlockSpec(shape, lambda *_: (0,) * nd, pipeline_mode=pl.Buffered(1))


def _rms_rows(x, gain):
    ms = jnp.mean(x * x, axis=-1, keepdims=True)
    return x * lax.rsqrt(ms + RMS_EPS) * gain


def _group_mean_sq(y, gmat):
    sq = y * y
    hi = sq.astype(BF16)
    lo = (sq - hi.astype(F32)).astype(BF16)
    return (jnp.dot(hi, gmat, preferred_element_type=F32)
            + jnp.dot(lo, gmat, preferred_element_type=F32))


def _dot_nt(a, b):
    return lax.dot_general(a, b, (((1,), (1,)), ((), ())), preferred_element_type=F32)


def _ffn_kernel(x_ref, g_ref, wg_ref, wu_ref, wd_ref, o_ref, *, chunks):
    x = x_ref[...]
    h = _rms_rows(x, g_ref[...]).astype(BF16)
    acc = None
    for c0, c1 in chunks:
        g = jnp.dot(h, wg_ref[:, c0:c1], preferred_element_type=F32)
        u = jnp.dot(h, wu_ref[:, c0:c1], preferred_element_type=F32)
        a = (g * (1.0 / (1.0 + jnp.exp(-g))) * u).astype(BF16)
        d = jnp.dot(a, wd_ref[c0:c1, :], preferred_element_type=F32)
        acc = d if acc is None else acc + d
    o_ref[...] = x + 0.5 * acc


def _ffn(x, gain, wg, wu, wd, *, tm=512, fc=512):
    B, S, D = x.shape
    F = wg.shape[1]
    chunks = tuple((c, min(c + fc, F)) for c in range(0, F, fc))
    xspec = pl.BlockSpec((None, tm, D), lambda b, i: (b, i, 0))
    return pl.pallas_call(
        functools.partial(_ffn_kernel, chunks=chunks),
        out_shape=jax.ShapeDtypeStruct(x.shape, F32),
        grid=(B, S // tm),
        in_specs=[xspec, _resident((1, D)), _resident((D, F)), _resident((D, F)),
                  _resident((F, D))],
        out_specs=xspec,
        compiler_params=_cparams(("parallel", "parallel")),
        name="ffn",
    )(x, gain.reshape(1, D), wg, wu, wd)


_SEG_PLAN = {
    SEG_NA_Q: (0, 0, None), SEG_NA_K: (1, 0, None),
    SEG_DF_Q: (2, 1, "df"), SEG_DF_K: (3, 1, "df"),
    SEG_GQ_Q: (4, 0, "gq"), SEG_GQ_K: (5, 0, "gq"),
    SEG_DL_Q: (6, 0, "dl"), SEG_DL_K: (7, 0, "dl"),
}
_ROPE_HALF = {"df": DIFF_QK_DIM // ROPE_FRACTION // 2, "gq": HEAD_DIM // 4,
              "dl": HEAD_DIM // ROPE_FRACTION // 2}


def _inproj_kernel(x_ref, gn_ref, w_ref, gmat_ref, gains_ref,
                   cdf_ref, sdf_ref, cgq_ref, sgq_ref, cdl_ref, sdl_ref,
                   o_ref, vtdf_ref, vtgq_ref):
    h = _rms_rows(x_ref[...], gn_ref[...]).astype(BF16)
    lane = lax.broadcasted_iota(jnp.int32, (1, SEG), 1)
    tables = {"df": (cdf_ref, sdf_ref), "gq": (cgq_ref, sgq_ref), "dl": (cdl_ref, sdl_ref)}
    vt_plan = {SEG_DF_V: (vtdf_ref, tuple(range(DIFF_HEADS))),
               SEG_GQ_V: (vtgq_ref, tuple(2 * j for j in range(GQA_KV_HEADS)))}
    for seg in range(N_SEG):
        y = jnp.dot(h, w_ref[:, seg * SEG:(seg + 1) * SEG], preferred_element_type=F32)
        if seg in vt_plan:
            vt_ref, lane_heads = vt_plan[seg]
            yt = y.T
            ones = jnp.ones((VT_ROWS - HEAD_DIM, yt.shape[1]), BF16)
            for j, lh in enumerate(lane_heads):
                vt_ref[j * VT_ROWS:j * VT_ROWS + HEAD_DIM, :] = (
                    yt[lh * HEAD_DIM:(lh + 1) * HEAD_DIM].astype(BF16))
                vt_ref[j * VT_ROWS + HEAD_DIM:(j + 1) * VT_ROWS, :] = ones
        plan = _SEG_PLAN.get(seg)
        if plan is not None:
            gi, mi, rope = plan
            ms = _group_mean_sq(y, gmat_ref[mi])
            y = y * lax.rsqrt(ms + RMS_EPS) * gains_ref[gi:gi + 1, :]
            if rope is not None:
                half = _ROPE_HALF[rope]
                c_ref, s_ref = tables[rope]
                first = (lane & (2 * half - 1)) < half
                partner = jnp.where(first, pltpu.roll(y, SEG - half, 1), pltpu.roll(y, half, 1))
                y = y * c_ref[...] + partner * s_ref[...]
        o_ref[:, seg * SEG:(seg + 1) * SEG] = y.astype(BF16)


def _inproj(x, gain, w_ext, gmats, gains, tables, *, tm=512):
    B, S, D = x.shape
    tspec = pl.BlockSpec((tm, SEG), lambda i, b: (i, 0))
    vt_rows = (DIFF_HEADS * VT_ROWS, GQA_KV_HEADS * VT_ROWS)
    return pl.pallas_call(
        _inproj_kernel,
        out_shape=(jax.ShapeDtypeStruct((B, S, EXT_WIDTH), BF16),)
        + tuple(jax.ShapeDtypeStruct((B, S // tm, r, tm), BF16) for r in vt_rows),
        grid=(S // tm, B),
        in_specs=[pl.BlockSpec((None, tm, D), lambda i, b: (b, i, 0)),
                  _resident((1, D)), _resident((D, EXT_WIDTH)),
                  _resident(gmats.shape), _resident(gains.shape)] + [tspec] * 6,
        out_specs=(pl.BlockSpec((None, tm, EXT_WIDTH), lambda i, b: (b, i, 0)),)
        + tuple(pl.BlockSpec((None, None, r, tm), lambda i, b: (b, i, 0, 0)) for r in vt_rows),
        compiler_params=_cparams(("parallel", "parallel")),
        name="inproj",
    )(x, gain.reshape(1, D), w_ext, gmats, gains, *tables)


def _flash_kernel(q_ref, k_ref, vt_ref, gmat_ref, ogain_ref, lamp_ref, li_ref, o_ref,
                  qmt_ref, m_ref, acc_ref, sbuf_ref, *, groups, tq, tk, diff, vheads):
    gw = SEG // groups
    qt = q_ref[...].astype(F32).T
    row_g = lax.broadcasted_iota(jnp.int32, (SEG, 1), 0) >> int(math.log2(gw))
    for r in range(groups):
        qmt_ref[:, r * tq:(r + 1) * tq] = jnp.where(row_g == r, qt, 0.0).astype(BF16)
    m_ref[...] = jnp.full(m_ref.shape, MASK_VALUE, F32)
    acc_ref[...] = jnp.zeros(acc_ref.shape, F32)
    ncol = groups * tq // COL_BLOCK

    nkv = k_ref.shape[0] // tk
    ahead = sbuf_ref.shape[0]

    def k_block(kv):
        return k_ref[pl.ds(pl.multiple_of(kv * tk, tk), tk), :]

    def qk(k, c):
        return jnp.dot(k, qmt_ref[:, c * COL_BLOCK:(c + 1) * COL_BLOCK],
                       preferred_element_type=F32)

    k0 = k_block(0)
    for c in range(ahead):
        sbuf_ref[c] = qk(k0, c)

    def body(kv, carry):
        k = k_block(kv)
        k_next = k_block(jnp.minimum(kv + 1, nkv - 1))
        vt = vt_ref[kv]
        pend = [sbuf_ref[c] for c in range(ahead)]
        for c in range(ncol):
            hv = (c * COL_BLOCK // tq) * vheads // groups
            cs = slice(c * COL_BLOCK, (c + 1) * COL_BLOCK)
            s = pend.pop(0)
            if c + ahead < ncol:
                pend.append(qk(k, c + ahead))
            else:
                sbuf_ref[c + ahead - ncol] = qk(k_next, c + ahead - ncol)
            m_prev = m_ref[:, cs]
            m_new = jnp.maximum(m_prev, jnp.max(s, axis=0, keepdims=True))
            alpha = jnp.exp2(m_prev - m_new)
            p = jnp.exp2(s - m_new).astype(BF16)
            pv = jnp.dot(vt[hv * VT_ROWS:(hv + 1) * VT_ROWS], p, preferred_element_type=F32)
            acc_ref[:, cs] = acc_ref[:, cs] * alpha + pv
            m_ref[:, cs] = m_new
        return carry

    lax.fori_loop(0, nkv, body, 0)

    acc = acc_ref[...]
    norm = acc[0:HEAD_DIM] * (1.0 / acc[HEAD_DIM:HEAD_DIM + 1])
    if diff:
        lp = lamp_ref[...]
        lam = (jnp.exp(jnp.sum(lp[0:1] * lp[1:2], axis=1, keepdims=True))
               - jnp.exp(jnp.sum(lp[2:3] * lp[3:4], axis=1, keepdims=True)) + li_ref[...])
        pieces = [norm[:, 2 * hd * tq:(2 * hd + 1) * tq]
                  - lam * norm[:, (2 * hd + 1) * tq:(2 * hd + 2) * tq] for hd in range(groups // 2)]
    else:
        pieces = [norm[:, r * tq:(r + 1) * tq] for r in range(groups)]
    out = jnp.concatenate(pieces, axis=0).T
    if diff:
        ms = _group_mean_sq(out, gmat_ref[...])
        out = out * lax.rsqrt(ms + RMS_EPS) * ogain_ref[...] * (1.0 - li_ref[...])
    o_ref[...] = out.astype(BF16)


def _flash(qkv, vt, segs, gmat, ogain, lamp, li, *, groups, diff, cols=2048):
    B, S, _ = qkv.shape
    _, nkv, vrows, tk = vt.shape
    tq = cols // groups
    qs, ks = segs
    return pl.pallas_call(
        functools.partial(_flash_kernel, groups=groups, tq=tq, tk=tk, diff=diff,
                          vheads=vrows // VT_ROWS),
        out_shape=jax.ShapeDtypeStruct((B, S, SEG), BF16),
        grid=(B, S // tq),
        in_specs=[pl.BlockSpec((None, tq, SEG), lambda b, i: (b, i, qs)),
                  pl.BlockSpec((None, S, SEG), lambda b, i: (b, 0, ks)),
                  pl.BlockSpec((None, nkv, vrows, tk), lambda b, i: (b, 0, 0, 0)),
                  _resident(gmat.shape), _resident(ogain.shape), _resident(lamp.shape),
                  _resident(li.shape)],
        out_specs=pl.BlockSpec((None, tq, SEG), lambda b, i: (b, i, 0)),
        scratch_shapes=[pltpu.VMEM((SEG, cols), BF16), pltpu.VMEM((1, cols), F32),
                        pltpu.VMEM((VT_ROWS, cols), F32),
                        pltpu.VMEM((QK_AHEAD, tk, COL_BLOCK), F32)],
        compiler_params=_cparams(("parallel", "parallel")),
        name="diff_attn" if diff else "gqa_attn",
    )(qkv, qkv, vt, gmat, ogain, lamp, li)


def _na_kernel(q_ref, k_ref, v_ref, tab_ref, o_ref, *, nq, nk):
    i = pl.program_id(2)
    seq = k_ref.shape[0]
    start = jnp.clip(i * nq - (nk - nq) // 2, 0, seq - nk)
    start = pl.multiple_of(start, (nk - nq) // 2)
    kw = k_ref[pl.ds(start, nk), :]
    vw = v_ref[pl.ds(start, nk), :]
    q = q_ref[...]
    lane_h = lax.broadcasted_iota(jnp.int32, (1, 2 * HEAD_DIM), 1) >> int(math.log2(HEAD_DIM))
    outs = []
    for j in range(2):
        qm = jnp.where(lane_h == j, q, jnp.zeros_like(q))
        s = _dot_nt(qm, kw) + tab_ref[j]
        m = jnp.max(s, axis=1, keepdims=True)
        p = jnp.exp2(s - m)
        l = jnp.sum(p, axis=1, keepdims=True)
        outs.append(jnp.dot(p.astype(BF16), vw, preferred_element_type=F32) * (1.0 / l))
    o_ref[...] = jnp.where(lane_h == 0, outs[0], outs[1]).astype(BF16)


def _na_table(rel_bias, rows):
    kh = min(NA_WIN_H, rows)
    n_ro, n_co = 2 * NA_WIN_H - 1, 2 * NA_WIN_W - 1
    c = np.arange(GRID_W)[:, None]
    kc = np.arange(GRID_W)[None, :]
    cs = np.clip(c - NA_WIN_W // 2, 0, GRID_W - NA_WIN_W)
    col_valid = (kc >= cs) & (kc < cs + NA_WIN_W)
    col_sel = (kc - c + (NA_WIN_W - 1))[..., None] == np.arange(n_co)
    row_sel, row_valid = [], []
    for r0 in (0, NA_QROWS, rows - NA_QROWS):
        kr0 = int(np.clip(r0 - (NA_KROWS - NA_QROWS) // 2, 0, rows - NA_KROWS))
        r = r0 + np.arange(NA_QROWS)[:, None]
        kr = kr0 + np.arange(NA_KROWS)[None, :]
        rs = np.clip(r - kh // 2, 0, rows - kh)
        row_valid.append((kr >= rs) & (kr < rs + kh))
        row_sel.append((kr - r + (NA_WIN_H - 1))[..., None] == np.arange(n_ro))
    row_sel = jnp.asarray(np.stack(row_sel), F32)
    valid = (np.stack(row_valid)[:, None, :, None, :, None]
             & col_valid[None, None, None, :, None, :])
    hi = lax.Precision.HIGHEST
    cols = jnp.einsum("hab,cdb->hacd", rel_bias.astype(F32), jnp.asarray(col_sel, F32),
                      precision=hi)
    full = jnp.einsum("vqka,hacd->vhqckd", row_sel, cols, precision=hi)
    table = jnp.where(valid, full * LOG2E, MASK_VALUE)
    return table.reshape(3, rel_bias.shape[0], NA_QROWS * GRID_W, NA_KROWS * GRID_W)


def _na(qkv, table):
    B, S, _ = qkv.shape
    nq, nk = NA_QROWS * GRID_W, NA_KROWS * GRID_W
    nblk = S // nq
    hp = 2 * HEAD_DIM

    def variant(i):
        return jnp.where(i == 0, 0, jnp.where(i == nblk - 1, 2, 1))

    return pl.pallas_call(
        functools.partial(_na_kernel, nq=nq, nk=nk),
        out_shape=jax.ShapeDtypeStruct((B, S, SEG), BF16),
        grid=(2, B, nblk),
        in_specs=[pl.BlockSpec((None, nq, hp), lambda p, b, i: (b, i, 2 * SEG_NA_Q + p)),
                  pl.BlockSpec((None, S, hp), lambda p, b, i: (b, 0, 2 * SEG_NA_K + p)),
                  pl.BlockSpec((None, S, hp), lambda p, b, i: (b, 0, 2 * SEG_NA_V + p)),
                  pl.BlockSpec((None, 2, nq, nk), lambda p, b, i: (variant(i), p, 0, 0))],
        out_specs=pl.BlockSpec((None, nq, hp), lambda p, b, i: (b, i, p)),
        compiler_params=_cparams(("parallel", "parallel", "arbitrary")),
        name="na_attn",
    )(qkv, qkv, qkv, table)


def _dil_kernel(q_ref, k_ref, v_ref, o_ref, lse_ref, *, tq, halo):
    j = pl.program_id(2)
    length = k_ref.shape[0]
    win = tq + 2 * halo
    start = pl.multiple_of(jnp.clip(j * tq - halo, 0, length - win), halo)
    kw = k_ref[pl.ds(start, win), :]
    vw = v_ref[pl.ds(start, win), :]
    q = q_ref[...]
    qpos = j * tq + lax.broadcasted_iota(jnp.int32, (tq, win), 0)
    kpos = start + lax.broadcasted_iota(jnp.int32, (tq, win), 1)
    bias = jnp.where(jnp.abs(kpos - qpos) <= halo, 0.0, MASK_VALUE)
    lane_h = lax.broadcasted_iota(jnp.int32, (1, SEG), 1) >> int(math.log2(HEAD_DIM))
    out = jnp.zeros((tq, SEG), F32)
    lse = jnp.zeros((tq, SEG), F32)
    for hd in range(DIL_HEADS):
        qm = jnp.where(lane_h == hd, q, jnp.zeros_like(q))
        s = _dot_nt(qm, kw) + bias
        m = jnp.max(s, axis=1, keepdims=True)
        p = jnp.exp2(s - m)
        l = jnp.sum(p, axis=1, keepdims=True)
        o = jnp.dot(p.astype(BF16), vw, preferred_element_type=F32) * (1.0 / l)
        out = jnp.where(lane_h == hd, o, out)
        lse = jnp.where(lane_h == hd, m + jnp.log(l) * LOG2E, lse)
    o_ref[...] = out
    lse_ref[...] = lse


def _dilated(qkv, dil, halo, *, tq=256):
    B, S, _ = qkv.shape
    length = S // dil
    view = qkv.reshape(B, length, dil * EXT_WIDTH)
    oshape = jax.ShapeDtypeStruct((B, length, dil * SEG), F32)
    ospec = pl.BlockSpec((None, tq, SEG), lambda b, r, j: (b, j, r))
    out, lse = pl.pallas_call(
        functools.partial(_dil_kernel, tq=tq, halo=halo),
        out_shape=(oshape, oshape),
        grid=(B, dil, length // tq),
        in_specs=[pl.BlockSpec((None, tq, SEG), lambda b, r, j: (b, j, r * N_SEG + SEG_DL_Q)),
                  pl.BlockSpec((None, length, SEG), lambda b, r, j: (b, 0, r * N_SEG + SEG_DL_K)),
                  pl.BlockSpec((None, length, SEG), lambda b, r, j: (b, 0, r * N_SEG + SEG_DL_V))],
        out_specs=(ospec, ospec),
        compiler_params=_cparams(("parallel", "parallel", "arbitrary")),
        name=f"dil_attn_{dil}",
    )(view, view, view)
    return out.reshape(B, S, SEG), lse.reshape(B, S, SEG)


def _outproj_kernel(x_ref, oa_ref, ob_ref, oc_ref, d0_ref, d1_ref, d2_ref,
                    l0_ref, l1_ref, l2_ref, w_ref, o_ref, cat_ref):
    l0, l1, l2 = l0_ref[...], l1_ref[...], l2_ref[...]
    m = jnp.maximum(jnp.maximum(l0, l1), l2)
    e0, e1, e2 = jnp.exp2(l0 - m), jnp.exp2(l1 - m), jnp.exp2(l2 - m)
    od = (e0 * d0_ref[...] + e1 * d1_ref[...] + e2 * d2_ref[...]) * (1.0 / (e0 + e1 + e2))
    cat_ref[:, 0 * SEG:1 * SEG] = oa_ref[...]
    cat_ref[:, 1 * SEG:2 * SEG] = ob_ref[...]
    cat_ref[:, 2 * SEG:3 * SEG] = oc_ref[...]
    cat_ref[:, 3 * SEG:4 * SEG] = od.astype(BF16)
    o_ref[...] = x_ref[...] + jnp.dot(cat_ref[...], w_ref[...], preferred_element_type=F32)


def _outproj(x, oa, ob, oc, dils, w_out, *, tm=512):
    B, S, D = x.shape
    xspec = pl.BlockSpec((None, tm, D), lambda b, i: (b, i, 0))
    sspec = pl.BlockSpec((None, tm, SEG), lambda b, i: (b, i, 0))
    outs = [d[0] for d in dils]
    lses = [d[1] for d in dils]
    return pl.pallas_call(
        _outproj_kernel,
        out_shape=jax.ShapeDtypeStruct(x.shape, F32),
        grid=(B, S // tm),
        in_specs=[xspec] + [sspec] * 9 + [_resident(w_out.shape)],
        out_specs=xspec,
        scratch_shapes=[pltpu.VMEM((tm, 4 * SEG), BF16)],
        compiler_params=_cparams(("parallel", "parallel")),
        name="outproj",
    )(x, oa, ob, oc, *outs, *lses, w_out)


def _rope_cs(pos, dim, theta):
    inv_freq = 1.0 / (theta ** (jnp.arange(0, dim, 2, dtype=F32) / dim))
    ang = pos.astype(F32)[:, None] * inv_freq[None, :]
    return jnp.cos(ang), jnp.sin(ang)


def _rope_tables(S):
    pos = jnp.arange(S, dtype=jnp.int32)

    def expand(parts_c, parts_s, reps):
        return (jnp.tile(jnp.concatenate(parts_c, axis=1), (1, reps)),
                jnp.tile(jnp.concatenate(parts_s, axis=1), (1, reps)))

    c, s = _rope_cs(pos, DIFF_QK_DIM // ROPE_FRACTION, ROPE_THETA)
    rest = DIFF_QK_DIM - 2 * c.shape[1]
    df = expand([c, c, jnp.ones((S, rest), F32)], [-s, s, jnp.zeros((S, rest), F32)],
                SEG // DIFF_QK_DIM)
    c, s = _rope_cs(pos, HEAD_DIM // ROPE_FRACTION, ROPE_THETA)
    rest = HEAD_DIM - 2 * c.shape[1]
    dl = expand([c, c, jnp.ones((S, rest), F32)], [-s, s, jnp.zeros((S, rest), F32)],
                SEG // HEAD_DIM)
    cr, sr = _rope_cs(pos // GRID_W, HEAD_DIM // 2, AXIAL_THETA)
    cc, sc = _rope_cs(pos % GRID_W, HEAD_DIM // 2, AXIAL_THETA)
    gq = expand([cr, cr, cc, cc], [-sr, sr, -sc, sc], SEG // HEAD_DIM)
    return df + gq + dl


def _group_matrices():
    lane = np.arange(SEG)
    mats = [(lane[:, None] // g == lane[None, :] // g).astype(np.float32) / g
            for g in (HEAD_DIM, DIFF_QK_DIM)]
    return jnp.asarray(np.stack(mats), BF16)


def _extend_w_in(w_in):
    kv0 = 7 * SEG
    hd = HEAD_DIM
    cols = np.concatenate([
        np.arange(0, kv0),
        kv0 + np.array([0, 0, 1, 1]).repeat(hd) * hd + np.tile(np.arange(hd), 4),
        kv0 + 2 * hd + np.array([0, 0, 1, 1]).repeat(hd) * hd + np.tile(np.arange(hd), 4),
        np.arange(kv0 + 4 * hd, w_in.shape[1]),
    ])
    return w_in[:, cols].astype(BF16)


def _gain_rows(na_q, na_k, df_q, df_k, gq_q, gq_k, dl_q, dl_k):
    def row(g, scale):
        return jnp.tile(g.astype(F32), SEG // g.shape[0]) * scale
    sc64 = HEAD_DIM ** -0.5 * LOG2E
    sc32 = DIFF_QK_DIM ** -0.5 * LOG2E
    return jnp.stack([row(na_q, sc64), row(na_k, 1.0), row(df_q, sc32), row(df_k, 1.0),
                      row(gq_q, sc64), row(gq_k, 1.0), row(dl_q, sc64), row(dl_k, 1.0)])


def kernel(x, ffn1_norm, ffn1_w_gate, ffn1_w_up, ffn1_w_down, mix_norm, w_in, w_out, na_q_norm, na_k_norm, na_rel_bias, diff_q_norm, diff_k_norm, diff_lambda_q1, diff_lambda_k1, diff_lambda_q2, diff_lambda_k2, diff_out_norm, gqa_q_norm, gqa_k_norm, dil_q_norm, dil_k_norm, ffn2_norm, ffn2_w_gate, ffn2_w_up, ffn2_w_down):
    B, S, D = x.shape
    depth = w_in.shape[0]
    rows = S // GRID_W
    tables = _rope_tables(S)
    gmats = _group_matrices()
    ones_row = jnp.ones((1, SEG), F32)
    zero_lamp = jnp.zeros((4, DIFF_QK_DIM), F32)
    zero_li = jnp.zeros((1, 1), F32)

    for l in range(depth):
        x = _ffn(x, ffn1_norm[l], ffn1_w_gate[l].astype(BF16), ffn1_w_up[l].astype(BF16),
                 ffn1_w_down[l].astype(BF16))

        gains = _gain_rows(na_q_norm[l], na_k_norm[l], diff_q_norm[l], diff_k_norm[l],
                           gqa_q_norm[l], gqa_k_norm[l], dil_q_norm[l], dil_k_norm[l])
        qkv, vt_df, vt_gq = _inproj(x, mix_norm[l], _extend_w_in(w_in[l]), gmats, gains, tables)

        o_a = _na(qkv, _na_table(na_rel_bias[l], rows))
        lamp = jnp.stack([diff_lambda_q1[l], diff_lambda_k1[l], diff_lambda_q2[l],
                          diff_lambda_k2[l]]).astype(F32)
        li = jnp.full((1, 1), 0.8 - 0.6 * math.exp(-0.3 * l), F32)
        ogain = jnp.tile(diff_out_norm[l].astype(F32), SEG // HEAD_DIM).reshape(1, SEG)
        o_b = _flash(qkv, vt_df, (SEG_DF_Q, SEG_DF_K), gmats[0], ogain, lamp, li,
                     groups=2 * DIFF_HEADS, diff=True)
        o_c = _flash(qkv, vt_gq, (SEG_GQ_Q, SEG_GQ_K), gmats[0], ones_row, zero_lamp,
                     zero_li, groups=GQA_Q_HEADS, diff=False)
        dils = [_dilated(qkv, dil, (window // 2) // dil) for window, dil in DIL_PAIRS]
        x = _outproj(x, o_a, o_b, o_c, dils, w_out[l].astype(BF16))

        x = _ffn(x, ffn2_norm[l], ffn2_w_gate[l].astype(BF16), ffn2_w_up[l].astype(BF16),
                 ffn2_w_down[l].astype(BF16))
    return x
```

```python
import functools
import math

import numpy as np
import jax
import jax.numpy as jnp
from jax import lax
from jax.experimental import pallas as pl
from jax.experimental.pallas import tpu as pltpu

F32 = jnp.float32
BF16 = jnp.bfloat16

GRID_W = 64
HEAD_DIM = 64
RMS_EPS = 1e-6
MASK_VALUE = -1e30
LOG2E = math.log2(math.e)

NA_HEADS = 4
NA_WIN_H = 8
NA_WIN_W = 16
DIFF_HEADS = 4
DIFF_QK_DIM = HEAD_DIM // 2
GQA_Q_HEADS = 4
GQA_KV_HEADS = 2
AXIAL_THETA = 10000.0
DIL_HEADS = 4
DIL_PAIRS = ((128, 1), (512, 4), (2048, 16))
DILATIONS = tuple(d for _, d in DIL_PAIRS if d > 1)
ROPE_THETA = 500000.0
ROPE_FRACTION = 4

LANES = 128
SEG = 256
(SEG_NA_Q, SEG_NA_K, SEG_NA_V, SEG_DF_Q, SEG_DF_K, SEG_DF_V,
 SEG_GQ_Q, SEG_GQ_K, SEG_GQ_V, SEG_DL_Q, SEG_DL_K, SEG_DL_V) = range(12)
N_SEG = 12
EXT_WIDTH = N_SEG * SEG

VMEM_LIMIT = 52 * 1024 * 1024

VT_ROWS = HEAD_DIM + 16
COL_BLOCK = 256
QK_AHEAD = 4

NA_QROWS = 8
NA_KROWS = 16


def _cparams(sem):
    return pltpu.CompilerParams(dimension_semantics=sem, vmem_limit_bytes=VMEM_LIMIT)


def _resident(shape):
    nd = len(shape)
    return pl.BlockSpec(shape, lambda *_: (0,) * nd, pipeline_mode=pl.Buffered(1))


def _rms_rows(x, gain):
    ms = jnp.mean(x * x, axis=-1, keepdims=True)
    return x * lax.rsqrt(ms + RMS_EPS) * gain


def _group_mean_sq(y, gmat):
    sq = y * y
    hi = sq.astype(BF16)
    lo = (sq - hi.astype(F32)).astype(BF16)
    return (jnp.dot(hi, gmat, preferred_element_type=F32)
            + jnp.dot(lo, gmat, preferred_element_type=F32))


def _dot_nt(a, b):
    return lax.dot_general(a, b, (((1,), (1,)), ((), ())), preferred_element_type=F32)


def _ffn_kernel(x_ref, g_ref, wg_ref, wu_ref, wd_ref, o_ref, *, chunks):
    x = x_ref[...]
    h = _rms_rows(x, g_ref[...]).astype(BF16)
    acc = None
    for c0, c1 in chunks:
        g = jnp.dot(h, wg_ref[:, c0:c1], preferred_element_type=F32)
        u = jnp.dot(h, wu_ref[:, c0:c1], preferred_element_type=F32)
        a = (g * (1.0 / (1.0 + jnp.exp(-g))) * u).astype(BF16)
        d = jnp.dot(a, wd_ref[c0:c1, :], preferred_element_type=F32)
        acc = d if acc is None else acc + d
    o_ref[...] = x + 0.5 * acc


def _ffn(x, gain, wg, wu, wd, *, tm=512, fc=512):
    B, S, D = x.shape
    F = wg.shape[1]
    chunks = tuple((c, min(c + fc, F)) for c in range(0, F, fc))
    xspec = pl.BlockSpec((None, tm, D), lambda b, i: (b, i, 0))
    return pl.pallas_call(
        functools.partial(_ffn_kernel, chunks=chunks),
        out_shape=jax.ShapeDtypeStruct(x.shape, F32),
        grid=(B, S // tm),
        in_specs=[xspec, _resident((1, D)), _resident((D, F)), _resident((D, F)),
                  _resident((F, D))],
        out_specs=xspec,
        compiler_params=_cparams(("parallel", "parallel")),
        name="ffn",
    )(x, gain.reshape(1, D), wg, wu, wd)


_SEG_PLAN = {
    SEG_NA_Q: (0, 0, None), SEG_NA_K: (1, 0, None),
    SEG_DF_Q: (2, 1, "df"), SEG_DF_K: (3, 1, "df"),
    SEG_GQ_Q: (4, 0, "gq"), SEG_GQ_K: (5, 0, "gq"),
    SEG_DL_Q: (6, 0, "dl"), SEG_DL_K: (7, 0, "dl"),
}
_ROPE_HALF = {"df": DIFF_QK_DIM // ROPE_FRACTION // 2, "gq": HEAD_DIM // 4,
              "dl": HEAD_DIM // ROPE_FRACTION // 2}


def _inproj_kernel(x_ref, gn_ref, w_ref, gmat_ref, gains_ref,
                   cdf_ref, sdf_ref, cgq_ref, sgq_ref, cdl_ref, sdl_ref,
                   o_ref, vtdf_ref, vtgq_ref, *dl_refs_and_scratch):
    *dl_refs, ys_ref = dl_refs_and_scratch
    tm = x_ref.shape[0]
    h = _rms_rows(x_ref[...], gn_ref[...]).astype(BF16)
    lane = lax.broadcasted_iota(jnp.int32, (1, SEG), 1)
    tables = {"df": (cdf_ref, sdf_ref), "gq": (cgq_ref, sgq_ref), "dl": (cdl_ref, sdl_ref)}
    vt_plan = {SEG_DF_V: (vtdf_ref, tuple(range(DIFF_HEADS))),
               SEG_GQ_V: (vtgq_ref, tuple(2 * j for j in range(GQA_KV_HEADS)))}
    for seg in range(N_SEG):
        y = jnp.dot(h, w_ref[:, seg * SEG:(seg + 1) * SEG], preferred_element_type=F32)
        if seg in vt_plan:
            vt_ref, lane_heads = vt_plan[seg]
            yt = y.T
            ones = jnp.ones((VT_ROWS - HEAD_DIM, yt.shape[1]), BF16)
            for j, lh in enumerate(lane_heads):
                vt_ref[j * VT_ROWS:j * VT_ROWS + HEAD_DIM, :] = (
                    yt[lh * HEAD_DIM:(lh + 1) * HEAD_DIM].astype(BF16))
                vt_ref[j * VT_ROWS + HEAD_DIM:(j + 1) * VT_ROWS, :] = ones
        plan = _SEG_PLAN.get(seg)
        if plan is not None:
            gi, mi, rope = plan
            ms = _group_mean_sq(y, gmat_ref[mi])
            y = y * lax.rsqrt(ms + RMS_EPS) * gains_ref[gi:gi + 1, :]
            if rope is not None:
                half = _ROPE_HALF[rope]
                c_ref, s_ref = tables[rope]
                first = (lane & (2 * half - 1)) < half
                partner = jnp.where(first, pltpu.roll(y, SEG - half, 1), pltpu.roll(y, half, 1))
                y = y * c_ref[...] + partner * s_ref[...]
        o_ref[:, seg * SEG:(seg + 1) * SEG] = y.astype(BF16)
        if SEG_DL_Q <= seg <= SEG_DL_V:
            for hf in range(SEG // LANES):
                ys_ref[hf] = y[:, hf * LANES:(hf + 1) * LANES]
            col = (seg - SEG_DL_Q) * SEG
            for dl_ref, dil in zip(dl_refs, DILATIONS):
                for r in range(dil):
                    for hf in range(SEG // LANES):
                        dl_ref[r, :, col + hf * LANES:col + (hf + 1) * LANES] = (
                            ys_ref[hf, pl.ds(r, tm // dil, stride=dil), :].astype(BF16))


def _inproj(x, gain, w_ext, gmats, gains, tables, *, tm=512):
    B, S, D = x.shape
    tspec = pl.BlockSpec((tm, SEG), lambda i, b: (i, 0))
    vt_rows = (DIFF_HEADS * VT_ROWS, GQA_KV_HEADS * VT_ROWS)
    return pl.pallas_call(
        _inproj_kernel,
        out_shape=(jax.ShapeDtypeStruct((B, S, EXT_WIDTH), BF16),)
        + tuple(jax.ShapeDtypeStruct((B, S // tm, r, tm), BF16) for r in vt_rows)
        + tuple(jax.ShapeDtypeStruct((B, d, S // d, 3 * SEG), BF16) for d in DILATIONS),
        grid=(S // tm, B),
        in_specs=[pl.BlockSpec((None, tm, D), lambda i, b: (b, i, 0)),
                  _resident((1, D)), _resident((D, EXT_WIDTH)),
                  _resident(gmats.shape), _resident(gains.shape)] + [tspec] * 6,
        out_specs=(pl.BlockSpec((None, tm, EXT_WIDTH), lambda i, b: (b, i, 0)),)
        + tuple(pl.BlockSpec((None, None, r, tm), lambda i, b: (b, i, 0, 0)) for r in vt_rows)
        + tuple(pl.BlockSpec((None, d, tm // d, 3 * SEG), lambda i, b: (b, 0, i, 0))
                for d in DILATIONS),
        scratch_shapes=[pltpu.VMEM((SEG // LANES, tm, LANES), F32)],
        compiler_params=_cparams(("parallel", "parallel")),
        name="inproj",
    )(x, gain.reshape(1, D), w_ext, gmats, gains, *tables)


def _flash_kernel(q_ref, k_ref, vt_ref, gmat_ref, ogain_ref, lamp_ref, li_ref, o_ref,
                  qmt_ref, m_ref, acc_ref, sbuf_ref, *, groups, tq, tk, diff, vheads):
    gw = SEG // groups
    qt = q_ref[...].astype(F32).T
    row_g = lax.broadcasted_iota(jnp.int32, (SEG, 1), 0) >> int(math.log2(gw))
    for r in range(groups):
        qmt_ref[:, r * tq:(r + 1) * tq] = jnp.where(row_g == r, qt, 0.0).astype(BF16)
    m_ref[...] = jnp.full(m_ref.shape, MASK_VALUE, F32)
    acc_ref[...] = jnp.zeros(acc_ref.shape, F32)
    ncol = groups * tq // COL_BLOCK

    nkv = k_ref.shape[0] // tk
    ahead = sbuf_ref.shape[0]

    def k_block(kv):
        return k_ref[pl.ds(pl.multiple_of(kv * tk, tk), tk), :]

    def qk(k, c):
        return jnp.dot(k, qmt_ref[:, c * COL_BLOCK:(c + 1) * COL_BLOCK],
                       preferred_element_type=F32)

    k0 = k_block(0)
    for c in range(ahead):
        sbuf_ref[c] = qk(k0, c)

    def body(kv, carry):
        k = k_block(kv)
        k_next = k_block(jnp.minimum(kv + 1, nkv - 1))
        vt = vt_ref[kv]
        pend = [sbuf_ref[c] for c in range(ahead)]
        for c in range(ncol):
            hv = (c * COL_BLOCK // tq) * vheads // groups
            cs = slice(c * COL_BLOCK, (c + 1) * COL_BLOCK)
            s = pend.pop(0)
            if c + ahead < ncol:
                pend.append(qk(k, c + ahead))
            else:
                sbuf_ref[c + ahead - ncol] = qk(k_next, c + ahead - ncol)
            m_prev = m_ref[:, cs]
            m_new = jnp.maximum(m_prev, jnp.max(s, axis=0, keepdims=True))
            alpha = jnp.exp2(m_prev - m_new)
            p = jnp.exp2(s - m_new).astype(BF16)
            pv = jnp.dot(vt[hv * VT_ROWS:(hv + 1) * VT_ROWS], p, preferred_element_type=F32)
            acc_ref[:, cs] = acc_ref[:, cs] * alpha + pv
            m_ref[:, cs] = m_new
        return carry

    lax.fori_loop(0, nkv, body, 0)

    acc = acc_ref[...]
    norm = acc[0:HEAD_DIM] * (1.0 / acc[HEAD_DIM:HEAD_DIM + 1])
    if diff:
        lp = lamp_ref[...]
        lam = (jnp.exp(jnp.sum(lp[0:1] * lp[1:2], axis=1, keepdims=True))
               - jnp.exp(jnp.sum(lp[2:3] * lp[3:4], axis=1, keepdims=True)) + li_ref[...])
        pieces = [norm[:, 2 * hd * tq:(2 * hd + 1) * tq]
                  - lam * norm[:, (2 * hd + 1) * tq:(2 * hd + 2) * tq] for hd in range(groups // 2)]
    else:
        pieces = [norm[:, r * tq:(r + 1) * tq] for r in range(groups)]
    out = jnp.concatenate(pieces, axis=0).T
    if diff:
        ms = _group_mean_sq(out, gmat_ref[...])
        out = out * lax.rsqrt(ms + RMS_EPS) * ogain_ref[...] * (1.0 - li_ref[...])
    o_ref[...] = out.astype(BF16)


def _flash(qkv, vt, segs, gmat, ogain, lamp, li, *, groups, diff, cols=2048):
    B, S, _ = qkv.shape
    _, nkv, vrows, tk = vt.shape
    tq = cols // groups
    qs, ks = segs
    return pl.pallas_call(
        functools.partial(_flash_kernel, groups=groups, tq=tq, tk=tk, diff=diff,
                          vheads=vrows // VT_ROWS),
        out_shape=jax.ShapeDtypeStruct((B, S, SEG), BF16),
        grid=(B, S // tq),
        in_specs=[pl.BlockSpec((None, tq, SEG), lambda b, i: (b, i, qs)),
                  pl.BlockSpec((None, S, SEG), lambda b, i: (b, 0, ks)),
                  pl.BlockSpec((None, nkv, vrows, tk), lambda b, i: (b, 0, 0, 0)),
                  _resident(gmat.shape), _resident(ogain.shape), _resident(lamp.shape),
                  _resident(li.shape)],
        out_specs=pl.BlockSpec((None, tq, SEG), lambda b, i: (b, i, 0)),
        scratch_shapes=[pltpu.VMEM((SEG, cols), BF16), pltpu.VMEM((1, cols), F32),
                        pltpu.VMEM((VT_ROWS, cols), F32),
                        pltpu.VMEM((QK_AHEAD, tk, COL_BLOCK), F32)],
        compiler_params=_cparams(("parallel", "parallel")),
        name="diff_attn" if diff else "gqa_attn",
    )(qkv, qkv, vt, gmat, ogain, lamp, li)


def _na_kernel(q_ref, k_ref, v_ref, tab_ref, o_ref, *, nq, nk):
    i = pl.program_id(2)
    seq = k_ref.shape[0]
    start = jnp.clip(i * nq - (nk - nq) // 2, 0, seq - nk)
    start = pl.multiple_of(start, (nk - nq) // 2)
    kw = k_ref[pl.ds(start, nk), :]
    vw = v_ref[pl.ds(start, nk), :]
    q = q_ref[...]
    lane_h = lax.broadcasted_iota(jnp.int32, (1, 2 * HEAD_DIM), 1) >> int(math.log2(HEAD_DIM))
    outs = []
    for j in range(2):
        qm = jnp.where(lane_h == j, q, jnp.zeros_like(q))
        s = _dot_nt(qm, kw) + tab_ref[j]
        m = jnp.max(s, axis=1, keepdims=True)
        p = jnp.exp2(s - m)
        l = jnp.sum(p, axis=1, keepdims=True)
        outs.append(jnp.dot(p.astype(BF16), vw, preferred_element_type=F32) * (1.0 / l))
    o_ref[...] = jnp.where(lane_h == 0, outs[0], outs[1]).astype(BF16)


def _na_table(rel_bias, rows):
    kh = min(NA_WIN_H, rows)
    n_ro, n_co = 2 * NA_WIN_H - 1, 2 * NA_WIN_W - 1
    c = np.arange(GRID_W)[:, None]
    kc = np.arange(GRID_W)[None, :]
    cs = np.clip(c - NA_WIN_W // 2, 0, GRID_W - NA_WIN_W)
    col_valid = (kc >= cs) & (kc < cs + NA_WIN_W)
    col_sel = (kc - c + (NA_WIN_W - 1))[..., None] == np.arange(n_co)
    row_sel, row_valid = [], []
    for r0 in (0, NA_QROWS, rows - NA_QROWS):
        kr0 = int(np.clip(r0 - (NA_KROWS - NA_QROWS) // 2, 0, rows - NA_KROWS))
        r = r0 + np.arange(NA_QROWS)[:, None]
        kr = kr0 + np.arange(NA_KROWS)[None, :]
        rs = np.clip(r - kh // 2, 0, rows - kh)
        row_valid.append((kr >= rs) & (kr < rs + kh))
        row_sel.append((kr - r + (NA_WIN_H - 1))[..., None] == np.arange(n_ro))
    row_sel = jnp.asarray(np.stack(row_sel), F32)
    valid = (np.stack(row_valid)[:, None, :, None, :, None]
             & col_valid[None, None, None, :, None, :])
    hi = lax.Precision.HIGHEST
    cols = jnp.einsum("hab,cdb->hacd", rel_bias.astype(F32), jnp.asarray(col_sel, F32),
                      precision=hi)
    full = jnp.einsum("vqka,hacd->vhqckd", row_sel, cols, precision=hi)
    table = jnp.where(valid, full * LOG2E, MASK_VALUE)
    return table.reshape(3, rel_bias.shape[0], NA_QROWS * GRID_W, NA_KROWS * GRID_W)


def _na(qkv, table):
    B, S, _ = qkv.shape
    nq, nk = NA_QROWS * GRID_W, NA_KROWS * GRID_W
    nblk = S // nq
    hp = 2 * HEAD_DIM

    def variant(i):
        return jnp.where(i == 0, 0, jnp.where(i == nblk - 1, 2, 1))

    return pl.pallas_call(
        functools.partial(_na_kernel, nq=nq, nk=nk),
        out_shape=jax.ShapeDtypeStruct((B, S, SEG), BF16),
        grid=(2, B, nblk),
        in_specs=[pl.BlockSpec((None, nq, hp), lambda p, b, i: (b, i, 2 * SEG_NA_Q + p)),
                  pl.BlockSpec((None, S, hp), lambda p, b, i: (b, 0, 2 * SEG_NA_K + p)),
                  pl.BlockSpec((None, S, hp), lambda p, b, i: (b, 0, 2 * SEG_NA_V + p)),
                  pl.BlockSpec((None, 2, nq, nk), lambda p, b, i: (variant(i), p, 0, 0))],
        out_specs=pl.BlockSpec((None, nq, hp), lambda p, b, i: (b, i, p)),
        compiler_params=_cparams(("parallel", "parallel", "arbitrary")),
        name="na_attn",
    )(qkv, qkv, qkv, table)


def _dil_kernel(q_ref, k_ref, v_ref, o_ref, lse_ref, *, tq, halo):
    j = pl.program_id(2)
    length = k_ref.shape[0]
    win = tq + 2 * halo
    start = pl.multiple_of(jnp.clip(j * tq - halo, 0, length - win), halo)
    kw = k_ref[pl.ds(start, win), :]
    vw = v_ref[pl.ds(start, win), :]
    q = q_ref[...]
    qpos = j * tq + lax.broadcasted_iota(jnp.int32, (tq, win), 0)
    kpos = start + lax.broadcasted_iota(jnp.int32, (tq, win), 1)
    bias = jnp.where(jnp.abs(kpos - qpos) <= halo, 0.0, MASK_VALUE)
    lane_h = lax.broadcasted_iota(jnp.int32, (1, SEG), 1) >> int(math.log2(HEAD_DIM))
    out = jnp.zeros((tq, SEG), F32)
    lse = jnp.zeros((tq, SEG), F32)
    for hd in range(DIL_HEADS):
        qm = jnp.where(lane_h == hd, q, jnp.zeros_like(q))
        s = _dot_nt(qm, kw) + bias
        m = jnp.max(s, axis=1, keepdims=True)
        p = jnp.exp2(s - m)
        l = jnp.sum(p, axis=1, keepdims=True)
        o = jnp.dot(p.astype(BF16), vw, preferred_element_type=F32) * (1.0 / l)
        out = jnp.where(lane_h == hd, o, out)
        lse = jnp.where(lane_h == hd, m + jnp.log(l) * LOG2E, lse)
    o_ref[...] = out
    lse_ref[...] = lse


def _dilated(src, first_seg, halo, *, tq=256):
    B, dil, length, _ = src.shape
    oshape = jax.ShapeDtypeStruct((B, dil, length, SEG), F32)
    ospec = pl.BlockSpec((None, None, tq, SEG), lambda b, r, j: (b, r, j, 0))
    return pl.pallas_call(
        functools.partial(_dil_kernel, tq=tq, halo=halo),
        out_shape=(oshape, oshape),
        grid=(B, dil, length // tq),
        in_specs=[pl.BlockSpec((None, None, tq, SEG), lambda b, r, j: (b, r, j, first_seg)),
                  pl.BlockSpec((None, None, length, SEG), lambda b, r, j: (b, r, 0, first_seg + 1)),
                  pl.BlockSpec((None, None, length, SEG), lambda b, r, j: (b, r, 0, first_seg + 2))],
        out_specs=(ospec, ospec),
        compiler_params=_cparams(("parallel", "parallel", "arbitrary")),
        name=f"dil_attn_{dil}",
    )(src, src, src)


def _outproj_kernel(x_ref, oa_ref, ob_ref, oc_ref, d0_ref, d1_ref, d2_ref,
                    l0_ref, l1_ref, l2_ref, w_ref, o_ref, cat_ref, *row_scratch):
    row_scratch = list(row_scratch)

    def token_rows(ref):
        dil = ref.shape[0]
        if dil == 1:
            return ref[0]
        buf = row_scratch.pop()
        for r in range(dil):
            slab = ref[r]
            for hf in range(SEG // LANES):
                buf[hf, pl.ds(r, ref.shape[1], stride=dil), :] = slab[:, hf * LANES:(hf + 1) * LANES]
        return jnp.concatenate([buf[hf] for hf in range(SEG // LANES)], axis=1)

    l0, l1, l2 = token_rows(l0_ref), token_rows(l1_ref), token_rows(l2_ref)
    m = jnp.maximum(jnp.maximum(l0, l1), l2)
    e0, e1, e2 = jnp.exp2(l0 - m), jnp.exp2(l1 - m), jnp.exp2(l2 - m)
    od = ((e0 * token_rows(d0_ref) + e1 * token_rows(d1_ref) + e2 * token_rows(d2_ref))
          * (1.0 / (e0 + e1 + e2)))
    cat_ref[:, 0 * SEG:1 * SEG] = oa_ref[...]
    cat_ref[:, 1 * SEG:2 * SEG] = ob_ref[...]
    cat_ref[:, 2 * SEG:3 * SEG] = oc_ref[...]
    cat_ref[:, 3 * SEG:4 * SEG] = od.astype(BF16)
    o_ref[...] = x_ref[...] + jnp.dot(cat_ref[...], w_ref[...], preferred_element_type=F32)


def _outproj(x, oa, ob, oc, dils, w_out, *, tm=512):
    B, S, D = x.shape
    xspec = pl.BlockSpec((None, tm, D), lambda b, i: (b, i, 0))
    sspec = pl.BlockSpec((None, tm, SEG), lambda b, i: (b, i, 0))
    outs = [d[0] for d in dils]
    lses = [d[1] for d in dils]
    dspecs = [pl.BlockSpec((None, a.shape[1], tm // a.shape[1], SEG), lambda b, i: (b, 0, i, 0))
              for a in outs + lses]
    n_scratch = sum(a.shape[1] > 1 for a in outs + lses)
    return pl.pallas_call(
        _outproj_kernel,
        out_shape=jax.ShapeDtypeStruct(x.shape, F32),
        grid=(B, S // tm),
        in_specs=[xspec] + [sspec] * 3 + dspecs + [_resident(w_out.shape)],
        out_specs=xspec,
        scratch_shapes=[pltpu.VMEM((tm, 4 * SEG), BF16)]
        + [pltpu.VMEM((SEG // LANES, tm, LANES), F32)] * n_scratch,
        compiler_params=_cparams(("parallel", "parallel")),
        name="outproj",
    )(x, oa, ob, oc, *outs, *lses, w_out)


def _rope_cs(pos, dim, theta):
    inv_freq = 1.0 / (theta ** (jnp.arange(0, dim, 2, dtype=F32) / dim))
    ang = pos.astype(F32)[:, None] * inv_freq[None, :]
    return jnp.cos(ang), jnp.sin(ang)


def _rope_tables(S):
    pos = jnp.arange(S, dtype=jnp.int32)

    def expand(parts_c, parts_s, reps):
        return (jnp.tile(jnp.concatenate(parts_c, axis=1), (1, reps)),
                jnp.tile(jnp.concatenate(parts_s, axis=1), (1, reps)))

    c, s = _rope_cs(pos, DIFF_QK_DIM // ROPE_FRACTION, ROPE_THETA)
    rest = DIFF_QK_DIM - 2 * c.shape[1]
    df = expand([c, c, jnp.ones((S, rest), F32)], [-s, s, jnp.zeros((S, rest), F32)],
                SEG // DIFF_QK_DIM)
    c, s = _rope_cs(pos, HEAD_DIM // ROPE_FRACTION, ROPE_THETA)
    rest = HEAD_DIM - 2 * c.shape[1]
    dl = expand([c, c, jnp.ones((S, rest), F32)], [-s, s, jnp.zeros((S, rest), F32)],
                SEG // HEAD_DIM)
    cr, sr = _rope_cs(pos // GRID_W, HEAD_DIM // 2, AXIAL_THETA)
    cc, sc = _rope_cs(pos % GRID_W, HEAD_DIM // 2, AXIAL_THETA)
    gq = expand([cr, cr, cc, cc], [-sr, sr, -sc, sc], SEG // HEAD_DIM)
    return df + gq + dl


def _group_matrices():
    lane = np.arange(SEG)
    mats = [(lane[:, None] // g == lane[None, :] // g).astype(np.float32) / g
            for g in (HEAD_DIM, DIFF_QK_DIM)]
    return jnp.asarray(np.stack(mats), BF16)


def _extend_w_in(w_in):
    kv0 = 7 * SEG
    hd = HEAD_DIM
    cols = np.concatenate([
        np.arange(0, kv0),
        kv0 + np.array([0, 0, 1, 1]).repeat(hd) * hd + np.tile(np.arange(hd), 4),
        kv0 + 2 * hd + np.array([0, 0, 1, 1]).repeat(hd) * hd + np.tile(np.arange(hd), 4),
        np.arange(kv0 + 4 * hd, w_in.shape[1]),
    ])
    return w_in[:, cols].astype(BF16)


def _gain_rows(na_q, na_k, df_q, df_k, gq_q, gq_k, dl_q, dl_k):
    def row(g, scale):
        return jnp.tile(g.astype(F32), SEG // g.shape[0]) * scale
    sc64 = HEAD_DIM ** -0.5 * LOG2E
    sc32 = DIFF_QK_DIM ** -0.5 * LOG2E
    return jnp.stack([row(na_q, sc64), row(na_k, 1.0), row(df_q, sc32), row(df_k, 1.0),
                      row(gq_q, sc64), row(gq_k, 1.0), row(dl_q, sc64), row(dl_k, 1.0)])


def kernel(x, ffn1_norm, ffn1_w_gate, ffn1_w_up, ffn1_w_down, mix_norm, w_in, w_out, na_q_norm, na_k_norm, na_rel_bias, diff_q_norm, diff_k_norm, diff_lambda_q1, diff_lambda_k1, diff_lambda_q2, diff_lambda_k2, diff_out_norm, gqa_q_norm, gqa_k_norm, dil_q_norm, dil_k_norm, ffn2_norm, ffn2_w_gate, ffn2_w_up, ffn2_w_down):
    B, S, D = x.shape
    depth = w_in.shape[0]
    rows = S // GRID_W
    tables = _rope_tables(S)
    gmats = _group_matrices()
    ones_row = jnp.ones((1, SEG), F32)
    zero_lamp = jnp.zeros((4, DIFF_QK_DIM), F32)
    zero_li = jnp.zeros((1, 1), F32)

    for l in range(depth):
        x = _ffn(x, ffn1_norm[l], ffn1_w_gate[l].astype(BF16), ffn1_w_up[l].astype(BF16),
                 ffn1_w_down[l].astype(BF16))

        gains = _gain_rows(na_q_norm[l], na_k_norm[l], diff_q_norm[l], diff_k_norm[l],
                           gqa_q_norm[l], gqa_k_norm[l], dil_q_norm[l], dil_k_norm[l])
        qkv, vt_df, vt_gq, *dl_classes = _inproj(x, mix_norm[l], _extend_w_in(w_in[l]), gmats,
                                                 gains, tables)

        o_a = _na(qkv, _na_table(na_rel_bias[l], rows))
        lamp = jnp.stack([diff_lambda_q1[l], diff_lambda_k1[l], diff_lambda_q2[l],
                          diff_lambda_k2[l]]).astype(F32)
        li = jnp.full((1, 1), 0.8 - 0.6 * math.exp(-0.3 * l), F32)
        ogain = jnp.tile(diff_out_norm[l].astype(F32), SEG // HEAD_DIM).reshape(1, SEG)
        o_b = _flash(qkv, vt_df, (SEG_DF_Q, SEG_DF_K), gmats[0], ogain, lamp, li,
                     groups=2 * DIFF_HEADS, diff=True)
        o_c = _flash(qkv, vt_gq, (SEG_GQ_Q, SEG_GQ_K), gmats[0], ones_row, zero_lamp,
                     zero_li, groups=GQA_Q_HEADS, diff=False)
        dl_src = {1: (qkv.reshape(B, 1, S, EXT_WIDTH), SEG_DL_Q)}
        dl_src.update({d: (a, 0) for d, a in zip(DILATIONS, dl_classes)})
        dils = [_dilated(*dl_src[dil], (window // 2) // dil) for window, dil in DIL_PAIRS]
        x = _outproj(x, o_a, o_b, o_c, dils, w_out[l].astype(BF16))

        x = _ffn(x, ffn2_norm[l], ffn2_w_gate[l].astype(BF16), ffn2_w_up[l].astype(BF16),
                 ffn2_w_down[l].astype(BF16))
    return x
```

```python
import functools
import math

import numpy as np
import jax
import jax.numpy as jnp
from jax import lax
from jax.experimental import pallas as pl
from jax.experimental.pallas import tpu as pltpu

F32 = jnp.float32
BF16 = jnp.bfloat16

GRID_W = 64
HEAD_DIM = 64
RMS_EPS = 1e-6
MASK_VALUE = -1e30
LOG2E = math.log2(math.e)

NA_HEADS = 4
NA_WIN_H = 8
NA_WIN_W = 16
DIFF_HEADS = 4
DIFF_QK_DIM = HEAD_DIM // 2
GQA_Q_HEADS = 4
GQA_KV_HEADS = 2
AXIAL_THETA = 10000.0
DIL_HEADS = 4
DIL_PAIRS = ((128, 1), (512, 4), (2048, 16))
DILATIONS = tuple(d for _, d in DIL_PAIRS if d > 1)
ROPE_THETA = 500000.0
ROPE_FRACTION = 4

LANES = 128
SEG = 256
(SEG_NA_Q, SEG_NA_K, SEG_NA_V, SEG_DF_Q, SEG_DF_K, SEG_DF_V,
 SEG_GQ_Q, SEG_GQ_K, SEG_GQ_V, SEG_DL_Q, SEG_DL_K, SEG_DL_V) = range(12)
N_SEG = 12
EXT_WIDTH = N_SEG * SEG

VMEM_LIMIT = 52 * 1024 * 1024

VT_ROWS = HEAD_DIM + 16
COL_BLOCK = 256
QK_AHEAD = 4
KV_UNROLL = 2
FLASH_MAX_SHIFT = 40.0

NA_QROWS = 8
NA_KROWS = 16


def _cparams(sem):
    return pltpu.CompilerParams(dimension_semantics=sem, vmem_limit_bytes=VMEM_LIMIT)


def _resident(shape):
    nd = len(shape)
    return pl.BlockSpec(shape, lambda *_: (0,) * nd, pipeline_mode=pl.Buffered(1))


def _rms_rows(x, gain):
    ms = jnp.mean(x * x, axis=-1, keepdims=True)
    return x * lax.rsqrt(ms + RMS_EPS) * gain


def _group_mean_sq(y, gmat):
    sq = y * y
    hi = sq.astype(BF16)
    lo = (sq - hi.astype(F32)).astype(BF16)
    return (jnp.dot(hi, gmat, preferred_element_type=F32)
            + jnp.dot(lo, gmat, preferred_element_type=F32))


def _dot_nt(a, b):
    return lax.dot_general(a, b, (((1,), (1,)), ((), ())), preferred_element_type=F32)


def _ffn_kernel(x_ref, g_ref, wg_ref, wu_ref, wd_ref, o_ref, *, chunks):
    x = x_ref[...]
    h = _rms_rows(x, g_ref[...]).astype(BF16)
    acc = None
    for c0, c1 in chunks:
        g = jnp.dot(h, wg_ref[:, c0:c1], preferred_element_type=F32)
        u = jnp.dot(h, wu_ref[:, c0:c1], preferred_element_type=F32)
        a = (g * (1.0 / (1.0 + jnp.exp(-g))) * u).astype(BF16)
        d = jnp.dot(a, wd_ref[c0:c1, :], preferred_element_type=F32)
        acc = d if acc is None else acc + d
    o_ref[...] = x + 0.5 * acc


def _ffn(x, gain, wg, wu, wd, *, tm=512, fc=512):
    B, S, D = x.shape
    F = wg.shape[1]
    chunks = tuple((c, min(c + fc, F)) for c in range(0, F, fc))
    xspec = pl.BlockSpec((None, tm, D), lambda b, i: (b, i, 0))
    return pl.pallas_call(
        functools.partial(_ffn_kernel, chunks=chunks),
        out_shape=jax.ShapeDtypeStruct(x.shape, F32),
        grid=(B, S // tm),
        in_specs=[xspec, _resident((1, D)), _resident((D, F)), _resident((D, F)),
                  _resident((F, D))],
        out_specs=xspec,
        compiler_params=_cparams(("parallel", "parallel")),
        name="ffn",
    )(x, gain.reshape(1, D), wg, wu, wd)


_SEG_PLAN = {
    SEG_NA_Q: (0, 0, None), SEG_NA_K: (1, 0, None),
    SEG_DF_Q: (2, 1, "df"), SEG_DF_K: (3, 1, "df"),
    SEG_GQ_Q: (4, 0, "gq"), SEG_GQ_K: (5, 0, "gq"),
    SEG_DL_Q: (6, 0, "dl"), SEG_DL_K: (7, 0, "dl"),
}
_ROPE_HALF = {"df": DIFF_QK_DIM // ROPE_FRACTION // 2, "gq": HEAD_DIM // 4,
              "dl": HEAD_DIM // ROPE_FRACTION // 2}


def _inproj_kernel(x_ref, gn_ref, w_ref, gmat_ref, gains_ref,
                   cdf_ref, sdf_ref, cgq_ref, sgq_ref, cdl_ref, sdl_ref,
                   o_ref, vtdf_ref, vtgq_ref, *dl_refs_and_scratch):
    *dl_refs, ys_ref = dl_refs_and_scratch
    tm = x_ref.shape[0]
    h = _rms_rows(x_ref[...], gn_ref[...]).astype(BF16)
    lane = lax.broadcasted_iota(jnp.int32, (1, SEG), 1)
    tables = {"df": (cdf_ref, sdf_ref), "gq": (cgq_ref, sgq_ref), "dl": (cdl_ref, sdl_ref)}
    vt_plan = {SEG_DF_V: (vtdf_ref, tuple(range(DIFF_HEADS))),
               SEG_GQ_V: (vtgq_ref, tuple(2 * j for j in range(GQA_KV_HEADS)))}
    for seg in range(N_SEG):
        y = jnp.dot(h, w_ref[:, seg * SEG:(seg + 1) * SEG], preferred_element_type=F32)
        if seg in vt_plan:
            vt_ref, lane_heads = vt_plan[seg]
            yt = y.T
            ones = jnp.ones((VT_ROWS - HEAD_DIM, yt.shape[1]), BF16)
            for j, lh in enumerate(lane_heads):
                vt_ref[j * VT_ROWS:j * VT_ROWS + HEAD_DIM, :] = (
                    yt[lh * HEAD_DIM:(lh + 1) * HEAD_DIM].astype(BF16))
                vt_ref[j * VT_ROWS + HEAD_DIM:(j + 1) * VT_ROWS, :] = ones
        plan = _SEG_PLAN.get(seg)
        if plan is not None:
            gi, mi, rope = plan
            ms = _group_mean_sq(y, gmat_ref[mi])
            y = y * lax.rsqrt(ms + RMS_EPS) * gains_ref[gi:gi + 1, :]
            if rope is not None:
                half = _ROPE_HALF[rope]
                c_ref, s_ref = tables[rope]
                first = (lane & (2 * half - 1)) < half
                partner = jnp.where(first, pltpu.roll(y, SEG - half, 1), pltpu.roll(y, half, 1))
                y = y * c_ref[...] + partner * s_ref[...]
        o_ref[:, seg * SEG:(seg + 1) * SEG] = y.astype(BF16)
        if SEG_DL_Q <= seg <= SEG_DL_V:
            for hf in range(SEG // LANES):
                ys_ref[hf] = y[:, hf * LANES:(hf + 1) * LANES]
            col = (seg - SEG_DL_Q) * SEG
            for dl_ref, dil in zip(dl_refs, DILATIONS):
                for r in range(dil):
                    for hf in range(SEG // LANES):
                        dl_ref[r, :, col + hf * LANES:col + (hf + 1) * LANES] = (
                            ys_ref[hf, pl.ds(r, tm // dil, stride=dil), :].astype(BF16))


def _inproj(x, gain, w_ext, gmats, gains, tables, *, tm=512):
    B, S, D = x.shape
    tspec = pl.BlockSpec((tm, SEG), lambda i, b: (i, 0))
    vt_rows = (DIFF_HEADS * VT_ROWS, GQA_KV_HEADS * VT_ROWS)
    return pl.pallas_call(
        _inproj_kernel,
        out_shape=(jax.ShapeDtypeStruct((B, S, EXT_WIDTH), BF16),)
        + tuple(jax.ShapeDtypeStruct((B, S // tm, r, tm), BF16) for r in vt_rows)
        + tuple(jax.ShapeDtypeStruct((B, d, S // d, 3 * SEG), BF16) for d in DILATIONS),
        grid=(S // tm, B),
        in_specs=[pl.BlockSpec((None, tm, D), lambda i, b: (b, i, 0)),
                  _resident((1, D)), _resident((D, EXT_WIDTH)),
                  _resident(gmats.shape), _resident(gains.shape)] + [tspec] * 6,
        out_specs=(pl.BlockSpec((None, tm, EXT_WIDTH), lambda i, b: (b, i, 0)),)
        + tuple(pl.BlockSpec((None, None, r, tm), lambda i, b: (b, i, 0, 0)) for r in vt_rows)
        + tuple(pl.BlockSpec((None, d, tm // d, 3 * SEG), lambda i, b: (b, 0, i, 0))
                for d in DILATIONS),
        scratch_shapes=[pltpu.VMEM((SEG // LANES, tm, LANES), F32)],
        compiler_params=_cparams(("parallel", "parallel")),
        name="inproj",
    )(x, gain.reshape(1, D), w_ext, gmats, gains, *tables)


def _flash_kernel(q_ref, k_ref, vt_ref, gmat_ref, ogain_ref, lamp_ref, li_ref, shift_ref, o_ref,
                  qmt_ref, m_ref, acc_ref, sbuf_ref, *, groups, tq, tk, diff, vheads, bounded):
    gw = SEG // groups
    qt = q_ref[...].astype(F32).T
    row_g = lax.broadcasted_iota(jnp.int32, (SEG, 1), 0) >> int(math.log2(gw))
    for r in range(groups):
        qmt_ref[:, r * tq:(r + 1) * tq] = jnp.where(row_g == r, qt, 0.0).astype(BF16)
    m_ref[...] = jnp.full(m_ref.shape, MASK_VALUE, F32)
    acc_ref[...] = jnp.zeros(acc_ref.shape, F32)
    ncol = groups * tq // COL_BLOCK

    nkv = k_ref.shape[0] // tk
    ahead = sbuf_ref.shape[0]

    def k_block(kv):
        return k_ref[pl.ds(pl.multiple_of(kv * tk, tk), tk), :]

    def qk(k, c):
        return jnp.dot(k, qmt_ref[:, c * COL_BLOCK:(c + 1) * COL_BLOCK],
                       preferred_element_type=F32)

    k0 = k_block(0)
    for c in range(ahead):
        sbuf_ref[c] = qk(k0, c)

    shift = shift_ref[...]

    def body(it, carry):
        pend = [sbuf_ref[c] for c in range(ahead)]
        for u in range(KV_UNROLL):
            kv = it * KV_UNROLL + u
            k = k_block(kv)
            k_next = k_block(jnp.minimum(kv + 1, nkv - 1))
            vt = vt_ref[kv]
            for c in range(ncol):
                hv = (c * COL_BLOCK // tq) * vheads // groups
                cs = slice(c * COL_BLOCK, (c + 1) * COL_BLOCK)
                s = pend.pop(0)
                if c + ahead < ncol:
                    pend.append(qk(k, c + ahead))
                elif u + 1 < KV_UNROLL:
                    pend.append(qk(k_next, c + ahead - ncol))
                else:
                    sbuf_ref[c + ahead - ncol] = qk(k_next, c + ahead - ncol)
                vth = vt[hv * VT_ROWS:(hv + 1) * VT_ROWS]
                if bounded:
                    p = jnp.exp2(s - shift).astype(BF16)
                    acc_ref[:, cs] = acc_ref[:, cs] + jnp.dot(vth, p, preferred_element_type=F32)
                else:
                    m_prev = m_ref[:, cs]
                    m_new = jnp.maximum(m_prev, jnp.max(s, axis=0, keepdims=True))
                    alpha = jnp.exp2(m_prev - m_new)
                    p = jnp.exp2(s - m_new).astype(BF16)
                    acc_ref[:, cs] = (acc_ref[:, cs] * alpha
                                      + jnp.dot(vth, p, preferred_element_type=F32))
                    m_ref[:, cs] = m_new
        return carry

    lax.fori_loop(0, nkv // KV_UNROLL, body, 0)

    acc = acc_ref[...]
    norm = acc[0:HEAD_DIM] * (1.0 / acc[HEAD_DIM:HEAD_DIM + 1])
    if diff:
        lp = lamp_ref[...]
        lam = (jnp.exp(jnp.sum(lp[0:1] * lp[1:2], axis=1, keepdims=True))
               - jnp.exp(jnp.sum(lp[2:3] * lp[3:4], axis=1, keepdims=True)) + li_ref[...])
        pieces = [norm[:, 2 * hd * tq:(2 * hd + 1) * tq]
                  - lam * norm[:, (2 * hd + 1) * tq:(2 * hd + 2) * tq] for hd in range(groups // 2)]
    else:
        pieces = [norm[:, r * tq:(r + 1) * tq] for r in range(groups)]
    out = jnp.concatenate(pieces, axis=0).T
    if diff:
        ms = _group_mean_sq(out, gmat_ref[...])
        out = out * lax.rsqrt(ms + RMS_EPS) * ogain_ref[...] * (1.0 - li_ref[...])
    o_ref[...] = out.astype(BF16)


def _flash(qkv, vt, segs, gmat, ogain, lamp, li, q_gain_row, k_gain_row, *, groups, diff,
           cols=2048):
    B, S, _ = qkv.shape
    _, nkv, vrows, tk = vt.shape
    tq = cols // groups
    qs, ks = segs
    bound = ((SEG // groups) * jnp.max(jnp.abs(q_gain_row)) * jnp.max(jnp.abs(k_gain_row)))
    shift = bound.reshape(1, 1).astype(F32)

    def call(bounded):
        name = ("diff_attn" if diff else "gqa_attn") + ("" if bounded else "_runmax")
        return pl.pallas_call(
            functools.partial(_flash_kernel, groups=groups, tq=tq, tk=tk, diff=diff,
                              vheads=vrows // VT_ROWS, bounded=bounded),
            out_shape=jax.ShapeDtypeStruct((B, S, SEG), BF16),
            grid=(B, S // tq),
            in_specs=[pl.BlockSpec((None, tq, SEG), lambda b, i: (b, i, qs)),
                      pl.BlockSpec((None, S, SEG), lambda b, i: (b, 0, ks)),
                      pl.BlockSpec((None, nkv, vrows, tk), lambda b, i: (b, 0, 0, 0)),
                      _resident(gmat.shape), _resident(ogain.shape), _resident(lamp.shape),
                      _resident(li.shape), _resident(shift.shape)],
            out_specs=pl.BlockSpec((None, tq, SEG), lambda b, i: (b, i, 0)),
            scratch_shapes=[pltpu.VMEM((SEG, cols), BF16), pltpu.VMEM((1, cols), F32),
                            pltpu.VMEM((VT_ROWS, cols), F32),
                            pltpu.VMEM((QK_AHEAD, tk, COL_BLOCK), F32)],
            compiler_params=_cparams(("parallel", "parallel")),
            name=name,
        )(qkv, qkv, vt, gmat, ogain, lamp, li, shift)

    return lax.cond(bound <= FLASH_MAX_SHIFT, lambda: call(True), lambda: call(False))


def _na_kernel(q_ref, k_ref, v_ref, tab_ref, o_ref, *, nq, nk):
    i = pl.program_id(2)
    seq = k_ref.shape[0]
    start = jnp.clip(i * nq - (nk - nq) // 2, 0, seq - nk)
    start = pl.multiple_of(start, (nk - nq) // 2)
    kw = k_ref[pl.ds(start, nk), :]
    vw = v_ref[pl.ds(start, nk), :]
    q = q_ref[...]
    lane_h = lax.broadcasted_iota(jnp.int32, (1, 2 * HEAD_DIM), 1) >> int(math.log2(HEAD_DIM))
    outs = []
    for j in range(2):
        qm = jnp.where(lane_h == j, q, jnp.zeros_like(q))
        s = _dot_nt(qm, kw) + tab_ref[j]
        m = jnp.max(s, axis=1, keepdims=True)
        p = jnp.exp2(s - m)
        l = jnp.sum(p, axis=1, keepdims=True)
        outs.append(jnp.dot(p.astype(BF16), vw, preferred_element_type=F32) * (1.0 / l))
    o_ref[...] = jnp.where(lane_h == 0, outs[0], outs[1]).astype(BF16)


def _na_table(rel_bias, rows):
    kh = min(NA_WIN_H, rows)
    n_ro, n_co = 2 * NA_WIN_H - 1, 2 * NA_WIN_W - 1
    c = np.arange(GRID_W)[:, None]
    kc = np.arange(GRID_W)[None, :]
    cs = np.clip(c - NA_WIN_W // 2, 0, GRID_W - NA_WIN_W)
    col_valid = (kc >= cs) & (kc < cs + NA_WIN_W)
    col_sel = (kc - c + (NA_WIN_W - 1))[..., None] == np.arange(n_co)
    row_sel, row_valid = [], []
    for r0 in (0, NA_QROWS, rows - NA_QROWS):
        kr0 = int(np.clip(r0 - (NA_KROWS - NA_QROWS) // 2, 0, rows - NA_KROWS))
        r = r0 + np.arange(NA_QROWS)[:, None]
        kr = kr0 + np.arange(NA_KROWS)[None, :]
        rs = np.clip(r - kh // 2, 0, rows - kh)
        row_valid.append((kr >= rs) & (kr < rs + kh))
        row_sel.append((kr - r + (NA_WIN_H - 1))[..., None] == np.arange(n_ro))
    row_sel = jnp.asarray(np.stack(row_sel), F32)
    valid = (np.stack(row_valid)[:, None, :, None, :, None]
             & col_valid[None, None, None, :, None, :])
    hi = lax.Precision.HIGHEST
    cols = jnp.einsum("hab,cdb->hacd", rel_bias.astype(F32), jnp.asarray(col_sel, F32),
                      precision=hi)
    full = jnp.einsum("vqka,hacd->vhqckd", row_sel, cols, precision=hi)
    table = jnp.where(valid, full * LOG2E, MASK_VALUE)
    return table.reshape(3, rel_bias.shape[0], NA_QROWS * GRID_W, NA_KROWS * GRID_W)


def _na(qkv, table):
    B, S, _ = qkv.shape
    nq, nk = NA_QROWS * GRID_W, NA_KROWS * GRID_W
    nblk = S // nq
    hp = 2 * HEAD_DIM

    def variant(i):
        return jnp.where(i == 0, 0, jnp.where(i == nblk - 1, 2, 1))

    return pl.pallas_call(
        functools.partial(_na_kernel, nq=nq, nk=nk),
        out_shape=jax.ShapeDtypeStruct((B, S, SEG), BF16),
        grid=(2, B, nblk),
        in_specs=[pl.BlockSpec((None, nq, hp), lambda p, b, i: (b, i, 2 * SEG_NA_Q + p)),
                  pl.BlockSpec((None, S, hp), lambda p, b, i: (b, 0, 2 * SEG_NA_K + p)),
                  pl.BlockSpec((None, S, hp), lambda p, b, i: (b, 0, 2 * SEG_NA_V + p)),
                  pl.BlockSpec((None, 2, nq, nk), lambda p, b, i: (variant(i), p, 0, 0))],
        out_specs=pl.BlockSpec((None, nq, hp), lambda p, b, i: (b, i, p)),
        compiler_params=_cparams(("parallel", "parallel", "arbitrary")),
        name="na_attn",
    )(qkv, qkv, qkv, table)


def _dil_kernel(q_ref, k_ref, v_ref, o_ref, lse_ref, *, tq, halo):
    j = pl.program_id(2)
    length = k_ref.shape[0]
    win = tq + 2 * halo
    start = pl.multiple_of(jnp.clip(j * tq - halo, 0, length - win), halo)
    kw = k_ref[pl.ds(start, win), :]
    vw = v_ref[pl.ds(start, win), :]
    q = q_ref[...]
    qpos = j * tq + lax.broadcasted_iota(jnp.int32, (tq, win), 0)
    kpos = start + lax.broadcasted_iota(jnp.int32, (tq, win), 1)
    bias = jnp.where(jnp.abs(kpos - qpos) <= halo, 0.0, MASK_VALUE)
    lane_h = lax.broadcasted_iota(jnp.int32, (1, SEG), 1) >> int(math.log2(HEAD_DIM))
    out = jnp.zeros((tq, SEG), F32)
    lse = jnp.zeros((tq, SEG), F32)
    for hd in range(DIL_HEADS):
        qm = jnp.where(lane_h == hd, q, jnp.zeros_like(q))
        s = _dot_nt(qm, kw) + bias
        m = jnp.max(s, axis=1, keepdims=True)
        p = jnp.exp2(s - m)
        l = jnp.sum(p, axis=1, keepdims=True)
        o = jnp.dot(p.astype(BF16), vw, preferred_element_type=F32) * (1.0 / l)
        out = jnp.where(lane_h == hd, o, out)
        lse = jnp.where(lane_h == hd, m + jnp.log(l) * LOG2E, lse)
    o_ref[...] = out
    lse_ref[...] = lse


def _dilated(src, first_seg, halo, *, tq=256):
    if src.ndim == 3:
        (B, length, _), dil = src.shape, 1
        in_specs = [pl.BlockSpec((None, tq, SEG), lambda b, r, j: (b, j, first_seg)),
                    pl.BlockSpec((None, length, SEG), lambda b, r, j: (b, 0, first_seg + 1)),
                    pl.BlockSpec((None, length, SEG), lambda b, r, j: (b, 0, first_seg + 2))]
    else:
        B, dil, length, _ = src.shape
        in_specs = [pl.BlockSpec((None, None, tq, SEG), lambda b, r, j: (b, r, j, first_seg)),
                    pl.BlockSpec((None, None, length, SEG),
                                 lambda b, r, j: (b, r, 0, first_seg + 1)),
                    pl.BlockSpec((None, None, length, SEG),
                                 lambda b, r, j: (b, r, 0, first_seg + 2))]
    oshape = jax.ShapeDtypeStruct((B, dil, length, SEG), F32)
    ospec = pl.BlockSpec((None, None, tq, SEG), lambda b, r, j: (b, r, j, 0))
    return pl.pallas_call(
        functools.partial(_dil_kernel, tq=tq, halo=halo),
        out_shape=(oshape, oshape),
        grid=(B, dil, length // tq),
        in_specs=in_specs,
        out_specs=(ospec, ospec),
        compiler_params=_cparams(("parallel", "parallel", "arbitrary")),
        name=f"dil_attn_{dil}",
    )(src, src, src)


def _outproj_kernel(x_ref, oa_ref, ob_ref, oc_ref, d0_ref, d1_ref, d2_ref,
                    l0_ref, l1_ref, l2_ref, w_ref, o_ref, cat_ref, *row_scratch):
    row_scratch = list(row_scratch)

    def token_rows(ref):
        dil = ref.shape[0]
        if dil == 1:
            return ref[0]
        buf = row_scratch.pop()
        for r in range(dil):
            slab = ref[r]
            for hf in range(SEG // LANES):
                buf[hf, pl.ds(r, ref.shape[1], stride=dil), :] = slab[:, hf * LANES:(hf + 1) * LANES]
        return jnp.concatenate([buf[hf] for hf in range(SEG // LANES)], axis=1)

    l0, l1, l2 = token_rows(l0_ref), token_rows(l1_ref), token_rows(l2_ref)
    m = jnp.maximum(jnp.maximum(l0, l1), l2)
    e0, e1, e2 = jnp.exp2(l0 - m), jnp.exp2(l1 - m), jnp.exp2(l2 - m)
    od = ((e0 * token_rows(d0_ref) + e1 * token_rows(d1_ref) + e2 * token_rows(d2_ref))
          * (1.0 / (e0 + e1 + e2)))
    cat_ref[:, 0 * SEG:1 * SEG] = oa_ref[...]
    cat_ref[:, 1 * SEG:2 * SEG] = ob_ref[...]
    cat_ref[:, 2 * SEG:3 * SEG] = oc_ref[...]
    cat_ref[:, 3 * SEG:4 * SEG] = od.astype(BF16)
    o_ref[...] = x_ref[...] + jnp.dot(cat_ref[...], w_ref[...], preferred_element_type=F32)


def _outproj(x, oa, ob, oc, dils, w_out, *, tm=512):
    B, S, D = x.shape
    xspec = pl.BlockSpec((None, tm, D), lambda b, i: (b, i, 0))
    sspec = pl.BlockSpec((None, tm, SEG), lambda b, i: (b, i, 0))
    outs = [d[0] for d in dils]
    lses = [d[1] for d in dils]
    dspecs = [pl.BlockSpec((None, a.shape[1], tm // a.shape[1], SEG), lambda b, i: (b, 0, i, 0))
              for a in outs + lses]
    n_scratch = sum(a.shape[1] > 1 for a in outs + lses)
    return pl.pallas_call(
        _outproj_kernel,
        out_shape=jax.ShapeDtypeStruct(x.shape, F32),
        grid=(B, S // tm),
        in_specs=[xspec] + [sspec] * 3 + dspecs + [_resident(w_out.shape)],
        out_specs=xspec,
        scratch_shapes=[pltpu.VMEM((tm, 4 * SEG), BF16)]
        + [pltpu.VMEM((SEG // LANES, tm, LANES), F32)] * n_scratch,
        compiler_params=_cparams(("parallel", "parallel")),
        name="outproj",
    )(x, oa, ob, oc, *outs, *lses, w_out)


def _rope_cs(pos, dim, theta):
    inv_freq = 1.0 / (theta ** (jnp.arange(0, dim, 2, dtype=F32) / dim))
    ang = pos.astype(F32)[:, None] * inv_freq[None, :]
    return jnp.cos(ang), jnp.sin(ang)


def _rope_tables(S):
    pos = jnp.arange(S, dtype=jnp.int32)

    def expand(parts_c, parts_s, reps):
        return (jnp.tile(jnp.concatenate(parts_c, axis=1), (1, reps)),
                jnp.tile(jnp.concatenate(parts_s, axis=1), (1, reps)))

    c, s = _rope_cs(pos, DIFF_QK_DIM // ROPE_FRACTION, ROPE_THETA)
    rest = DIFF_QK_DIM - 2 * c.shape[1]
    df = expand([c, c, jnp.ones((S, rest), F32)], [-s, s, jnp.zeros((S, rest), F32)],
                SEG // DIFF_QK_DIM)
    c, s = _rope_cs(pos, HEAD_DIM // ROPE_FRACTION, ROPE_THETA)
    rest = HEAD_DIM - 2 * c.shape[1]
    dl = expand([c, c, jnp.ones((S, rest), F32)], [-s, s, jnp.zeros((S, rest), F32)],
                SEG // HEAD_DIM)
    cr, sr = _rope_cs(pos // GRID_W, HEAD_DIM // 2, AXIAL_THETA)
    cc, sc = _rope_cs(pos % GRID_W, HEAD_DIM // 2, AXIAL_THETA)
    gq = expand([cr, cr, cc, cc], [-sr, sr, -sc, sc], SEG // HEAD_DIM)
    return df + gq + dl


def _group_matrices():
    lane = np.arange(SEG)
    mats = [(lane[:, None] // g == lane[None, :] // g).astype(np.float32) / g
            for g in (HEAD_DIM, DIFF_QK_DIM)]
    return jnp.asarray(np.stack(mats), BF16)


def _extend_w_in(w_in):
    kv0 = 7 * SEG
    hd = HEAD_DIM
    cols = np.concatenate([
        np.arange(0, kv0),
        kv0 + np.array([0, 0, 1, 1]).repeat(hd) * hd + np.tile(np.arange(hd), 4),
        kv0 + 2 * hd + np.array([0, 0, 1, 1]).repeat(hd) * hd + np.tile(np.arange(hd), 4),
        np.arange(kv0 + 4 * hd, w_in.shape[1]),
    ])
    return w_in[:, cols].astype(BF16)


def _gain_rows(na_q, na_k, df_q, df_k, gq_q, gq_k, dl_q, dl_k):
    def row(g, scale):
        return jnp.tile(g.astype(F32), SEG // g.shape[0]) * scale
    sc64 = HEAD_DIM ** -0.5 * LOG2E
    sc32 = DIFF_QK_DIM ** -0.5 * LOG2E
    return jnp.stack([row(na_q, sc64), row(na_k, 1.0), row(df_q, sc32), row(df_k, 1.0),
                      row(gq_q, sc64), row(gq_k, 1.0), row(dl_q, sc64), row(dl_k, 1.0)])


def kernel(x, ffn1_norm, ffn1_w_gate, ffn1_w_up, ffn1_w_down, mix_norm, w_in, w_out, na_q_norm, na_k_norm, na_rel_bias, diff_q_norm, diff_k_norm, diff_lambda_q1, diff_lambda_k1, diff_lambda_q2, diff_lambda_k2, diff_out_norm, gqa_q_norm, gqa_k_norm, dil_q_norm, dil_k_norm, ffn2_norm, ffn2_w_gate, ffn2_w_up, ffn2_w_down):
    B, S, D = x.shape
    depth = w_in.shape[0]
    rows = S // GRID_W
    tables = _rope_tables(S)
    gmats = _group_matrices()
    ones_row = jnp.ones((1, SEG), F32)
    zero_lamp = jnp.zeros((4, DIFF_QK_DIM), F32)
    zero_li = jnp.zeros((1, 1), F32)

    for l in range(depth):
        x = _ffn(x, ffn1_norm[l], ffn1_w_gate[l].astype(BF16), ffn1_w_up[l].astype(BF16),
                 ffn1_w_down[l].astype(BF16))

        gains = _gain_rows(na_q_norm[l], na_k_norm[l], diff_q_norm[l], diff_k_norm[l],
                           gqa_q_norm[l], gqa_k_norm[l], dil_q_norm[l], dil_k_norm[l])
        qkv, vt_df, vt_gq, *dl_classes = _inproj(x, mix_norm[l], _extend_w_in(w_in[l]), gmats,
                                                 gains, tables)

        o_a = _na(qkv, _na_table(na_rel_bias[l], rows))
        lamp = jnp.stack([diff_lambda_q1[l], diff_lambda_k1[l], diff_lambda_q2[l],
                          diff_lambda_k2[l]]).astype(F32)
        li = jnp.full((1, 1), 0.8 - 0.6 * math.exp(-0.3 * l), F32)
        ogain = jnp.tile(diff_out_norm[l].astype(F32), SEG // HEAD_DIM).reshape(1, SEG)
        o_b = _flash(qkv, vt_df, (SEG_DF_Q, SEG_DF_K), gmats[0], ogain, lamp, li,
                     gains[2], gains[3], groups=2 * DIFF_HEADS, diff=True)
        o_c = _flash(qkv, vt_gq, (SEG_GQ_Q, SEG_GQ_K), gmats[0], ones_row, zero_lamp,
                     zero_li, gains[4], gains[5], groups=GQA_Q_HEADS, diff=False)
        dl_src = {1: (qkv, SEG_DL_Q)}
        dl_src.update({d: (a, 0) for d, a in zip(DILATIONS, dl_classes)})
        dils = [_dilated(*dl_src[dil], (window // 2) // dil) for window, dil in DIL_PAIRS]
        x = _outproj(x, o_a, o_b, o_c, dils, w_out[l].astype(BF16))

        x = _ffn(x, ffn2_norm[l], ffn2_w_gate[l].astype(BF16), ffn2_w_up[l].astype(BF16),
                 ffn2_w_down[l].astype(BF16))
    return x
```

```python
import functools
import math

import numpy as np
import jax
import jax.numpy as jnp
from jax import lax
from jax.experimental import pallas as pl
from jax.experimental.pallas import tpu as pltpu

F32 = jnp.float32
BF16 = jnp.bfloat16

GRID_W = 64
HEAD_DIM = 64
RMS_EPS = 1e-6
MASK_VALUE = -1e30
LOG2E = math.log2(math.e)

NA_HEADS = 4
NA_WIN_H = 8
NA_WIN_W = 16
DIFF_HEADS = 4
DIFF_QK_DIM = HEAD_DIM // 2
GQA_Q_HEADS = 4
GQA_KV_HEADS = 2
AXIAL_THETA = 10000.0
DIL_HEADS = 4
DIL_PAIRS = ((128, 1), (512, 4), (2048, 16))
DILATIONS = tuple(d for _, d in DIL_PAIRS if d > 1)
ROPE_THETA = 500000.0
ROPE_FRACTION = 4

LANES = 128
SEG = 256
(SEG_NA_Q, SEG_NA_K, SEG_NA_V, SEG_DF_Q, SEG_DF_K, SEG_DF_V,
 SEG_GQ_Q, SEG_GQ_K, SEG_GQ_V, SEG_DL_Q, SEG_DL_K, SEG_DL_V) = range(12)
N_SEG = 12
EXT_WIDTH = N_SEG * SEG

VMEM_LIMIT = 52 * 1024 * 1024

VT_ROWS = HEAD_DIM + 16
COL_BLOCK = 256
QK_AHEAD = 4
KV_UNROLL = 4
FLASH_COLS = 4096
FLASH_MAX_SHIFT = 40.0

NA_QROWS = 8
NA_KROWS = 16


def _cparams(sem):
    return pltpu.CompilerParams(dimension_semantics=sem, vmem_limit_bytes=VMEM_LIMIT)


def _resident(shape):
    nd = len(shape)
    return pl.BlockSpec(shape, lambda *_: (0,) * nd, pipeline_mode=pl.Buffered(1))


def _rms_rows(x, gain):
    ms = jnp.mean(x * x, axis=-1, keepdims=True)
    return x * lax.rsqrt(ms + RMS_EPS) * gain


def _group_mean_sq(y, gmat):
    sq = y * y
    hi = sq.astype(BF16)
    lo = (sq - hi.astype(F32)).astype(BF16)
    return (jnp.dot(hi, gmat, preferred_element_type=F32)
            + jnp.dot(lo, gmat, preferred_element_type=F32))


def _dot_nt(a, b):
    return lax.dot_general(a, b, (((1,), (1,)), ((), ())), preferred_element_type=F32)


def _ffn_kernel(x_ref, g_ref, wg_ref, wu_ref, wd_ref, o_ref, *, chunks):
    x = x_ref[...]
    h = _rms_rows(x, g_ref[...]).astype(BF16)
    acc = None
    for c0, c1 in chunks:
        g = jnp.dot(h, wg_ref[:, c0:c1], preferred_element_type=F32)
        u = jnp.dot(h, wu_ref[:, c0:c1], preferred_element_type=F32)
        a = (g * (1.0 / (1.0 + jnp.exp(-g))) * u).astype(BF16)
        d = jnp.dot(a, wd_ref[c0:c1, :], preferred_element_type=F32)
        acc = d if acc is None else acc + d
    o_ref[...] = x + 0.5 * acc


def _ffn(x, gain, wg, wu, wd, *, tm=512, fc=512):
    B, S, D = x.shape
    F = wg.shape[1]
    chunks = tuple((c, min(c + fc, F)) for c in range(0, F, fc))
    xspec = pl.BlockSpec((None, tm, D), lambda b, i: (b, i, 0))
    return pl.pallas_call(
        functools.partial(_ffn_kernel, chunks=chunks),
        out_shape=jax.ShapeDtypeStruct(x.shape, F32),
        grid=(B, S // tm),
        in_specs=[xspec, _resident((1, D)), _resident((D, F)), _resident((D, F)),
                  _resident((F, D))],
        out_specs=xspec,
        compiler_params=_cparams(("parallel", "parallel")),
        name="ffn",
    )(x, gain.reshape(1, D), wg, wu, wd)


_SEG_PLAN = {
    SEG_NA_Q: (0, 0, None), SEG_NA_K: (1, 0, None),
    SEG_DF_Q: (2, 1, "df"), SEG_DF_K: (3, 1, "df"),
    SEG_GQ_Q: (4, 0, "gq"), SEG_GQ_K: (5, 0, "gq"),
    SEG_DL_Q: (6, 0, "dl"), SEG_DL_K: (7, 0, "dl"),
}
_ROPE_HALF = {"df": DIFF_QK_DIM // ROPE_FRACTION // 2, "gq": HEAD_DIM // 4,
              "dl": HEAD_DIM // ROPE_FRACTION // 2}


def _inproj_kernel(x_ref, gn_ref, w_ref, gmat_ref, gains_ref,
                   cdf_ref, sdf_ref, cgq_ref, sgq_ref, cdl_ref, sdl_ref,
                   o_ref, vtdf_ref, vtgq_ref, *dl_refs_and_scratch):
    *dl_refs, ys_ref = dl_refs_and_scratch
    tm = x_ref.shape[0]
    h = _rms_rows(x_ref[...], gn_ref[...]).astype(BF16)
    lane = lax.broadcasted_iota(jnp.int32, (1, SEG), 1)
    tables = {"df": (cdf_ref, sdf_ref), "gq": (cgq_ref, sgq_ref), "dl": (cdl_ref, sdl_ref)}
    vt_plan = {SEG_DF_V: (vtdf_ref, tuple(range(DIFF_HEADS))),
               SEG_GQ_V: (vtgq_ref, tuple(2 * j for j in range(GQA_KV_HEADS)))}
    for seg in range(N_SEG):
        y = jnp.dot(h, w_ref[:, seg * SEG:(seg + 1) * SEG], preferred_element_type=F32)
        if seg in vt_plan:
            vt_ref, lane_heads = vt_plan[seg]
            yt = y.T
            ones = jnp.ones((VT_ROWS - HEAD_DIM, yt.shape[1]), BF16)
            for j, lh in enumerate(lane_heads):
                vt_ref[j * VT_ROWS:j * VT_ROWS + HEAD_DIM, :] = (
                    yt[lh * HEAD_DIM:(lh + 1) * HEAD_DIM].astype(BF16))
                vt_ref[j * VT_ROWS + HEAD_DIM:(j + 1) * VT_ROWS, :] = ones
        plan = _SEG_PLAN.get(seg)
        if plan is not None:
            gi, mi, rope = plan
            ms = _group_mean_sq(y, gmat_ref[mi])
            y = y * lax.rsqrt(ms + RMS_EPS) * gains_ref[gi:gi + 1, :]
            if rope is not None:
                half = _ROPE_HALF[rope]
                c_ref, s_ref = tables[rope]
                first = (lane & (2 * half - 1)) < half
                partner = jnp.where(first, pltpu.roll(y, SEG - half, 1), pltpu.roll(y, half, 1))
                y = y * c_ref[...] + partner * s_ref[...]
        o_ref[:, seg * SEG:(seg + 1) * SEG] = y.astype(BF16)
        if SEG_DL_Q <= seg <= SEG_DL_V:
            for hf in range(SEG // LANES):
                ys_ref[hf] = y[:, hf * LANES:(hf + 1) * LANES]
            col = (seg - SEG_DL_Q) * SEG
            for dl_ref, dil in zip(dl_refs, DILATIONS):
                for r in range(dil):
                    for hf in range(SEG // LANES):
                        dl_ref[r, :, col + hf * LANES:col + (hf + 1) * LANES] = (
                            ys_ref[hf, pl.ds(r, tm // dil, stride=dil), :].astype(BF16))


def _inproj(x, gain, w_ext, gmats, gains, tables, *, tm=512):
    B, S, D = x.shape
    tspec = pl.BlockSpec((tm, SEG), lambda i, b: (i, 0))
    vt_rows = (DIFF_HEADS * VT_ROWS, GQA_KV_HEADS * VT_ROWS)
    return pl.pallas_call(
        _inproj_kernel,
        out_shape=(jax.ShapeDtypeStruct((B, S, EXT_WIDTH), BF16),)
        + tuple(jax.ShapeDtypeStruct((B, S // tm, r, tm), BF16) for r in vt_rows)
        + tuple(jax.ShapeDtypeStruct((B, d, S // d, 3 * SEG), BF16) for d in DILATIONS),
        grid=(S // tm, B),
        in_specs=[pl.BlockSpec((None, tm, D), lambda i, b: (b, i, 0)),
                  _resident((1, D)), _resident((D, EXT_WIDTH)),
                  _resident(gmats.shape), _resident(gains.shape)] + [tspec] * 6,
        out_specs=(pl.BlockSpec((None, tm, EXT_WIDTH), lambda i, b: (b, i, 0)),)
        + tuple(pl.BlockSpec((None, None, r, tm), lambda i, b: (b, i, 0, 0)) for r in vt_rows)
        + tuple(pl.BlockSpec((None, d, tm // d, 3 * SEG), lambda i, b: (b, 0, i, 0))
                for d in DILATIONS),
        scratch_shapes=[pltpu.VMEM((SEG // LANES, tm, LANES), F32)],
        compiler_params=_cparams(("parallel", "parallel")),
        name="inproj",
    )(x, gain.reshape(1, D), w_ext, gmats, gains, *tables)


def _flash_kernel(q_ref, k_ref, vt_ref, gmat_ref, ogain_ref, lamp_ref, li_ref, shift_ref, o_ref,
                  qmt_ref, m_ref, acc_ref, sbuf_ref, *, groups, tq, tk, diff, vheads, bounded):
    gw = SEG // groups
    qt = q_ref[...].astype(F32).T
    row_g = lax.broadcasted_iota(jnp.int32, (SEG, 1), 0) >> int(math.log2(gw))
    for r in range(groups):
        qmt_ref[:, r * tq:(r + 1) * tq] = jnp.where(row_g == r, qt, 0.0).astype(BF16)
    m_ref[...] = jnp.full(m_ref.shape, MASK_VALUE, F32)
    acc_ref[...] = jnp.zeros(acc_ref.shape, F32)
    ncol = groups * tq // COL_BLOCK

    nkv = k_ref.shape[0] // tk
    ahead = sbuf_ref.shape[0]

    def k_block(kv):
        return k_ref[pl.ds(pl.multiple_of(kv * tk, tk), tk), :]

    def qk(k, c):
        return jnp.dot(k, qmt_ref[:, c * COL_BLOCK:(c + 1) * COL_BLOCK],
                       preferred_element_type=F32)

    k0 = k_block(0)
    for c in range(ahead):
        sbuf_ref[c] = qk(k0, c)

    shift = shift_ref[...]

    def body(it, carry):
        pend = [sbuf_ref[c] for c in range(ahead)]
        for u in range(KV_UNROLL):
            kv = it * KV_UNROLL + u
            k = k_block(kv)
            k_next = k_block(jnp.minimum(kv + 1, nkv - 1))
            vt = vt_ref[kv]
            for c in range(ncol):
                hv = (c * COL_BLOCK // tq) * vheads // groups
                cs = slice(c * COL_BLOCK, (c + 1) * COL_BLOCK)
                s = pend.pop(0)
                if c + ahead < ncol:
                    pend.append(qk(k, c + ahead))
                elif u + 1 < KV_UNROLL:
                    pend.append(qk(k_next, c + ahead - ncol))
                else:
                    sbuf_ref[c + ahead - ncol] = qk(k_next, c + ahead - ncol)
                vth = vt[hv * VT_ROWS:(hv + 1) * VT_ROWS]
                if bounded:
                    p = jnp.exp2(s - shift).astype(BF16)
                    acc_ref[:, cs] = acc_ref[:, cs] + jnp.dot(vth, p, preferred_element_type=F32)
                else:
                    m_prev = m_ref[:, cs]
                    m_new = jnp.maximum(m_prev, jnp.max(s, axis=0, keepdims=True))
                    alpha = jnp.exp2(m_prev - m_new)
                    p = jnp.exp2(s - m_new).astype(BF16)
                    acc_ref[:, cs] = (acc_ref[:, cs] * alpha
                                      + jnp.dot(vth, p, preferred_element_type=F32))
                    m_ref[:, cs] = m_new
        return carry

    lax.fori_loop(0, nkv // KV_UNROLL, body, 0)

    acc = acc_ref[...]
    norm = acc[0:HEAD_DIM] * (1.0 / acc[HEAD_DIM:HEAD_DIM + 1])
    if diff:
        lp = lamp_ref[...]
        lam = (jnp.exp(jnp.sum(lp[0:1] * lp[1:2], axis=1, keepdims=True))
               - jnp.exp(jnp.sum(lp[2:3] * lp[3:4], axis=1, keepdims=True)) + li_ref[...])
        pieces = [norm[:, 2 * hd * tq:(2 * hd + 1) * tq]
                  - lam * norm[:, (2 * hd + 1) * tq:(2 * hd + 2) * tq] for hd in range(groups // 2)]
    else:
        pieces = [norm[:, r * tq:(r + 1) * tq] for r in range(groups)]
    out = jnp.concatenate(pieces, axis=0).T
    if diff:
        ms = _group_mean_sq(out, gmat_ref[...])
        out = out * lax.rsqrt(ms + RMS_EPS) * ogain_ref[...] * (1.0 - li_ref[...])
    o_ref[...] = out.astype(BF16)


def _flash(qkv, vt, segs, gmat, ogain, lamp, li, q_gain_row, k_gain_row, *, groups, diff,
           cols=FLASH_COLS):
    B, S, _ = qkv.shape
    _, nkv, vrows, tk = vt.shape
    assert nkv % KV_UNROLL == 0 and S % (cols // groups) == 0
    tq = cols // groups
    qs, ks = segs
    bound = ((SEG // groups) * jnp.max(jnp.abs(q_gain_row)) * jnp.max(jnp.abs(k_gain_row)))
    shift = bound.reshape(1, 1).astype(F32)

    def call(bounded):
        name = ("diff_attn" if diff else "gqa_attn") + ("" if bounded else "_runmax")
        return pl.pallas_call(
            functools.partial(_flash_kernel, groups=groups, tq=tq, tk=tk, diff=diff,
                              vheads=vrows // VT_ROWS, bounded=bounded),
            out_shape=jax.ShapeDtypeStruct((B, S, SEG), BF16),
            grid=(B, S // tq),
            in_specs=[pl.BlockSpec((None, tq, SEG), lambda b, i: (b, i, qs)),
                      pl.BlockSpec((None, S, SEG), lambda b, i: (b, 0, ks)),
                      pl.BlockSpec((None, nkv, vrows, tk), lambda b, i: (b, 0, 0, 0)),
                      _resident(gmat.shape), _resident(ogain.shape), _resident(lamp.shape),
                      _resident(li.shape), _resident(shift.shape)],
            out_specs=pl.BlockSpec((None, tq, SEG), lambda b, i: (b, i, 0)),
            scratch_shapes=[pltpu.VMEM((SEG, cols), BF16), pltpu.VMEM((1, cols), F32),
                            pltpu.VMEM((VT_ROWS, cols), F32),
                            pltpu.VMEM((QK_AHEAD, tk, COL_BLOCK), F32)],
            compiler_params=_cparams(("parallel", "parallel")),
            name=name,
        )(qkv, qkv, vt, gmat, ogain, lamp, li, shift)

    return lax.cond(bound <= FLASH_MAX_SHIFT, lambda: call(True), lambda: call(False))


def _na_kernel(q_ref, k_ref, v_ref, tab_ref, o_ref, *, nq, nk):
    i = pl.program_id(2)
    seq = k_ref.shape[0]
    start = jnp.clip(i * nq - (nk - nq) // 2, 0, seq - nk)
    start = pl.multiple_of(start, (nk - nq) // 2)
    kw = k_ref[pl.ds(start, nk), :]
    vw = v_ref[pl.ds(start, nk), :]
    q = q_ref[...]
    lane_h = lax.broadcasted_iota(jnp.int32, (1, 2 * HEAD_DIM), 1) >> int(math.log2(HEAD_DIM))
    outs = []
    for j in range(2):
        qm = jnp.where(lane_h == j, q, jnp.zeros_like(q))
        s = _dot_nt(qm, kw) + tab_ref[j]
        m = jnp.max(s, axis=1, keepdims=True)
        p = jnp.exp2(s - m)
        l = jnp.sum(p, axis=1, keepdims=True)
        outs.append(jnp.dot(p.astype(BF16), vw, preferred_element_type=F32) * (1.0 / l))
    o_ref[...] = jnp.where(lane_h == 0, outs[0], outs[1]).astype(BF16)


def _na_table(rel_bias, rows):
    kh = min(NA_WIN_H, rows)
    n_ro, n_co = 2 * NA_WIN_H - 1, 2 * NA_WIN_W - 1
    c = np.arange(GRID_W)[:, None]
    kc = np.arange(GRID_W)[None, :]
    cs = np.clip(c - NA_WIN_W // 2, 0, GRID_W - NA_WIN_W)
    col_valid = (kc >= cs) & (kc < cs + NA_WIN_W)
    col_sel = (kc - c + (NA_WIN_W - 1))[..., None] == np.arange(n_co)
    row_sel, row_valid = [], []
    for r0 in (0, NA_QROWS, rows - NA_QROWS):
        kr0 = int(np.clip(r0 - (NA_KROWS - NA_QROWS) // 2, 0, rows - NA_KROWS))
        r = r0 + np.arange(NA_QROWS)[:, None]
        kr = kr0 + np.arange(NA_KROWS)[None, :]
        rs = np.clip(r - kh // 2, 0, rows - kh)
        row_valid.append((kr >= rs) & (kr < rs + kh))
        row_sel.append((kr - r + (NA_WIN_H - 1))[..., None] == np.arange(n_ro))
    row_sel = jnp.asarray(np.stack(row_sel), F32)
    valid = (np.stack(row_valid)[:, None, :, None, :, None]
             & col_valid[None, None, None, :, None, :])
    hi = lax.Precision.HIGHEST
    cols = jnp.einsum("hab,cdb->hacd", rel_bias.astype(F32), jnp.asarray(col_sel, F32),
                      precision=hi)
    full = jnp.einsum("vqka,hacd->vhqckd", row_sel, cols, precision=hi)
    table = jnp.where(valid, full * LOG2E, MASK_VALUE)
    return table.reshape(3, rel_bias.shape[0], NA_QROWS * GRID_W, NA_KROWS * GRID_W)


def _na(qkv, table):
    B, S, _ = qkv.shape
    nq, nk = NA_QROWS * GRID_W, NA_KROWS * GRID_W
    nblk = S // nq
    hp = 2 * HEAD_DIM

    def variant(i):
        return jnp.where(i == 0, 0, jnp.where(i == nblk - 1, 2, 1))

    return pl.pallas_call(
        functools.partial(_na_kernel, nq=nq, nk=nk),
        out_shape=jax.ShapeDtypeStruct((B, S, SEG), BF16),
        grid=(2, B, nblk),
        in_specs=[pl.BlockSpec((None, nq, hp), lambda p, b, i: (b, i, 2 * SEG_NA_Q + p)),
                  pl.BlockSpec((None, S, hp), lambda p, b, i: (b, 0, 2 * SEG_NA_K + p)),
                  pl.BlockSpec((None, S, hp), lambda p, b, i: (b, 0, 2 * SEG_NA_V + p)),
                  pl.BlockSpec((None, 2, nq, nk), lambda p, b, i: (variant(i), p, 0, 0))],
        out_specs=pl.BlockSpec((None, nq, hp), lambda p, b, i: (b, i, p)),
        compiler_params=_cparams(("parallel", "parallel", "arbitrary")),
        name="na_attn",
    )(qkv, qkv, qkv, table)


def _dil_kernel(q_ref, k_ref, v_ref, o_ref, lse_ref, *, tq, halo):
    j = pl.program_id(2)
    length = k_ref.shape[0]
    win = tq + 2 * halo
    start = pl.multiple_of(jnp.clip(j * tq - halo, 0, length - win), halo)
    kw = k_ref[pl.ds(start, win), :]
    vw = v_ref[pl.ds(start, win), :]
    q = q_ref[...]
    qpos = j * tq + lax.broadcasted_iota(jnp.int32, (tq, win), 0)
    kpos = start + lax.broadcasted_iota(jnp.int32, (tq, win), 1)
    bias = jnp.where(jnp.abs(kpos - qpos) <= halo, 0.0, MASK_VALUE)
    lane_h = lax.broadcasted_iota(jnp.int32, (1, SEG), 1) >> int(math.log2(HEAD_DIM))
    out = jnp.zeros((tq, SEG), F32)
    lse = jnp.zeros((tq, SEG), F32)
    for hd in range(DIL_HEADS):
        qm = jnp.where(lane_h == hd, q, jnp.zeros_like(q))
        s = _dot_nt(qm, kw) + bias
        m = jnp.max(s, axis=1, keepdims=True)
        p = jnp.exp2(s - m)
        l = jnp.sum(p, axis=1, keepdims=True)
        o = jnp.dot(p.astype(BF16), vw, preferred_element_type=F32) * (1.0 / l)
        out = jnp.where(lane_h == hd, o, out)
        lse = jnp.where(lane_h == hd, m + jnp.log(l) * LOG2E, lse)
    o_ref[...] = out
    lse_ref[...] = lse


def _dilated(src, first_seg, halo, *, tq=256):
    if src.ndim == 3:
        (B, length, _), dil = src.shape, 1
        in_specs = [pl.BlockSpec((None, tq, SEG), lambda b, r, j: (b, j, first_seg)),
                    pl.BlockSpec((None, length, SEG), lambda b, r, j: (b, 0, first_seg + 1)),
                    pl.BlockSpec((None, length, SEG), lambda b, r, j: (b, 0, first_seg + 2))]
    else:
        B, dil, length, _ = src.shape
        in_specs = [pl.BlockSpec((None, None, tq, SEG), lambda b, r, j: (b, r, j, first_seg)),
                    pl.BlockSpec((None, None, length, SEG),
                                 lambda b, r, j: (b, r, 0, first_seg + 1)),
                    pl.BlockSpec((None, None, length, SEG),
                                 lambda b, r, j: (b, r, 0, first_seg + 2))]
    oshape = jax.ShapeDtypeStruct((B, dil, length, SEG), F32)
    ospec = pl.BlockSpec((None, None, tq, SEG), lambda b, r, j: (b, r, j, 0))
    return pl.pallas_call(
        functools.partial(_dil_kernel, tq=tq, halo=halo),
        out_shape=(oshape, oshape),
        grid=(B, dil, length // tq),
        in_specs=in_specs,
        out_specs=(ospec, ospec),
        compiler_params=_cparams(("parallel", "parallel", "arbitrary")),
        name=f"dil_attn_{dil}",
    )(src, src, src)


def _outproj_kernel(x_ref, oa_ref, ob_ref, oc_ref, d0_ref, d1_ref, d2_ref,
                    l0_ref, l1_ref, l2_ref, w_ref, o_ref, cat_ref, *row_scratch):
    row_scratch = list(row_scratch)

    def token_rows(ref):
        dil = ref.shape[0]
        if dil == 1:
            return ref[0]
        buf = row_scratch.pop()
        for r in range(dil):
            slab = ref[r]
            for hf in range(SEG // LANES):
                buf[hf, pl.ds(r, ref.shape[1], stride=dil), :] = slab[:, hf * LANES:(hf + 1) * LANES]
        return jnp.concatenate([buf[hf] for hf in range(SEG // LANES)], axis=1)

    l0, l1, l2 = token_rows(l0_ref), token_rows(l1_ref), token_rows(l2_ref)
    m = jnp.maximum(jnp.maximum(l0, l1), l2)
    e0, e1, e2 = jnp.exp2(l0 - m), jnp.exp2(l1 - m), jnp.exp2(l2 - m)
    od = ((e0 * token_rows(d0_ref) + e1 * token_rows(d1_ref) + e2 * token_rows(d2_ref))
          * (1.0 / (e0 + e1 + e2)))
    cat_ref[:, 0 * SEG:1 * SEG] = oa_ref[...]
    cat_ref[:, 1 * SEG:2 * SEG] = ob_ref[...]
    cat_ref[:, 2 * SEG:3 * SEG] = oc_ref[...]
    cat_ref[:, 3 * SEG:4 * SEG] = od.astype(BF16)
    o_ref[...] = x_ref[...] + jnp.dot(cat_ref[...], w_ref[...], preferred_element_type=F32)


def _outproj(x, oa, ob, oc, dils, w_out, *, tm=512):
    B, S, D = x.shape
    xspec = pl.BlockSpec((None, tm, D), lambda b, i: (b, i, 0))
    sspec = pl.BlockSpec((None, tm, SEG), lambda b, i: (b, i, 0))
    outs = [d[0] for d in dils]
    lses = [d[1] for d in dils]
    dspecs = [pl.BlockSpec((None, a.shape[1], tm // a.shape[1], SEG), lambda b, i: (b, 0, i, 0))
              for a in outs + lses]
    n_scratch = sum(a.shape[1] > 1 for a in outs + lses)
    return pl.pallas_call(
        _outproj_kernel,
        out_shape=jax.ShapeDtypeStruct(x.shape, F32),
        grid=(B, S // tm),
        in_specs=[xspec] + [sspec] * 3 + dspecs + [_resident(w_out.shape)],
        out_specs=xspec,
        scratch_shapes=[pltpu.VMEM((tm, 4 * SEG), BF16)]
        + [pltpu.VMEM((SEG // LANES, tm, LANES), F32)] * n_scratch,
        compiler_params=_cparams(("parallel", "parallel")),
        name="outproj",
    )(x, oa, ob, oc, *outs, *lses, w_out)


def _rope_cs(pos, dim, theta):
    inv_freq = 1.0 / (theta ** (jnp.arange(0, dim, 2, dtype=F32) / dim))
    ang = pos.astype(F32)[:, None] * inv_freq[None, :]
    return jnp.cos(ang), jnp.sin(ang)


def _rope_tables(S):
    pos = jnp.arange(S, dtype=jnp.int32)

    def expand(parts_c, parts_s, reps):
        return (jnp.tile(jnp.concatenate(parts_c, axis=1), (1, reps)),
                jnp.tile(jnp.concatenate(parts_s, axis=1), (1, reps)))

    c, s = _rope_cs(pos, DIFF_QK_DIM // ROPE_FRACTION, ROPE_THETA)
    rest = DIFF_QK_DIM - 2 * c.shape[1]
    df = expand([c, c, jnp.ones((S, rest), F32)], [-s, s, jnp.zeros((S, rest), F32)],
                SEG // DIFF_QK_DIM)
    c, s = _rope_cs(pos, HEAD_DIM // ROPE_FRACTION, ROPE_THETA)
    rest = HEAD_DIM - 2 * c.shape[1]
    dl = expand([c, c, jnp.ones((S, rest), F32)], [-s, s, jnp.zeros((S, rest), F32)],
                SEG // HEAD_DIM)
    cr, sr = _rope_cs(pos // GRID_W, HEAD_DIM // 2, AXIAL_THETA)
    cc, sc = _rope_cs(pos % GRID_W, HEAD_DIM // 2, AXIAL_THETA)
    gq = expand([cr, cr, cc, cc], [-sr, sr, -sc, sc], SEG // HEAD_DIM)
    return df + gq + dl


def _group_matrices():
    lane = np.arange(SEG)
    mats = [(lane[:, None] // g == lane[None, :] // g).astype(np.float32) / g
            for g in (HEAD_DIM, DIFF_QK_DIM)]
    return jnp.asarray(np.stack(mats), BF16)


def _extend_w_in(w_in):
    kv0 = 7 * SEG
    hd = HEAD_DIM
    cols = np.concatenate([
        np.arange(0, kv0),
        kv0 + np.array([0, 0, 1, 1]).repeat(hd) * hd + np.tile(np.arange(hd), 4),
        kv0 + 2 * hd + np.array([0, 0, 1, 1]).repeat(hd) * hd + np.tile(np.arange(hd), 4),
        np.arange(kv0 + 4 * hd, w_in.shape[1]),
    ])
    return w_in[:, cols].astype(BF16)


def _gain_rows(na_q, na_k, df_q, df_k, gq_q, gq_k, dl_q, dl_k):
    def row(g, scale):
        return jnp.tile(g.astype(F32), SEG // g.shape[0]) * scale
    sc64 = HEAD_DIM ** -0.5 * LOG2E
    sc32 = DIFF_QK_DIM ** -0.5 * LOG2E
    return jnp.stack([row(na_q, sc64), row(na_k, 1.0), row(df_q, sc32), row(df_k, 1.0),
                      row(gq_q, sc64), row(gq_k, 1.0), row(dl_q, sc64), row(dl_k, 1.0)])


def kernel(x, ffn1_norm, ffn1_w_gate, ffn1_w_up, ffn1_w_down, mix_norm, w_in, w_out, na_q_norm, na_k_norm, na_rel_bias, diff_q_norm, diff_k_norm, diff_lambda_q1, diff_lambda_k1, diff_lambda_q2, diff_lambda_k2, diff_out_norm, gqa_q_norm, gqa_k_norm, dil_q_norm, dil_k_norm, ffn2_norm, ffn2_w_gate, ffn2_w_up, ffn2_w_down):
    B, S, D = x.shape
    depth = w_in.shape[0]
    rows = S // GRID_W
    tables = _rope_tables(S)
    gmats = _group_matrices()
    ones_row = jnp.ones((1, SEG), F32)
    zero_lamp = jnp.zeros((4, DIFF_QK_DIM), F32)
    zero_li = jnp.zeros((1, 1), F32)

    for l in range(depth):
        x = _ffn(x, ffn1_norm[l], ffn1_w_gate[l].astype(BF16), ffn1_w_up[l].astype(BF16),
                 ffn1_w_down[l].astype(BF16))

        gains = _gain_rows(na_q_norm[l], na_k_norm[l], diff_q_norm[l], diff_k_norm[l],
                           gqa_q_norm[l], gqa_k_norm[l], dil_q_norm[l], dil_k_norm[l])
        qkv, vt_df, vt_gq, *dl_classes = _inproj(x, mix_norm[l], _extend_w_in(w_in[l]), gmats,
                                                 gains, tables)

        o_a = _na(qkv, _na_table(na_rel_bias[l], rows))
        lamp = jnp.stack([diff_lambda_q1[l], diff_lambda_k1[l], diff_lambda_q2[l],
                          diff_lambda_k2[l]]).astype(F32)
        li = jnp.full((1, 1), 0.8 - 0.6 * math.exp(-0.3 * l), F32)
        ogain = jnp.tile(diff_out_norm[l].astype(F32), SEG // HEAD_DIM).reshape(1, SEG)
        o_b = _flash(qkv, vt_df, (SEG_DF_Q, SEG_DF_K), gmats[0], ogain, lamp, li,
                     gains[2], gains[3], groups=2 * DIFF_HEADS, diff=True)
        o_c = _flash(qkv, vt_gq, (SEG_GQ_Q, SEG_GQ_K), gmats[0], ones_row, zero_lamp,
                     zero_li, gains[4], gains[5], groups=GQA_Q_HEADS, diff=False)
        dl_src = {1: (qkv, SEG_DL_Q)}
        dl_src.update({d: (a, 0) for d, a in zip(DILATIONS, dl_classes)})
        dils = [_dilated(*dl_src[dil], (window // 2) // dil) for window, dil in DIL_PAIRS]
        x = _outproj(x, o_a, o_b, o_c, dils, w_out[l].astype(BF16))

        x = _ffn(x, ffn2_norm[l], ffn2_w_gate[l].astype(BF16), ffn2_w_up[l].astype(BF16),
                 ffn2_w_down[l].astype(BF16))
    return x
```

```python
import functools
import math

import numpy as np
import jax
import jax.numpy as jnp
from jax import lax
from jax.experimental import pallas as pl
from jax.experimental.pallas import tpu as pltpu

F32 = jnp.float32
BF16 = jnp.bfloat16

GRID_W = 64
HEAD_DIM = 64
RMS_EPS = 1e-6
MASK_VALUE = -1e30
LOG2E = math.log2(math.e)

NA_HEADS = 4
NA_WIN_H = 8
NA_WIN_W = 16
DIFF_HEADS = 4
DIFF_QK_DIM = HEAD_DIM // 2
GQA_Q_HEADS = 4
GQA_KV_HEADS = 2
AXIAL_THETA = 10000.0
DIL_HEADS = 4
DIL_PAIRS = ((128, 1), (512, 4), (2048, 16))
DILATIONS = tuple(d for _, d in DIL_PAIRS if d > 1)
ROPE_THETA = 500000.0
ROPE_FRACTION = 4

LANES = 128
SEG = 256
(SEG_NA_Q, SEG_NA_K, SEG_NA_V, SEG_DF_Q, SEG_DF_K, SEG_DF_V,
 SEG_GQ_Q, SEG_GQ_K, SEG_GQ_V, SEG_DL_Q, SEG_DL_K, SEG_DL_V) = range(12)
N_SEG = 12
EXT_WIDTH = N_SEG * SEG

VMEM_LIMIT = 52 * 1024 * 1024

VT_ROWS = HEAD_DIM + 16
COL_BLOCK = 256
QK_AHEAD = 4
KV_UNROLL = 4
FLASH_COLS = 4096
FLASH_MAX_SHIFT = 40.0

NA_QROWS = 8
NA_KROWS = 16


def _cparams(sem):
    return pltpu.CompilerParams(dimension_semantics=sem, vmem_limit_bytes=VMEM_LIMIT)


def _resident(shape):
    nd = len(shape)
    return pl.BlockSpec(shape, lambda *_: (0,) * nd, pipeline_mode=pl.Buffered(1))


def _rms_rows(x, gain):
    ms = jnp.mean(x * x, axis=-1, keepdims=True)
    return x * lax.rsqrt(ms + RMS_EPS) * gain


def _group_mean_sq(y, gmat):
    sq = y * y
    hi = sq.astype(BF16)
    lo = (sq - hi.astype(F32)).astype(BF16)
    return (jnp.dot(hi, gmat, preferred_element_type=F32)
            + jnp.dot(lo, gmat, preferred_element_type=F32))


def _dot_nt(a, b):
    return lax.dot_general(a, b, (((1,), (1,)), ((), ())), preferred_element_type=F32)


def _ffn_kernel(x_ref, g_ref, wg_ref, wu_ref, wd_ref, o_ref, *, chunks):
    x = x_ref[...]
    h = _rms_rows(x, g_ref[...]).astype(BF16)
    acc = None
    for c0, c1 in chunks:
        g = jnp.dot(h, wg_ref[:, c0:c1], preferred_element_type=F32)
        u = jnp.dot(h, wu_ref[:, c0:c1], preferred_element_type=F32)
        a = (g * (1.0 / (1.0 + jnp.exp(-g))) * u).astype(BF16)
        d = jnp.dot(a, wd_ref[c0:c1, :], preferred_element_type=F32)
        acc = d if acc is None else acc + d
    o_ref[...] = x + 0.5 * acc


def _ffn(x, gain, wg, wu, wd, *, tm=512, fc=512):
    B, S, D = x.shape
    F = wg.shape[1]
    chunks = tuple((c, min(c + fc, F)) for c in range(0, F, fc))
    xspec = pl.BlockSpec((None, tm, D), lambda b, i: (b, i, 0))
    return pl.pallas_call(
        functools.partial(_ffn_kernel, chunks=chunks),
        out_shape=jax.ShapeDtypeStruct(x.shape, F32),
        grid=(B, S // tm),
        in_specs=[xspec, _resident((1, D)), _resident((D, F)), _resident((D, F)),
                  _resident((F, D))],
        out_specs=xspec,
        compiler_params=_cparams(("parallel", "parallel")),
        name="ffn",
    )(x, gain.reshape(1, D), wg, wu, wd)


_SEG_PLAN = {
    SEG_NA_Q: (0, 0, None), SEG_NA_K: (1, 0, None),
    SEG_DF_Q: (2, 1, "df"), SEG_DF_K: (3, 1, "df"),
    SEG_GQ_Q: (4, 0, "gq"), SEG_GQ_K: (5, 0, "gq"),
    SEG_DL_Q: (6, 0, "dl"), SEG_DL_K: (7, 0, "dl"),
}
_ROPE_HALF = {"df": DIFF_QK_DIM // ROPE_FRACTION // 2, "gq": HEAD_DIM // 4,
              "dl": HEAD_DIM // ROPE_FRACTION // 2}


def _inproj_kernel(x_ref, gn_ref, w_ref, gmat_ref, gains_ref,
                   cdf_ref, sdf_ref, cgq_ref, sgq_ref, cdl_ref, sdl_ref,
                   o_ref, vtdf_ref, vtgq_ref, *dl_refs_and_scratch):
    *dl_refs, ys_ref = dl_refs_and_scratch
    tm = x_ref.shape[0]
    h = _rms_rows(x_ref[...], gn_ref[...]).astype(BF16)
    lane = lax.broadcasted_iota(jnp.int32, (1, SEG), 1)
    tables = {"df": (cdf_ref, sdf_ref), "gq": (cgq_ref, sgq_ref), "dl": (cdl_ref, sdl_ref)}
    vt_plan = {SEG_DF_V: (vtdf_ref, tuple(range(DIFF_HEADS))),
               SEG_GQ_V: (vtgq_ref, tuple(2 * j for j in range(GQA_KV_HEADS)))}
    for seg in range(N_SEG):
        y = jnp.dot(h, w_ref[:, seg * SEG:(seg + 1) * SEG], preferred_element_type=F32)
        if seg in vt_plan:
            vt_ref, lane_heads = vt_plan[seg]
            yt = y.T
            ones = jnp.ones((VT_ROWS - HEAD_DIM, yt.shape[1]), BF16)
            for j, lh in enumerate(lane_heads):
                vt_ref[j * VT_ROWS:j * VT_ROWS + HEAD_DIM, :] = (
                    yt[lh * HEAD_DIM:(lh + 1) * HEAD_DIM].astype(BF16))
                vt_ref[j * VT_ROWS + HEAD_DIM:(j + 1) * VT_ROWS, :] = ones
        plan = _SEG_PLAN.get(seg)
        if plan is not None:
            gi, mi, rope = plan
            ms = _group_mean_sq(y, gmat_ref[mi])
            y = y * lax.rsqrt(ms + RMS_EPS) * gains_ref[gi:gi + 1, :]
            if rope is not None:
                half = _ROPE_HALF[rope]
                c_ref, s_ref = tables[rope]
                first = (lane & (2 * half - 1)) < half
                partner = jnp.where(first, pltpu.roll(y, SEG - half, 1), pltpu.roll(y, half, 1))
                y = y * c_ref[...] + partner * s_ref[...]
        o_ref[:, seg * SEG:(seg + 1) * SEG] = y.astype(BF16)
        if SEG_DL_Q <= seg <= SEG_DL_V:
            for hf in range(SEG // LANES):
                ys_ref[hf] = y[:, hf * LANES:(hf + 1) * LANES]
            col = (seg - SEG_DL_Q) * SEG
            for dl_ref, dil in zip(dl_refs, DILATIONS):
                for r in range(dil):
                    for hf in range(SEG // LANES):
                        dl_ref[r, :, col + hf * LANES:col + (hf + 1) * LANES] = (
                            ys_ref[hf, pl.ds(r, tm // dil, stride=dil), :].astype(BF16))


def _inproj(x, gain, w_ext, gmats, gains, tables, *, tm=512):
    B, S, D = x.shape
    tspec = pl.BlockSpec((tm, SEG), lambda i, b: (i, 0))
    vt_rows = (DIFF_HEADS * VT_ROWS, GQA_KV_HEADS * VT_ROWS)
    return pl.pallas_call(
        _inproj_kernel,
        out_shape=(jax.ShapeDtypeStruct((B, S, EXT_WIDTH), BF16),)
        + tuple(jax.ShapeDtypeStruct((B, S // tm, r, tm), BF16) for r in vt_rows)
        + tuple(jax.ShapeDtypeStruct((B, d, S // d, 3 * SEG), BF16) for d in DILATIONS),
        grid=(S // tm, B),
        in_specs=[pl.BlockSpec((None, tm, D), lambda i, b: (b, i, 0)),
                  _resident((1, D)), _resident((D, EXT_WIDTH)),
                  _resident(gmats.shape), _resident(gains.shape)] + [tspec] * 6,
        out_specs=(pl.BlockSpec((None, tm, EXT_WIDTH), lambda i, b: (b, i, 0)),)
        + tuple(pl.BlockSpec((None, None, r, tm), lambda i, b: (b, i, 0, 0)) for r in vt_rows)
        + tuple(pl.BlockSpec((None, d, tm // d, 3 * SEG), lambda i, b: (b, 0, i, 0))
                for d in DILATIONS),
        scratch_shapes=[pltpu.VMEM((SEG // LANES, tm, LANES), F32)],
        compiler_params=_cparams(("parallel", "parallel")),
        name="inproj",
    )(x, gain.reshape(1, D), w_ext, gmats, gains, *tables)


def _flash_kernel(q_ref, k_ref, vt_ref, gmat_ref, ogain_ref, lamp_ref, li_ref, shift_ref, o_ref,
                  qmt_ref, m_ref, acc_ref, sbuf_ref, *, groups, tq, tk, diff, vheads, bounded):
    gw = SEG // groups
    qt = q_ref[...].astype(F32).T
    row_g = lax.broadcasted_iota(jnp.int32, (SEG, 1), 0) >> int(math.log2(gw))
    for r in range(groups):
        qmt_ref[:, r * tq:(r + 1) * tq] = jnp.where(row_g == r, qt, 0.0).astype(BF16)
    m_ref[...] = jnp.full(m_ref.shape, MASK_VALUE, F32)
    acc_ref[...] = jnp.zeros(acc_ref.shape, F32)
    ncol = groups * tq // COL_BLOCK

    nkv = k_ref.shape[0] // tk
    ahead = sbuf_ref.shape[0]

    def k_block(kv):
        return k_ref[pl.ds(pl.multiple_of(kv * tk, tk), tk), :]

    def qk(k, c):
        return jnp.dot(k, qmt_ref[:, c * COL_BLOCK:(c + 1) * COL_BLOCK],
                       preferred_element_type=F32)

    k0 = k_block(0)
    for c in range(ahead):
        sbuf_ref[c] = qk(k0, c)

    shift = shift_ref[...]

    def body(it, carry):
        pend = [sbuf_ref[c] for c in range(ahead)]
        for u in range(KV_UNROLL):
            kv = it * KV_UNROLL + u
            k = k_block(kv)
            k_next = k_block(jnp.minimum(kv + 1, nkv - 1))
            vt = vt_ref[kv]
            for c in range(ncol):
                hv = (c * COL_BLOCK // tq) * vheads // groups
                cs = slice(c * COL_BLOCK, (c + 1) * COL_BLOCK)
                s = pend.pop(0)
                if c + ahead < ncol:
                    pend.append(qk(k, c + ahead))
                elif u + 1 < KV_UNROLL:
                    pend.append(qk(k_next, c + ahead - ncol))
                else:
                    sbuf_ref[c + ahead - ncol] = qk(k_next, c + ahead - ncol)
                vth = vt[hv * VT_ROWS:(hv + 1) * VT_ROWS]
                if bounded:
                    p = jnp.exp2((s - shift).astype(BF16))
                    acc_ref[:, cs] = acc_ref[:, cs] + jnp.dot(vth, p, preferred_element_type=F32)
                else:
                    m_prev = m_ref[:, cs]
                    m_new = jnp.maximum(m_prev, jnp.max(s, axis=0, keepdims=True))
                    alpha = jnp.exp2(m_prev - m_new)
                    p = jnp.exp2(s - m_new).astype(BF16)
                    acc_ref[:, cs] = (acc_ref[:, cs] * alpha
                                      + jnp.dot(vth, p, preferred_element_type=F32))
                    m_ref[:, cs] = m_new
        return carry

    lax.fori_loop(0, nkv // KV_UNROLL, body, 0)

    acc = acc_ref[...]
    norm = acc[0:HEAD_DIM] * (1.0 / acc[HEAD_DIM:HEAD_DIM + 1])
    if diff:
        lp = lamp_ref[...]
        lam = (jnp.exp(jnp.sum(lp[0:1] * lp[1:2], axis=1, keepdims=True))
               - jnp.exp(jnp.sum(lp[2:3] * lp[3:4], axis=1, keepdims=True)) + li_ref[...])
        pieces = [norm[:, 2 * hd * tq:(2 * hd + 1) * tq]
                  - lam * norm[:, (2 * hd + 1) * tq:(2 * hd + 2) * tq] for hd in range(groups // 2)]
    else:
        pieces = [norm[:, r * tq:(r + 1) * tq] for r in range(groups)]
    out = jnp.concatenate(pieces, axis=0).T
    if diff:
        ms = _group_mean_sq(out, gmat_ref[...])
        out = out * lax.rsqrt(ms + RMS_EPS) * ogain_ref[...] * (1.0 - li_ref[...])
    o_ref[...] = out.astype(BF16)


def _flash(qkv, vt, segs, gmat, ogain, lamp, li, q_gain_row, k_gain_row, *, groups, diff,
           cols=FLASH_COLS):
    B, S, _ = qkv.shape
    _, nkv, vrows, tk = vt.shape
    assert nkv % KV_UNROLL == 0 and S % (cols // groups) == 0
    tq = cols // groups
    qs, ks = segs
    bound = ((SEG // groups) * jnp.max(jnp.abs(q_gain_row)) * jnp.max(jnp.abs(k_gain_row)))
    shift = bound.reshape(1, 1).astype(F32)

    def call(bounded):
        name = ("diff_attn" if diff else "gqa_attn") + ("" if bounded else "_runmax")
        return pl.pallas_call(
            functools.partial(_flash_kernel, groups=groups, tq=tq, tk=tk, diff=diff,
                              vheads=vrows // VT_ROWS, bounded=bounded),
            out_shape=jax.ShapeDtypeStruct((B, S, SEG), BF16),
            grid=(B, S // tq),
            in_specs=[pl.BlockSpec((None, tq, SEG), lambda b, i: (b, i, qs)),
                      pl.BlockSpec((None, S, SEG), lambda b, i: (b, 0, ks)),
                      pl.BlockSpec((None, nkv, vrows, tk), lambda b, i: (b, 0, 0, 0)),
                      _resident(gmat.shape), _resident(ogain.shape), _resident(lamp.shape),
                      _resident(li.shape), _resident(shift.shape)],
            out_specs=pl.BlockSpec((None, tq, SEG), lambda b, i: (b, i, 0)),
            scratch_shapes=[pltpu.VMEM((SEG, cols), BF16), pltpu.VMEM((1, cols), F32),
                            pltpu.VMEM((VT_ROWS, cols), F32),
                            pltpu.VMEM((QK_AHEAD, tk, COL_BLOCK), F32)],
            compiler_params=_cparams(("parallel", "parallel")),
            name=name,
        )(qkv, qkv, vt, gmat, ogain, lamp, li, shift)

    return lax.cond(bound <= FLASH_MAX_SHIFT, lambda: call(True), lambda: call(False))


def _na_kernel(q_ref, k_ref, v_ref, tab_ref, o_ref, *, nq, nk):
    i = pl.program_id(2)
    seq = k_ref.shape[0]
    start = jnp.clip(i * nq - (nk - nq) // 2, 0, seq - nk)
    start = pl.multiple_of(start, (nk - nq) // 2)
    kw = k_ref[pl.ds(start, nk), :]
    vw = v_ref[pl.ds(start, nk), :]
    q = q_ref[...]
    lane_h = lax.broadcasted_iota(jnp.int32, (1, 2 * HEAD_DIM), 1) >> int(math.log2(HEAD_DIM))
    outs = []
    for j in range(2):
        qm = jnp.where(lane_h == j, q, jnp.zeros_like(q))
        s = _dot_nt(qm, kw) + tab_ref[j]
        m = jnp.max(s, axis=1, keepdims=True)
        p = jnp.exp2(s - m)
        l = jnp.sum(p, axis=1, keepdims=True)
        outs.append(jnp.dot(p.astype(BF16), vw, preferred_element_type=F32) * (1.0 / l))
    o_ref[...] = jnp.where(lane_h == 0, outs[0], outs[1]).astype(BF16)


def _na_table(rel_bias, rows):
    kh = min(NA_WIN_H, rows)
    n_ro, n_co = 2 * NA_WIN_H - 1, 2 * NA_WIN_W - 1
    c = np.arange(GRID_W)[:, None]
    kc = np.arange(GRID_W)[None, :]
    cs = np.clip(c - NA_WIN_W // 2, 0, GRID_W - NA_WIN_W)
    col_valid = (kc >= cs) & (kc < cs + NA_WIN_W)
    col_sel = (kc - c + (NA_WIN_W - 1))[..., None] == np.arange(n_co)
    row_sel, row_valid = [], []
    for r0 in (0, NA_QROWS, rows - NA_QROWS):
        kr0 = int(np.clip(r0 - (NA_KROWS - NA_QROWS) // 2, 0, rows - NA_KROWS))
        r = r0 + np.arange(NA_QROWS)[:, None]
        kr = kr0 + np.arange(NA_KROWS)[None, :]
        rs = np.clip(r - kh // 2, 0, rows - kh)
        row_valid.append((kr >= rs) & (kr < rs + kh))
        row_sel.append((kr - r + (NA_WIN_H - 1))[..., None] == np.arange(n_ro))
    row_sel = jnp.asarray(np.stack(row_sel), F32)
    valid = (np.stack(row_valid)[:, None, :, None, :, None]
             & col_valid[None, None, None, :, None, :])
    hi = lax.Precision.HIGHEST
    cols = jnp.einsum("hab,cdb->hacd", rel_bias.astype(F32), jnp.asarray(col_sel, F32),
                      precision=hi)
    full = jnp.einsum("vqka,hacd->vhqckd", row_sel, cols, precision=hi)
    table = jnp.where(valid, full * LOG2E, MASK_VALUE)
    return table.reshape(3, rel_bias.shape[0], NA_QROWS * GRID_W, NA_KROWS * GRID_W)


def _na(qkv, table):
    B, S, _ = qkv.shape
    nq, nk = NA_QROWS * GRID_W, NA_KROWS * GRID_W
    nblk = S // nq
    hp = 2 * HEAD_DIM

    def variant(i):
        return jnp.where(i == 0, 0, jnp.where(i == nblk - 1, 2, 1))

    return pl.pallas_call(
        functools.partial(_na_kernel, nq=nq, nk=nk),
        out_shape=jax.ShapeDtypeStruct((B, S, SEG), BF16),
        grid=(2, B, nblk),
        in_specs=[pl.BlockSpec((None, nq, hp), lambda p, b, i: (b, i, 2 * SEG_NA_Q + p)),
                  pl.BlockSpec((None, S, hp), lambda p, b, i: (b, 0, 2 * SEG_NA_K + p)),
                  pl.BlockSpec((None, S, hp), lambda p, b, i: (b, 0, 2 * SEG_NA_V + p)),
                  pl.BlockSpec((None, 2, nq, nk), lambda p, b, i: (variant(i), p, 0, 0))],
        out_specs=pl.BlockSpec((None, nq, hp), lambda p, b, i: (b, i, p)),
        compiler_params=_cparams(("parallel", "parallel", "arbitrary")),
        name="na_attn",
    )(qkv, qkv, qkv, table)


def _dil_kernel(q_ref, k_ref, v_ref, o_ref, lse_ref, *, tq, halo):
    j = pl.program_id(2)
    length = k_ref.shape[0]
    win = tq + 2 * halo
    start = pl.multiple_of(jnp.clip(j * tq - halo, 0, length - win), halo)
    kw = k_ref[pl.ds(start, win), :]
    vw = v_ref[pl.ds(start, win), :]
    q = q_ref[...]
    qpos = j * tq + lax.broadcasted_iota(jnp.int32, (tq, win), 0)
    kpos = start + lax.broadcasted_iota(jnp.int32, (tq, win), 1)
    bias = jnp.where(jnp.abs(kpos - qpos) <= halo, 0.0, MASK_VALUE)
    lane_h = lax.broadcasted_iota(jnp.int32, (1, SEG), 1) >> int(math.log2(HEAD_DIM))
    out = jnp.zeros((tq, SEG), F32)
    lse = jnp.zeros((tq, SEG), F32)
    for hd in range(DIL_HEADS):
        qm = jnp.where(lane_h == hd, q, jnp.zeros_like(q))
        s = _dot_nt(qm, kw) + bias
        m = jnp.max(s, axis=1, keepdims=True)
        p = jnp.exp2(s - m)
        l = jnp.sum(p, axis=1, keepdims=True)
        o = jnp.dot(p.astype(BF16), vw, preferred_element_type=F32) * (1.0 / l)
        out = jnp.where(lane_h == hd, o, out)
        lse = jnp.where(lane_h == hd, m + jnp.log(l) * LOG2E, lse)
    o_ref[...] = out
    lse_ref[...] = lse


def _dilated(src, first_seg, halo, *, tq=256):
    if src.ndim == 3:
        (B, length, _), dil = src.shape, 1
        in_specs = [pl.BlockSpec((None, tq, SEG), lambda b, r, j: (b, j, first_seg)),
                    pl.BlockSpec((None, length, SEG), lambda b, r, j: (b, 0, first_seg + 1)),
                    pl.BlockSpec((None, length, SEG), lambda b, r, j: (b, 0, first_seg + 2))]
    else:
        B, dil, length, _ = src.shape
        in_specs = [pl.BlockSpec((None, None, tq, SEG), lambda b, r, j: (b, r, j, first_seg)),
                    pl.BlockSpec((None, None, length, SEG),
                                 lambda b, r, j: (b, r, 0, first_seg + 1)),
                    pl.BlockSpec((None, None, length, SEG),
                                 lambda b, r, j: (b, r, 0, first_seg + 2))]
    oshape = jax.ShapeDtypeStruct((B, dil, length, SEG), F32)
    ospec = pl.BlockSpec((None, None, tq, SEG), lambda b, r, j: (b, r, j, 0))
    return pl.pallas_call(
        functools.partial(_dil_kernel, tq=tq, halo=halo),
        out_shape=(oshape, oshape),
        grid=(B, dil, length // tq),
        in_specs=in_specs,
        out_specs=(ospec, ospec),
        compiler_params=_cparams(("parallel", "parallel", "arbitrary")),
        name=f"dil_attn_{dil}",
    )(src, src, src)


def _outproj_kernel(x_ref, oa_ref, ob_ref, oc_ref, d0_ref, d1_ref, d2_ref,
                    l0_ref, l1_ref, l2_ref, w_ref, o_ref, cat_ref, *row_scratch):
    row_scratch = list(row_scratch)

    def token_rows(ref):
        dil = ref.shape[0]
        if dil == 1:
            return ref[0]
        buf = row_scratch.pop()
        for r in range(dil):
            slab = ref[r]
            for hf in range(SEG // LANES):
                buf[hf, pl.ds(r, ref.shape[1], stride=dil), :] = slab[:, hf * LANES:(hf + 1) * LANES]
        return jnp.concatenate([buf[hf] for hf in range(SEG // LANES)], axis=1)

    l0, l1, l2 = token_rows(l0_ref), token_rows(l1_ref), token_rows(l2_ref)
    m = jnp.maximum(jnp.maximum(l0, l1), l2)
    e0, e1, e2 = jnp.exp2(l0 - m), jnp.exp2(l1 - m), jnp.exp2(l2 - m)
    od = ((e0 * token_rows(d0_ref) + e1 * token_rows(d1_ref) + e2 * token_rows(d2_ref))
          * (1.0 / (e0 + e1 + e2)))
    cat_ref[:, 0 * SEG:1 * SEG] = oa_ref[...]
    cat_ref[:, 1 * SEG:2 * SEG] = ob_ref[...]
    cat_ref[:, 2 * SEG:3 * SEG] = oc_ref[...]
    cat_ref[:, 3 * SEG:4 * SEG] = od.astype(BF16)
    o_ref[...] = x_ref[...] + jnp.dot(cat_ref[...], w_ref[...], preferred_element_type=F32)


def _outproj(x, oa, ob, oc, dils, w_out, *, tm=512):
    B, S, D = x.shape
    xspec = pl.BlockSpec((None, tm, D), lambda b, i: (b, i, 0))
    sspec = pl.BlockSpec((None, tm, SEG), lambda b, i: (b, i, 0))
    outs = [d[0] for d in dils]
    lses = [d[1] for d in dils]
    dspecs = [pl.BlockSpec((None, a.shape[1], tm // a.shape[1], SEG), lambda b, i: (b, 0, i, 0))
              for a in outs + lses]
    n_scratch = sum(a.shape[1] > 1 for a in outs + lses)
    return pl.pallas_call(
        _outproj_kernel,
        out_shape=jax.ShapeDtypeStruct(x.shape, F32),
        grid=(B, S // tm),
        in_specs=[xspec] + [sspec] * 3 + dspecs + [_resident(w_out.shape)],
        out_specs=xspec,
        scratch_shapes=[pltpu.VMEM((tm, 4 * SEG), BF16)]
        + [pltpu.VMEM((SEG // LANES, tm, LANES), F32)] * n_scratch,
        compiler_params=_cparams(("parallel", "parallel")),
        name="outproj",
    )(x, oa, ob, oc, *outs, *lses, w_out)


def _rope_cs(pos, dim, theta):
    inv_freq = 1.0 / (theta ** (jnp.arange(0, dim, 2, dtype=F32) / dim))
    ang = pos.astype(F32)[:, None] * inv_freq[None, :]
    return jnp.cos(ang), jnp.sin(ang)


def _rope_tables(S):
    pos = jnp.arange(S, dtype=jnp.int32)

    def expand(parts_c, parts_s, reps):
        return (jnp.tile(jnp.concatenate(parts_c, axis=1), (1, reps)),
                jnp.tile(jnp.concatenate(parts_s, axis=1), (1, reps)))

    c, s = _rope_cs(pos, DIFF_QK_DIM // ROPE_FRACTION, ROPE_THETA)
    rest = DIFF_QK_DIM - 2 * c.shape[1]
    df = expand([c, c, jnp.ones((S, rest), F32)], [-s, s, jnp.zeros((S, rest), F32)],
                SEG // DIFF_QK_DIM)
    c, s = _rope_cs(pos, HEAD_DIM // ROPE_FRACTION, ROPE_THETA)
    rest = HEAD_DIM - 2 * c.shape[1]
    dl = expand([c, c, jnp.ones((S, rest), F32)], [-s, s, jnp.zeros((S, rest), F32)],
                SEG // HEAD_DIM)
    cr, sr = _rope_cs(pos // GRID_W, HEAD_DIM // 2, AXIAL_THETA)
    cc, sc = _rope_cs(pos % GRID_W, HEAD_DIM // 2, AXIAL_THETA)
    gq = expand([cr, cr, cc, cc], [-sr, sr, -sc, sc], SEG // HEAD_DIM)
    return df + gq + dl


def _group_matrices():
    lane = np.arange(SEG)
    mats = [(lane[:, None] // g == lane[None, :] // g).astype(np.float32) / g
            for g in (HEAD_DIM, DIFF_QK_DIM)]
    return jnp.asarray(np.stack(mats), BF16)


def _extend_w_in(w_in):
    kv0 = 7 * SEG
    hd = HEAD_DIM
    cols = np.concatenate([
        np.arange(0, kv0),
        kv0 + np.array([0, 0, 1, 1]).repeat(hd) * hd + np.tile(np.arange(hd), 4),
        kv0 + 2 * hd + np.array([0, 0, 1, 1]).repeat(hd) * hd + np.tile(np.arange(hd), 4),
        np.arange(kv0 + 4 * hd, w_in.shape[1]),
    ])
    return w_in[:, cols].astype(BF16)


def _gain_rows(na_q, na_k, df_q, df_k, gq_q, gq_k, dl_q, dl_k):
    def row(g, scale):
        return jnp.tile(g.astype(F32), SEG // g.shape[0]) * scale
    sc64 = HEAD_DIM ** -0.5 * LOG2E
    sc32 = DIFF_QK_DIM ** -0.5 * LOG2E
    return jnp.stack([row(na_q, sc64), row(na_k, 1.0), row(df_q, sc32), row(df_k, 1.0),
                      row(gq_q, sc64), row(gq_k, 1.0), row(dl_q, sc64), row(dl_k, 1.0)])


def kernel(x, ffn1_norm, ffn1_w_gate, ffn1_w_up, ffn1_w_down, mix_norm, w_in, w_out, na_q_norm, na_k_norm, na_rel_bias, diff_q_norm, diff_k_norm, diff_lambda_q1, diff_lambda_k1, diff_lambda_q2, diff_lambda_k2, diff_out_norm, gqa_q_norm, gqa_k_norm, dil_q_norm, dil_k_norm, ffn2_norm, ffn2_w_gate, ffn2_w_up, ffn2_w_down):
    B, S, D = x.shape
    depth = w_in.shape[0]
    rows = S // GRID_W
    tables = _rope_tables(S)
    gmats = _group_matrices()
    ones_row = jnp.ones((1, SEG), F32)
    zero_lamp = jnp.zeros((4, DIFF_QK_DIM), F32)
    zero_li = jnp.zeros((1, 1), F32)

    for l in range(depth):
        x = _ffn(x, ffn1_norm[l], ffn1_w_gate[l].astype(BF16), ffn1_w_up[l].astype(BF16),
                 ffn1_w_down[l].astype(BF16))

        gains = _gain_rows(na_q_norm[l], na_k_norm[l], diff_q_norm[l], diff_k_norm[l],
                           gqa_q_norm[l], gqa_k_norm[l], dil_q_norm[l], dil_k_norm[l])
        qkv, vt_df, vt_gq, *dl_classes = _inproj(x, mix_norm[l], _extend_w_in(w_in[l]), gmats,
                                                 gains, tables)

        o_a = _na(qkv, _na_table(na_rel_bias[l], rows))
        lamp = jnp.stack([diff_lambda_q1[l], diff_lambda_k1[l], diff_lambda_q2[l],
                          diff_lambda_k2[l]]).astype(F32)
        li = jnp.full((1, 1), 0.8 - 0.6 * math.exp(-0.3 * l), F32)
        ogain = jnp.tile(diff_out_norm[l].astype(F32), SEG // HEAD_DIM).reshape(1, SEG)
        o_b = _flash(qkv, vt_df, (SEG_DF_Q, SEG_DF_K), gmats[0], ogain, lamp, li,
                     gains[2], gains[3], groups=2 * DIFF_HEADS, diff=True)
        o_c = _flash(qkv, vt_gq, (SEG_GQ_Q, SEG_GQ_K), gmats[0], ones_row, zero_lamp,
                     zero_li, gains[4], gains[5], groups=GQA_Q_HEADS, diff=False)
        dl_src = {1: (qkv, SEG_DL_Q)}
        dl_src.update({d: (a, 0) for d, a in zip(DILATIONS, dl_classes)})
        dils = [_dilated(*dl_src[dil], (window // 2) // dil) for window, dil in DIL_PAIRS]
        x = _outproj(x, o_a, o_b, o_c, dils, w_out[l].astype(BF16))

        x = _ffn(x, ffn2_norm[l], ffn2_w_gate[l].astype(BF16), ffn2_w_up[l].astype(BF16),
                 ffn2_w_down[l].astype(BF16))
    return x
```

```python
import functools
import math

import numpy as np
import jax
import jax.numpy as jnp
from jax import lax
from jax.experimental import pallas as pl
from jax.experimental.pallas import tpu as pltpu

F32 = jnp.float32
BF16 = jnp.bfloat16

GRID_W = 64
HEAD_DIM = 64
RMS_EPS = 1e-6
MASK_VALUE = -1e30
LOG2E = math.log2(math.e)

NA_HEADS = 4
NA_WIN_H = 8
NA_WIN_W = 16
DIFF_HEADS = 4
DIFF_QK_DIM = HEAD_DIM // 2
GQA_Q_HEADS = 4
GQA_KV_HEADS = 2
AXIAL_THETA = 10000.0
DIL_HEADS = 4
DIL_PAIRS = ((128, 1), (512, 4), (2048, 16))
DILATIONS = tuple(d for _, d in DIL_PAIRS if d > 1)
ROPE_THETA = 500000.0
ROPE_FRACTION = 4

LANES = 128
SEG = 256
(SEG_NA_Q, SEG_NA_K, SEG_NA_V, SEG_DF_Q, SEG_DF_K, SEG_DF_V,
 SEG_GQ_Q, SEG_GQ_K, SEG_GQ_V, SEG_DL_Q, SEG_DL_K, SEG_DL_V) = range(12)
N_SEG = 12
EXT_WIDTH = N_SEG * SEG

VMEM_LIMIT = 52 * 1024 * 1024

VT_ROWS = HEAD_DIM + 16
COL_BLOCK = 256
QK_AHEAD = 4
KV_UNROLL = 4
FLASH_COLS = 4096
MAX_FIXED_SHIFT = 40.0

NA_QROWS = 8
NA_KROWS = 16


def _cparams(sem):
    return pltpu.CompilerParams(dimension_semantics=sem, vmem_limit_bytes=VMEM_LIMIT)


def _resident(shape):
    nd = len(shape)
    return pl.BlockSpec(shape, lambda *_: (0,) * nd, pipeline_mode=pl.Buffered(1))


def _rms_rows(x, gain):
    ms = jnp.mean(x * x, axis=-1, keepdims=True)
    return x * lax.rsqrt(ms + RMS_EPS) * gain


def _group_mean_sq(y, gmat):
    sq = y * y
    hi = sq.astype(BF16)
    lo = (sq - hi.astype(F32)).astype(BF16)
    return (jnp.dot(hi, gmat, preferred_element_type=F32)
            + jnp.dot(lo, gmat, preferred_element_type=F32))


def _dot_nt(a, b):
    return lax.dot_general(a, b, (((1,), (1,)), ((), ())), preferred_element_type=F32)


def _ffn_kernel(x_ref, g_ref, wg_ref, wu_ref, wd_ref, o_ref, *, chunks):
    x = x_ref[...]
    h = _rms_rows(x, g_ref[...]).astype(BF16)
    acc = None
    for c0, c1 in chunks:
        g = jnp.dot(h, wg_ref[:, c0:c1], preferred_element_type=F32)
        u = jnp.dot(h, wu_ref[:, c0:c1], preferred_element_type=F32)
        a = (g * (1.0 / (1.0 + jnp.exp(-g))) * u).astype(BF16)
        d = jnp.dot(a, wd_ref[c0:c1, :], preferred_element_type=F32)
        acc = d if acc is None else acc + d
    o_ref[...] = x + 0.5 * acc


def _ffn(x, gain, wg, wu, wd, *, tm=512, fc=512):
    B, S, D = x.shape
    F = wg.shape[1]
    chunks = tuple((c, min(c + fc, F)) for c in range(0, F, fc))
    xspec = pl.BlockSpec((None, tm, D), lambda b, i: (b, i, 0))
    return pl.pallas_call(
        functools.partial(_ffn_kernel, chunks=chunks),
        out_shape=jax.ShapeDtypeStruct(x.shape, F32),
        grid=(B, S // tm),
        in_specs=[xspec, _resident((1, D)), _resident((D, F)), _resident((D, F)),
                  _resident((F, D))],
        out_specs=xspec,
        compiler_params=_cparams(("parallel", "parallel")),
        name="ffn",
    )(x, gain.reshape(1, D), wg, wu, wd)


_SEG_PLAN = {
    SEG_NA_Q: (0, 0, None), SEG_NA_K: (1, 0, None),
    SEG_DF_Q: (2, 1, "df"), SEG_DF_K: (3, 1, "df"),
    SEG_GQ_Q: (4, 0, "gq"), SEG_GQ_K: (5, 0, "gq"),
    SEG_DL_Q: (6, 0, "dl"), SEG_DL_K: (7, 0, "dl"),
}
_ROPE_HALF = {"df": DIFF_QK_DIM // ROPE_FRACTION // 2, "gq": HEAD_DIM // 4,
              "dl": HEAD_DIM // ROPE_FRACTION // 2}


def _inproj_kernel(x_ref, gn_ref, w_ref, gmat_ref, gains_ref,
                   cdf_ref, sdf_ref, cgq_ref, sgq_ref, cdl_ref, sdl_ref,
                   o_ref, vtdf_ref, vtgq_ref, *dl_refs_and_scratch):
    *dl_refs, ys_ref = dl_refs_and_scratch
    tm = x_ref.shape[0]
    h = _rms_rows(x_ref[...], gn_ref[...]).astype(BF16)
    lane = lax.broadcasted_iota(jnp.int32, (1, SEG), 1)
    tables = {"df": (cdf_ref, sdf_ref), "gq": (cgq_ref, sgq_ref), "dl": (cdl_ref, sdl_ref)}
    vt_plan = {SEG_DF_V: (vtdf_ref, tuple(range(DIFF_HEADS))),
               SEG_GQ_V: (vtgq_ref, tuple(2 * j for j in range(GQA_KV_HEADS)))}
    for seg in range(N_SEG):
        y = jnp.dot(h, w_ref[:, seg * SEG:(seg + 1) * SEG], preferred_element_type=F32)
        if seg in vt_plan:
            vt_ref, lane_heads = vt_plan[seg]
            yt = y.T
            ones = jnp.ones((VT_ROWS - HEAD_DIM, yt.shape[1]), BF16)
            for j, lh in enumerate(lane_heads):
                vt_ref[j * VT_ROWS:j * VT_ROWS + HEAD_DIM, :] = (
                    yt[lh * HEAD_DIM:(lh + 1) * HEAD_DIM].astype(BF16))
                vt_ref[j * VT_ROWS + HEAD_DIM:(j + 1) * VT_ROWS, :] = ones
        plan = _SEG_PLAN.get(seg)
        if plan is not None:
            gi, mi, rope = plan
            ms = _group_mean_sq(y, gmat_ref[mi])
            y = y * lax.rsqrt(ms + RMS_EPS) * gains_ref[gi:gi + 1, :]
            if rope is not None:
                half = _ROPE_HALF[rope]
                c_ref, s_ref = tables[rope]
                first = (lane & (2 * half - 1)) < half
                partner = jnp.where(first, pltpu.roll(y, SEG - half, 1), pltpu.roll(y, half, 1))
                y = y * c_ref[...] + partner * s_ref[...]
        o_ref[:, seg * SEG:(seg + 1) * SEG] = y.astype(BF16)
        if SEG_DL_Q <= seg <= SEG_DL_V:
            for hf in range(SEG // LANES):
                ys_ref[hf] = y[:, hf * LANES:(hf + 1) * LANES]
            col = (seg - SEG_DL_Q) * SEG
            for dl_ref, dil in zip(dl_refs, DILATIONS):
                for r in range(dil):
                    for hf in range(SEG // LANES):
                        dl_ref[r, :, col + hf * LANES:col + (hf + 1) * LANES] = (
                            ys_ref[hf, pl.ds(r, tm // dil, stride=dil), :].astype(BF16))


def _inproj(x, gain, w_ext, gmats, gains, tables, *, tm=512):
    B, S, D = x.shape
    tspec = pl.BlockSpec((tm, SEG), lambda i, b: (i, 0))
    vt_rows = (DIFF_HEADS * VT_ROWS, GQA_KV_HEADS * VT_ROWS)
    return pl.pallas_call(
        _inproj_kernel,
        out_shape=(jax.ShapeDtypeStruct((B, S, EXT_WIDTH), BF16),)
        + tuple(jax.ShapeDtypeStruct((B, S // tm, r, tm), BF16) for r in vt_rows)
        + tuple(jax.ShapeDtypeStruct((B, d, S // d, 3 * SEG), BF16) for d in DILATIONS),
        grid=(S // tm, B),
        in_specs=[pl.BlockSpec((None, tm, D), lambda i, b: (b, i, 0)),
                  _resident((1, D)), _resident((D, EXT_WIDTH)),
                  _resident(gmats.shape), _resident(gains.shape)] + [tspec] * 6,
        out_specs=(pl.BlockSpec((None, tm, EXT_WIDTH), lambda i, b: (b, i, 0)),)
        + tuple(pl.BlockSpec((None, None, r, tm), lambda i, b: (b, i, 0, 0)) for r in vt_rows)
        + tuple(pl.BlockSpec((None, d, tm // d, 3 * SEG), lambda i, b: (b, 0, i, 0))
                for d in DILATIONS),
        scratch_shapes=[pltpu.VMEM((SEG // LANES, tm, LANES), F32)],
        compiler_params=_cparams(("parallel", "parallel")),
        name="inproj",
    )(x, gain.reshape(1, D), w_ext, gmats, gains, *tables)


def _flash_kernel(q_ref, k_ref, vt_ref, gmat_ref, ogain_ref, lamp_ref, li_ref, shift_ref, o_ref,
                  qmt_ref, m_ref, acc_ref, sbuf_ref, *, groups, tq, tk, diff, vheads, bounded):
    gw = SEG // groups
    qt = q_ref[...].astype(F32).T
    row_g = lax.broadcasted_iota(jnp.int32, (SEG, 1), 0) >> int(math.log2(gw))
    for r in range(groups):
        qmt_ref[:, r * tq:(r + 1) * tq] = jnp.where(row_g == r, qt, 0.0).astype(BF16)
    m_ref[...] = jnp.full(m_ref.shape, MASK_VALUE, F32)
    acc_ref[...] = jnp.zeros(acc_ref.shape, F32)
    ncol = groups * tq // COL_BLOCK

    nkv = k_ref.shape[0] // tk
    ahead = sbuf_ref.shape[0]

    def k_block(kv):
        return k_ref[pl.ds(pl.multiple_of(kv * tk, tk), tk), :]

    def qk(k, c):
        return jnp.dot(k, qmt_ref[:, c * COL_BLOCK:(c + 1) * COL_BLOCK],
                       preferred_element_type=F32)

    k0 = k_block(0)
    for c in range(ahead):
        sbuf_ref[c] = qk(k0, c)

    shift = shift_ref[...]

    def body(it, carry):
        pend = [sbuf_ref[c] for c in range(ahead)]
        for u in range(KV_UNROLL):
            kv = it * KV_UNROLL + u
            k = k_block(kv)
            k_next = k_block(jnp.minimum(kv + 1, nkv - 1))
            vt = vt_ref[kv]
            for c in range(ncol):
                hv = (c * COL_BLOCK // tq) * vheads // groups
                cs = slice(c * COL_BLOCK, (c + 1) * COL_BLOCK)
                s = pend.pop(0)
                if c + ahead < ncol:
                    pend.append(qk(k, c + ahead))
                elif u + 1 < KV_UNROLL:
                    pend.append(qk(k_next, c + ahead - ncol))
                else:
                    sbuf_ref[c + ahead - ncol] = qk(k_next, c + ahead - ncol)
                vth = vt[hv * VT_ROWS:(hv + 1) * VT_ROWS]
                if bounded:
                    p = jnp.exp2((s - shift).astype(BF16))
                    acc_ref[:, cs] = acc_ref[:, cs] + jnp.dot(vth, p, preferred_element_type=F32)
                else:
                    m_prev = m_ref[:, cs]
                    m_new = jnp.maximum(m_prev, jnp.max(s, axis=0, keepdims=True))
                    alpha = jnp.exp2(m_prev - m_new)
                    p = jnp.exp2(s - m_new).astype(BF16)
                    acc_ref[:, cs] = (acc_ref[:, cs] * alpha
                                      + jnp.dot(vth, p, preferred_element_type=F32))
                    m_ref[:, cs] = m_new
        return carry

    lax.fori_loop(0, nkv // KV_UNROLL, body, 0)

    acc = acc_ref[...]
    norm = acc[0:HEAD_DIM] * (1.0 / acc[HEAD_DIM:HEAD_DIM + 1])
    if diff:
        lp = lamp_ref[...]
        lam = (jnp.exp(jnp.sum(lp[0:1] * lp[1:2], axis=1, keepdims=True))
               - jnp.exp(jnp.sum(lp[2:3] * lp[3:4], axis=1, keepdims=True)) + li_ref[...])
        pieces = [norm[:, 2 * hd * tq:(2 * hd + 1) * tq]
                  - lam * norm[:, (2 * hd + 1) * tq:(2 * hd + 2) * tq] for hd in range(groups // 2)]
    else:
        pieces = [norm[:, r * tq:(r + 1) * tq] for r in range(groups)]
    out = jnp.concatenate(pieces, axis=0).T
    if diff:
        ms = _group_mean_sq(out, gmat_ref[...])
        out = out * lax.rsqrt(ms + RMS_EPS) * ogain_ref[...] * (1.0 - li_ref[...])
    o_ref[...] = out.astype(BF16)


def _flash(qkv, vt, segs, gmat, ogain, lamp, li, shift, *, groups, diff, bounded,
           cols=FLASH_COLS):
    B, S, _ = qkv.shape
    _, nkv, vrows, tk = vt.shape
    assert nkv % KV_UNROLL == 0 and S % (cols // groups) == 0
    tq = cols // groups
    qs, ks = segs
    return pl.pallas_call(
        functools.partial(_flash_kernel, groups=groups, tq=tq, tk=tk, diff=diff,
                          vheads=vrows // VT_ROWS, bounded=bounded),
        out_shape=jax.ShapeDtypeStruct((B, S, SEG), BF16),
        grid=(B, S // tq),
        in_specs=[pl.BlockSpec((None, tq, SEG), lambda b, i: (b, i, qs)),
                  pl.BlockSpec((None, S, SEG), lambda b, i: (b, 0, ks)),
                  pl.BlockSpec((None, nkv, vrows, tk), lambda b, i: (b, 0, 0, 0)),
                  _resident(gmat.shape), _resident(ogain.shape), _resident(lamp.shape),
                  _resident(li.shape), _resident(shift.shape)],
        out_specs=pl.BlockSpec((None, tq, SEG), lambda b, i: (b, i, 0)),
        scratch_shapes=[pltpu.VMEM((SEG, cols), BF16), pltpu.VMEM((1, cols), F32),
                        pltpu.VMEM((VT_ROWS, cols), F32),
                        pltpu.VMEM((QK_AHEAD, tk, COL_BLOCK), F32)],
        compiler_params=_cparams(("parallel", "parallel")),
        name=("diff_attn" if diff else "gqa_attn") + ("" if bounded else "_runmax"),
    )(qkv, qkv, vt, gmat, ogain, lamp, li, shift)


def _pv_with_denominator(p, vw, own):
    return jnp.dot(p, jnp.where(own, vw, jnp.ones_like(vw)), preferred_element_type=F32)


def _na_kernel(q_ref, k_ref, v_ref, tab_ref, shift_ref, o_ref, *, nq, nk, bounded):
    i = pl.program_id(2)
    seq = k_ref.shape[0]
    start = jnp.clip(i * nq - (nk - nq) // 2, 0, seq - nk)
    start = pl.multiple_of(start, (nk - nq) // 2)
    kw = k_ref[pl.ds(start, nk), :]
    vw = v_ref[pl.ds(start, nk), :]
    q = q_ref[...]
    lane_h = lax.broadcasted_iota(jnp.int32, (1, 2 * HEAD_DIM), 1) >> int(math.log2(HEAD_DIM))
    outs = []
    for j in range(2):
        qm = jnp.where(lane_h == j, q, jnp.zeros_like(q))
        s = _dot_nt(qm, kw) + tab_ref[j]
        if bounded:
            p = jnp.exp2(s - shift_ref[...]).astype(BF16)
            pv = _pv_with_denominator(p, vw, lane_h == j)
            outs.append(pv * (1.0 / pltpu.roll(pv, HEAD_DIM, 1)))
        else:
            m = jnp.max(s, axis=1, keepdims=True)
            p = jnp.exp2(s - m)
            l = jnp.sum(p, axis=1, keepdims=True)
            outs.append(jnp.dot(p.astype(BF16), vw, preferred_element_type=F32) * (1.0 / l))
    o_ref[...] = jnp.where(lane_h == 0, outs[0], outs[1]).astype(BF16)


def _na_table(rel_bias, rows):
    kh = min(NA_WIN_H, rows)
    n_ro, n_co = 2 * NA_WIN_H - 1, 2 * NA_WIN_W - 1
    c = np.arange(GRID_W)[:, None]
    kc = np.arange(GRID_W)[None, :]
    cs = np.clip(c - NA_WIN_W // 2, 0, GRID_W - NA_WIN_W)
    col_valid = (kc >= cs) & (kc < cs + NA_WIN_W)
    col_sel = (kc - c + (NA_WIN_W - 1))[..., None] == np.arange(n_co)
    row_sel, row_valid = [], []
    for r0 in (0, NA_QROWS, rows - NA_QROWS):
        kr0 = int(np.clip(r0 - (NA_KROWS - NA_QROWS) // 2, 0, rows - NA_KROWS))
        r = r0 + np.arange(NA_QROWS)[:, None]
        kr = kr0 + np.arange(NA_KROWS)[None, :]
        rs = np.clip(r - kh // 2, 0, rows - kh)
        row_valid.append((kr >= rs) & (kr < rs + kh))
        row_sel.append((kr - r + (NA_WIN_H - 1))[..., None] == np.arange(n_ro))
    row_sel = jnp.asarray(np.stack(row_sel), F32)
    valid = (np.stack(row_valid)[:, None, :, None, :, None]
             & col_valid[None, None, None, :, None, :])
    hi = lax.Precision.HIGHEST
    cols = jnp.einsum("hab,cdb->hacd", rel_bias.astype(F32), jnp.asarray(col_sel, F32),
                      precision=hi)
    full = jnp.einsum("vqka,hacd->vhqckd", row_sel, cols, precision=hi)
    table = jnp.where(valid, full * LOG2E, MASK_VALUE)
    return table.reshape(3, rel_bias.shape[0], NA_QROWS * GRID_W, NA_KROWS * GRID_W)


def _na(qkv, table, shift, *, bounded):
    B, S, _ = qkv.shape
    nq, nk = NA_QROWS * GRID_W, NA_KROWS * GRID_W
    nblk = S // nq
    hp = 2 * HEAD_DIM

    def variant(i):
        return jnp.where(i == 0, 0, jnp.where(i == nblk - 1, 2, 1))

    return pl.pallas_call(
        functools.partial(_na_kernel, nq=nq, nk=nk, bounded=bounded),
        out_shape=jax.ShapeDtypeStruct((B, S, SEG), BF16),
        grid=(2, B, nblk),
        in_specs=[pl.BlockSpec((None, nq, hp), lambda p, b, i: (b, i, 2 * SEG_NA_Q + p)),
                  pl.BlockSpec((None, S, hp), lambda p, b, i: (b, 0, 2 * SEG_NA_K + p)),
                  pl.BlockSpec((None, S, hp), lambda p, b, i: (b, 0, 2 * SEG_NA_V + p)),
                  pl.BlockSpec((None, 2, nq, nk), lambda p, b, i: (variant(i), p, 0, 0)),
                  _resident(shift.shape)],
        out_specs=pl.BlockSpec((None, nq, hp), lambda p, b, i: (b, i, p)),
        compiler_params=_cparams(("parallel", "parallel", "arbitrary")),
        name="na_attn" if bounded else "na_attn_rowmax",
    )(qkv, qkv, qkv, table, shift)


def _dil_kernel(q_ref, k_ref, v_ref, shift_ref, o_ref, lse_ref, *, tq, halo, bounded):
    j = pl.program_id(2)
    length = k_ref.shape[0]
    win = tq + 2 * halo
    start = pl.multiple_of(jnp.clip(j * tq - halo, 0, length - win), halo)
    kw = k_ref[pl.ds(start, win), :]
    vw = v_ref[pl.ds(start, win), :]
    q = q_ref[...]
    qpos = j * tq + lax.broadcasted_iota(jnp.int32, (tq, win), 0)
    kpos = start + lax.broadcasted_iota(jnp.int32, (tq, win), 1)
    bias = jnp.where(jnp.abs(kpos - qpos) <= halo, 0.0, MASK_VALUE)
    lane_h = lax.broadcasted_iota(jnp.int32, (1, SEG), 1) >> int(math.log2(HEAD_DIM))
    out = jnp.zeros((tq, SEG), F32)
    lse = jnp.zeros((tq, SEG), F32)
    for hd in range(DIL_HEADS):
        qm = jnp.where(lane_h == hd, q, jnp.zeros_like(q))
        s = _dot_nt(qm, kw) + bias
        if bounded:
            p = jnp.exp2(s - shift_ref[...]).astype(BF16)
            pv = _pv_with_denominator(p, vw, lane_h == hd)
            out = jnp.where(lane_h == hd, pv, out)
            lse = jnp.where(lane_h == hd, pltpu.roll(pv, HEAD_DIM, 1), lse)
        else:
            m = jnp.max(s, axis=1, keepdims=True)
            p = jnp.exp2(s - m)
            l = jnp.sum(p, axis=1, keepdims=True)
            o = jnp.dot(p.astype(BF16), vw, preferred_element_type=F32) * (1.0 / l)
            out = jnp.where(lane_h == hd, o, out)
            lse = jnp.where(lane_h == hd, m + jnp.log(l) * LOG2E, lse)
    o_ref[...] = out
    lse_ref[...] = lse


def _dilated(src, first_seg, halo, shift, *, bounded, tq=256):
    if src.ndim == 3:
        (B, length, _), dil = src.shape, 1
        in_specs = [pl.BlockSpec((None, tq, SEG), lambda b, r, j: (b, j, first_seg)),
                    pl.BlockSpec((None, length, SEG), lambda b, r, j: (b, 0, first_seg + 1)),
                    pl.BlockSpec((None, length, SEG), lambda b, r, j: (b, 0, first_seg + 2))]
    else:
        B, dil, length, _ = src.shape
        in_specs = [pl.BlockSpec((None, None, tq, SEG), lambda b, r, j: (b, r, j, first_seg)),
                    pl.BlockSpec((None, None, length, SEG),
                                 lambda b, r, j: (b, r, 0, first_seg + 1)),
                    pl.BlockSpec((None, None, length, SEG),
                                 lambda b, r, j: (b, r, 0, first_seg + 2))]
    oshape = jax.ShapeDtypeStruct((B, dil, length, SEG), F32)
    ospec = pl.BlockSpec((None, None, tq, SEG), lambda b, r, j: (b, r, j, 0))
    return pl.pallas_call(
        functools.partial(_dil_kernel, tq=tq, halo=halo, bounded=bounded),
        out_shape=(oshape, oshape),
        grid=(B, dil, length // tq),
        in_specs=in_specs + [_resident(shift.shape)],
        out_specs=(ospec, ospec),
        compiler_params=_cparams(("parallel", "parallel", "arbitrary")),
        name=f"dil_attn_{dil}" + ("" if bounded else "_rowmax"),
    )(src, src, src, shift)


def _outproj_kernel(x_ref, oa_ref, ob_ref, oc_ref, d0_ref, d1_ref, d2_ref,
                    l0_ref, l1_ref, l2_ref, w_ref, o_ref, cat_ref, *row_scratch, bounded):
    row_scratch = list(row_scratch)

    def token_rows(ref):
        dil = ref.shape[0]
        if dil == 1:
            return ref[0]
        buf = row_scratch.pop()
        for r in range(dil):
            slab = ref[r]
            for hf in range(SEG // LANES):
                buf[hf, pl.ds(r, ref.shape[1], stride=dil), :] = slab[:, hf * LANES:(hf + 1) * LANES]
        return jnp.concatenate([buf[hf] for hf in range(SEG // LANES)], axis=1)

    l0, l1, l2 = token_rows(l0_ref), token_rows(l1_ref), token_rows(l2_ref)
    if bounded:
        od = ((token_rows(d0_ref) + token_rows(d1_ref) + token_rows(d2_ref))
              * (1.0 / (l0 + l1 + l2)))
    else:
        m = jnp.maximum(jnp.maximum(l0, l1), l2)
        e0, e1, e2 = jnp.exp2(l0 - m), jnp.exp2(l1 - m), jnp.exp2(l2 - m)
        od = ((e0 * token_rows(d0_ref) + e1 * token_rows(d1_ref) + e2 * token_rows(d2_ref))
              * (1.0 / (e0 + e1 + e2)))
    cat_ref[:, 0 * SEG:1 * SEG] = oa_ref[...]
    cat_ref[:, 1 * SEG:2 * SEG] = ob_ref[...]
    cat_ref[:, 2 * SEG:3 * SEG] = oc_ref[...]
    cat_ref[:, 3 * SEG:4 * SEG] = od.astype(BF16)
    o_ref[...] = x_ref[...] + jnp.dot(cat_ref[...], w_ref[...], preferred_element_type=F32)


def _outproj(x, oa, ob, oc, dils, w_out, *, bounded, tm=512):
    B, S, D = x.shape
    xspec = pl.BlockSpec((None, tm, D), lambda b, i: (b, i, 0))
    sspec = pl.BlockSpec((None, tm, SEG), lambda b, i: (b, i, 0))
    outs = [d[0] for d in dils]
    lses = [d[1] for d in dils]
    dspecs = [pl.BlockSpec((None, a.shape[1], tm // a.shape[1], SEG), lambda b, i: (b, 0, i, 0))
              for a in outs + lses]
    n_scratch = sum(a.shape[1] > 1 for a in outs + lses)
    return pl.pallas_call(
        functools.partial(_outproj_kernel, bounded=bounded),
        out_shape=jax.ShapeDtypeStruct(x.shape, F32),
        grid=(B, S // tm),
        in_specs=[xspec] + [sspec] * 3 + dspecs + [_resident(w_out.shape)],
        out_specs=xspec,
        scratch_shapes=[pltpu.VMEM((tm, 4 * SEG), BF16)]
        + [pltpu.VMEM((SEG // LANES, tm, LANES), F32)] * n_scratch,
        compiler_params=_cparams(("parallel", "parallel")),
        name="outproj" if bounded else "outproj_lse",
    )(x, oa, ob, oc, *outs, *lses, w_out)


def _rope_cs(pos, dim, theta):
    inv_freq = 1.0 / (theta ** (jnp.arange(0, dim, 2, dtype=F32) / dim))
    ang = pos.astype(F32)[:, None] * inv_freq[None, :]
    return jnp.cos(ang), jnp.sin(ang)


def _rope_tables(S):
    pos = jnp.arange(S, dtype=jnp.int32)

    def expand(parts_c, parts_s, reps):
        return (jnp.tile(jnp.concatenate(parts_c, axis=1), (1, reps)),
                jnp.tile(jnp.concatenate(parts_s, axis=1), (1, reps)))

    c, s = _rope_cs(pos, DIFF_QK_DIM // ROPE_FRACTION, ROPE_THETA)
    rest = DIFF_QK_DIM - 2 * c.shape[1]
    df = expand([c, c, jnp.ones((S, rest), F32)], [-s, s, jnp.zeros((S, rest), F32)],
                SEG // DIFF_QK_DIM)
    c, s = _rope_cs(pos, HEAD_DIM // ROPE_FRACTION, ROPE_THETA)
    rest = HEAD_DIM - 2 * c.shape[1]
    dl = expand([c, c, jnp.ones((S, rest), F32)], [-s, s, jnp.zeros((S, rest), F32)],
                SEG // HEAD_DIM)
    cr, sr = _rope_cs(pos // GRID_W, HEAD_DIM // 2, AXIAL_THETA)
    cc, sc = _rope_cs(pos % GRID_W, HEAD_DIM // 2, AXIAL_THETA)
    gq = expand([cr, cr, cc, cc], [-sr, sr, -sc, sc], SEG // HEAD_DIM)
    return df + gq + dl


def _group_matrices():
    lane = np.arange(SEG)
    mats = [(lane[:, None] // g == lane[None, :] // g).astype(np.float32) / g
            for g in (HEAD_DIM, DIFF_QK_DIM)]
    return jnp.asarray(np.stack(mats), BF16)


def _extend_w_in(w_in):
    kv0 = 7 * SEG
    hd = HEAD_DIM
    cols = np.concatenate([
        np.arange(0, kv0),
        kv0 + np.array([0, 0, 1, 1]).repeat(hd) * hd + np.tile(np.arange(hd), 4),
        kv0 + 2 * hd + np.array([0, 0, 1, 1]).repeat(hd) * hd + np.tile(np.arange(hd), 4),
        np.arange(kv0 + 4 * hd, w_in.shape[1]),
    ])
    return w_in[:, cols].astype(BF16)


def _gain_rows(na_q, na_k, df_q, df_k, gq_q, gq_k, dl_q, dl_k):
    def row(g, scale):
        return jnp.tile(g.astype(F32), SEG // g.shape[0]) * scale
    sc64 = HEAD_DIM ** -0.5 * LOG2E
    sc32 = DIFF_QK_DIM ** -0.5 * LOG2E
    return jnp.stack([row(na_q, sc64), row(na_k, 1.0), row(df_q, sc32), row(df_k, 1.0),
                      row(gq_q, sc64), row(gq_k, 1.0), row(dl_q, sc64), row(dl_k, 1.0)])


def kernel(x, ffn1_norm, ffn1_w_gate, ffn1_w_up, ffn1_w_down, mix_norm, w_in, w_out, na_q_norm, na_k_norm, na_rel_bias, diff_q_norm, diff_k_norm, diff_lambda_q1, diff_lambda_k1, diff_lambda_q2, diff_lambda_k2, diff_out_norm, gqa_q_norm, gqa_k_norm, dil_q_norm, dil_k_norm, ffn2_norm, ffn2_w_gate, ffn2_w_up, ffn2_w_down):
    B, S, D = x.shape
    depth = w_in.shape[0]
    rows = S // GRID_W
    tables = _rope_tables(S)
    gmats = _group_matrices()
    ones_row = jnp.ones((1, SEG), F32)
    zero_lamp = jnp.zeros((4, DIFF_QK_DIM), F32)
    zero_li = jnp.zeros((1, 1), F32)

    for l in range(depth):
        x = _ffn(x, ffn1_norm[l], ffn1_w_gate[l].astype(BF16), ffn1_w_up[l].astype(BF16),
                 ffn1_w_down[l].astype(BF16))

        gains = _gain_rows(na_q_norm[l], na_k_norm[l], diff_q_norm[l], diff_k_norm[l],
                           gqa_q_norm[l], gqa_k_norm[l], dil_q_norm[l], dil_k_norm[l])
        qkv, vt_df, vt_gq, *dl_classes = _inproj(x, mix_norm[l], _extend_w_in(w_in[l]), gmats,
                                                 gains, tables)

        na_table = _na_table(na_rel_bias[l], rows)
        lamp = jnp.stack([diff_lambda_q1[l], diff_lambda_k1[l], diff_lambda_q2[l],
                          diff_lambda_k2[l]]).astype(F32)
        li = jnp.full((1, 1), 0.8 - 0.6 * math.exp(-0.3 * l), F32)
        ogain = jnp.tile(diff_out_norm[l].astype(F32), SEG // HEAD_DIM).reshape(1, SEG)
        dl_src = {1: (qkv, SEG_DL_Q)}
        dl_src.update({d: (a, 0) for d, a in zip(DILATIONS, dl_classes)})
        w_out_l = w_out[l].astype(BF16)

        def score_bound(qi, width):
            return width * jnp.max(jnp.abs(gains[qi])) * jnp.max(jnp.abs(gains[qi + 1]))
        bounds = {"na": score_bound(0, HEAD_DIM) + LOG2E * jnp.max(jnp.abs(na_rel_bias[l])),
                  "df": score_bound(2, DIFF_QK_DIM), "gq": score_bound(4, HEAD_DIM),
                  "dl": score_bound(6, HEAD_DIM)}
        shifts = {k: v.reshape(1, 1).astype(F32) for k, v in bounds.items()}

        def mix(bounded, x=x, qkv=qkv, vt_df=vt_df, vt_gq=vt_gq, dl_src=dl_src, shifts=shifts,
                na_table=na_table, ogain=ogain, lamp=lamp, li=li, w_out_l=w_out_l):
            o_a = _na(qkv, na_table, shifts["na"], bounded=bounded)
            o_b = _flash(qkv, vt_df, (SEG_DF_Q, SEG_DF_K), gmats[0], ogain, lamp, li,
                         shifts["df"], groups=2 * DIFF_HEADS, diff=True, bounded=bounded)
            o_c = _flash(qkv, vt_gq, (SEG_GQ_Q, SEG_GQ_K), gmats[0], ones_row, zero_lamp,
                         zero_li, shifts["gq"], groups=GQA_Q_HEADS, diff=False, bounded=bounded)
            dils = [_dilated(*dl_src[dil], (window // 2) // dil, shifts["dl"], bounded=bounded)
                    for window, dil in DIL_PAIRS]
            return _outproj(x, o_a, o_b, o_c, dils, w_out_l, bounded=bounded)

        safe = functools.reduce(jnp.maximum, bounds.values()) <= MAX_FIXED_SHIFT
        x = lax.cond(safe, functools.partial(mix, True), functools.partial(mix, False))

        x = _ffn(x, ffn2_norm[l], ffn2_w_gate[l].astype(BF16), ffn2_w_up[l].astype(BF16),
                 ffn2_w_down[l].astype(BF16))
    return x
```

```python
import functools
import math

import numpy as np
import jax
import jax.numpy as jnp
from jax import lax
from jax.experimental import pallas as pl
from jax.experimental.pallas import tpu as pltpu

F32 = jnp.float32
BF16 = jnp.bfloat16

GRID_W = 64
HEAD_DIM = 64
RMS_EPS = 1e-6
MASK_VALUE = -1e30
LOG2E = math.log2(math.e)

NA_HEADS = 4
NA_WIN_H = 8
NA_WIN_W = 16
DIFF_HEADS = 4
DIFF_QK_DIM = HEAD_DIM // 2
GQA_Q_HEADS = 4
GQA_KV_HEADS = 2
AXIAL_THETA = 10000.0
DIL_HEADS = 4
DIL_PAIRS = ((128, 1), (512, 4), (2048, 16))
DILATIONS = tuple(d for _, d in DIL_PAIRS if d > 1)
ROPE_THETA = 500000.0
ROPE_FRACTION = 4

LANES = 128
SEG = 256
(SEG_NA_Q, SEG_NA_K, SEG_NA_V, SEG_DF_Q, SEG_DF_K, SEG_DF_V,
 SEG_GQ_Q, SEG_GQ_K, SEG_GQ_V, SEG_DL_Q, SEG_DL_K, SEG_DL_V) = range(12)
N_SEG = 12
EXT_WIDTH = N_SEG * SEG

VMEM_LIMIT = 52 * 1024 * 1024

VT_ROWS = HEAD_DIM + 16
COL_BLOCK = 256
QK_AHEAD = 4
KV_UNROLL = 4
FLASH_COLS = 4096
MAX_FIXED_SHIFT = 0.0

NA_QROWS = 8
NA_KROWS = 16


def _cparams(sem):
    return pltpu.CompilerParams(dimension_semantics=sem, vmem_limit_bytes=VMEM_LIMIT)


def _resident(shape):
    nd = len(shape)
    return pl.BlockSpec(shape, lambda *_: (0,) * nd, pipeline_mode=pl.Buffered(1))


def _rms_rows(x, gain):
    ms = jnp.mean(x * x, axis=-1, keepdims=True)
    return x * lax.rsqrt(ms + RMS_EPS) * gain


def _group_mean_sq(y, gmat):
    sq = y * y
    hi = sq.astype(BF16)
    lo = (sq - hi.astype(F32)).astype(BF16)
    return (jnp.dot(hi, gmat, preferred_element_type=F32)
            + jnp.dot(lo, gmat, preferred_element_type=F32))


def _dot_nt(a, b):
    return lax.dot_general(a, b, (((1,), (1,)), ((), ())), preferred_element_type=F32)


def _ffn_kernel(x_ref, g_ref, wg_ref, wu_ref, wd_ref, o_ref, *, chunks):
    x = x_ref[...]
    h = _rms_rows(x, g_ref[...]).astype(BF16)
    acc = None
    for c0, c1 in chunks:
        g = jnp.dot(h, wg_ref[:, c0:c1], preferred_element_type=F32)
        u = jnp.dot(h, wu_ref[:, c0:c1], preferred_element_type=F32)
        a = (g * (1.0 / (1.0 + jnp.exp(-g))) * u).astype(BF16)
        d = jnp.dot(a, wd_ref[c0:c1, :], preferred_element_type=F32)
        acc = d if acc is None else acc + d
    o_ref[...] = x + 0.5 * acc


def _ffn(x, gain, wg, wu, wd, *, tm=512, fc=512):
    B, S, D = x.shape
    F = wg.shape[1]
    chunks = tuple((c, min(c + fc, F)) for c in range(0, F, fc))
    xspec = pl.BlockSpec((None, tm, D), lambda b, i: (b, i, 0))
    return pl.pallas_call(
        functools.partial(_ffn_kernel, chunks=chunks),
        out_shape=jax.ShapeDtypeStruct(x.shape, F32),
        grid=(B, S // tm),
        in_specs=[xspec, _resident((1, D)), _resident((D, F)), _resident((D, F)),
                  _resident((F, D))],
        out_specs=xspec,
        compiler_params=_cparams(("parallel", "parallel")),
        name="ffn",
    )(x, gain.reshape(1, D), wg, wu, wd)


_SEG_PLAN = {
    SEG_NA_Q: (0, 0, None), SEG_NA_K: (1, 0, None),
    SEG_DF_Q: (2, 1, "df"), SEG_DF_K: (3, 1, "df"),
    SEG_GQ_Q: (4, 0, "gq"), SEG_GQ_K: (5, 0, "gq"),
    SEG_DL_Q: (6, 0, "dl"), SEG_DL_K: (7, 0, "dl"),
}
_ROPE_HALF = {"df": DIFF_QK_DIM // ROPE_FRACTION // 2, "gq": HEAD_DIM // 4,
              "dl": HEAD_DIM // ROPE_FRACTION // 2}


def _inproj_kernel(x_ref, gn_ref, w_ref, gmat_ref, gains_ref,
                   cdf_ref, sdf_ref, cgq_ref, sgq_ref, cdl_ref, sdl_ref,
                   o_ref, vtdf_ref, vtgq_ref, *dl_refs_and_scratch):
    *dl_refs, ys_ref = dl_refs_and_scratch
    tm = x_ref.shape[0]
    h = _rms_rows(x_ref[...], gn_ref[...]).astype(BF16)
    lane = lax.broadcasted_iota(jnp.int32, (1, SEG), 1)
    tables = {"df": (cdf_ref, sdf_ref), "gq": (cgq_ref, sgq_ref), "dl": (cdl_ref, sdl_ref)}
    vt_plan = {SEG_DF_V: (vtdf_ref, tuple(range(DIFF_HEADS))),
               SEG_GQ_V: (vtgq_ref, tuple(2 * j for j in range(GQA_KV_HEADS)))}
    for seg in range(N_SEG):
        y = jnp.dot(h, w_ref[:, seg * SEG:(seg + 1) * SEG], preferred_element_type=F32)
        if seg in vt_plan:
            vt_ref, lane_heads = vt_plan[seg]
            yt = y.T
            ones = jnp.ones((VT_ROWS - HEAD_DIM, yt.shape[1]), BF16)
            for j, lh in enumerate(lane_heads):
                vt_ref[j * VT_ROWS:j * VT_ROWS + HEAD_DIM, :] = (
                    yt[lh * HEAD_DIM:(lh + 1) * HEAD_DIM].astype(BF16))
                vt_ref[j * VT_ROWS + HEAD_DIM:(j + 1) * VT_ROWS, :] = ones
        plan = _SEG_PLAN.get(seg)
        if plan is not None:
            gi, mi, rope = plan
            ms = _group_mean_sq(y, gmat_ref[mi])
            y = y * lax.rsqrt(ms + RMS_EPS) * gains_ref[gi:gi + 1, :]
            if rope is not None:
                half = _ROPE_HALF[rope]
                c_ref, s_ref = tables[rope]
                first = (lane & (2 * half - 1)) < half
                partner = jnp.where(first, pltpu.roll(y, SEG - half, 1), pltpu.roll(y, half, 1))
                y = y * c_ref[...] + partner * s_ref[...]
        o_ref[:, seg * SEG:(seg + 1) * SEG] = y.astype(BF16)
        if SEG_DL_Q <= seg <= SEG_DL_V:
            for hf in range(SEG // LANES):
                ys_ref[hf] = y[:, hf * LANES:(hf + 1) * LANES]
            col = (seg - SEG_DL_Q) * SEG
            for dl_ref, dil in zip(dl_refs, DILATIONS):
                for r in range(dil):
                    for hf in range(SEG // LANES):
                        dl_ref[r, :, col + hf * LANES:col + (hf + 1) * LANES] = (
                            ys_ref[hf, pl.ds(r, tm // dil, stride=dil), :].astype(BF16))


def _inproj(x, gain, w_ext, gmats, gains, tables, *, tm=512):
    B, S, D = x.shape
    tspec = pl.BlockSpec((tm, SEG), lambda i, b: (i, 0))
    vt_rows = (DIFF_HEADS * VT_ROWS, GQA_KV_HEADS * VT_ROWS)
    return pl.pallas_call(
        _inproj_kernel,
        out_shape=(jax.ShapeDtypeStruct((B, S, EXT_WIDTH), BF16),)
        + tuple(jax.ShapeDtypeStruct((B, S // tm, r, tm), BF16) for r in vt_rows)
        + tuple(jax.ShapeDtypeStruct((B, d, S // d, 3 * SEG), BF16) for d in DILATIONS),
        grid=(S // tm, B),
        in_specs=[pl.BlockSpec((None, tm, D), lambda i, b: (b, i, 0)),
                  _resident((1, D)), _resident((D, EXT_WIDTH)),
                  _resident(gmats.shape), _resident(gains.shape)] + [tspec] * 6,
        out_specs=(pl.BlockSpec((None, tm, EXT_WIDTH), lambda i, b: (b, i, 0)),)
        + tuple(pl.BlockSpec((None, None, r, tm), lambda i, b: (b, i, 0, 0)) for r in vt_rows)
        + tuple(pl.BlockSpec((None, d, tm // d, 3 * SEG), lambda i, b: (b, 0, i, 0))
                for d in DILATIONS),
        scratch_shapes=[pltpu.VMEM((SEG // LANES, tm, LANES), F32)],
        compiler_params=_cparams(("parallel", "parallel")),
        name="inproj",
    )(x, gain.reshape(1, D), w_ext, gmats, gains, *tables)


def _flash_kernel(q_ref, k_ref, vt_ref, gmat_ref, ogain_ref, lamp_ref, li_ref, shift_ref, o_ref,
                  qmt_ref, m_ref, acc_ref, sbuf_ref, *, groups, tq, tk, diff, vheads, bounded):
    gw = SEG // groups
    qt = q_ref[...].astype(F32).T
    row_g = lax.broadcasted_iota(jnp.int32, (SEG, 1), 0) >> int(math.log2(gw))
    for r in range(groups):
        qmt_ref[:, r * tq:(r + 1) * tq] = jnp.where(row_g == r, qt, 0.0).astype(BF16)
    m_ref[...] = jnp.full(m_ref.shape, MASK_VALUE, F32)
    acc_ref[...] = jnp.zeros(acc_ref.shape, F32)
    ncol = groups * tq // COL_BLOCK

    nkv = k_ref.shape[0] // tk
    ahead = sbuf_ref.shape[0]

    def k_block(kv):
        return k_ref[pl.ds(pl.multiple_of(kv * tk, tk), tk), :]

    def qk(k, c):
        return jnp.dot(k, qmt_ref[:, c * COL_BLOCK:(c + 1) * COL_BLOCK],
                       preferred_element_type=F32)

    k0 = k_block(0)
    for c in range(ahead):
        sbuf_ref[c] = qk(k0, c)

    shift = shift_ref[...]

    def body(it, carry):
        pend = [sbuf_ref[c] for c in range(ahead)]
        for u in range(KV_UNROLL):
            kv = it * KV_UNROLL + u
            k = k_block(kv)
            k_next = k_block(jnp.minimum(kv + 1, nkv - 1))
            vt = vt_ref[kv]
            for c in range(ncol):
                hv = (c * COL_BLOCK // tq) * vheads // groups
                cs = slice(c * COL_BLOCK, (c + 1) * COL_BLOCK)
                s = pend.pop(0)
                if c + ahead < ncol:
                    pend.append(qk(k, c + ahead))
                elif u + 1 < KV_UNROLL:
                    pend.append(qk(k_next, c + ahead - ncol))
                else:
                    sbuf_ref[c + ahead - ncol] = qk(k_next, c + ahead - ncol)
                vth = vt[hv * VT_ROWS:(hv + 1) * VT_ROWS]
                if bounded:
                    p = jnp.exp2((s - shift).astype(BF16))
                    acc_ref[:, cs] = acc_ref[:, cs] + jnp.dot(vth, p, preferred_element_type=F32)
                else:
                    m_prev = m_ref[:, cs]
                    m_new = jnp.maximum(m_prev, jnp.max(s, axis=0, keepdims=True))
                    alpha = jnp.exp2(m_prev - m_new)
                    p = jnp.exp2(s - m_new).astype(BF16)
                    acc_ref[:, cs] = (acc_ref[:, cs] * alpha
                                      + jnp.dot(vth, p, preferred_element_type=F32))
                    m_ref[:, cs] = m_new
        return carry

    lax.fori_loop(0, nkv // KV_UNROLL, body, 0)

    acc = acc_ref[...]
    norm = acc[0:HEAD_DIM] * (1.0 / acc[HEAD_DIM:HEAD_DIM + 1])
    if diff:
        lp = lamp_ref[...]
        lam = (jnp.exp(jnp.sum(lp[0:1] * lp[1:2], axis=1, keepdims=True))
               - jnp.exp(jnp.sum(lp[2:3] * lp[3:4], axis=1, keepdims=True)) + li_ref[...])
        pieces = [norm[:, 2 * hd * tq:(2 * hd + 1) * tq]
                  - lam * norm[:, (2 * hd + 1) * tq:(2 * hd + 2) * tq] for hd in range(groups // 2)]
    else:
        pieces = [norm[:, r * tq:(r + 1) * tq] for r in range(groups)]
    out = jnp.concatenate(pieces, axis=0).T
    if diff:
        ms = _group_mean_sq(out, gmat_ref[...])
        out = out * lax.rsqrt(ms + RMS_EPS) * ogain_ref[...] * (1.0 - li_ref[...])
    o_ref[...] = out.astype(BF16)


def _flash(qkv, vt, segs, gmat, ogain, lamp, li, shift, *, groups, diff, bounded,
           cols=FLASH_COLS):
    B, S, _ = qkv.shape
    _, nkv, vrows, tk = vt.shape
    assert nkv % KV_UNROLL == 0 and S % (cols // groups) == 0
    tq = cols // groups
    qs, ks = segs
    return pl.pallas_call(
        functools.partial(_flash_kernel, groups=groups, tq=tq, tk=tk, diff=diff,
                          vheads=vrows // VT_ROWS, bounded=bounded),
        out_shape=jax.ShapeDtypeStruct((B, S, SEG), BF16),
        grid=(B, S // tq),
        in_specs=[pl.BlockSpec((None, tq, SEG), lambda b, i: (b, i, qs)),
                  pl.BlockSpec((None, S, SEG), lambda b, i: (b, 0, ks)),
                  pl.BlockSpec((None, nkv, vrows, tk), lambda b, i: (b, 0, 0, 0)),
                  _resident(gmat.shape), _resident(ogain.shape), _resident(lamp.shape),
                  _resident(li.shape), _resident(shift.shape)],
        out_specs=pl.BlockSpec((None, tq, SEG), lambda b, i: (b, i, 0)),
        scratch_shapes=[pltpu.VMEM((SEG, cols), BF16), pltpu.VMEM((1, cols), F32),
                        pltpu.VMEM((VT_ROWS, cols), F32),
                        pltpu.VMEM((QK_AHEAD, tk, COL_BLOCK), F32)],
        compiler_params=_cparams(("parallel", "parallel")),
        name=("diff_attn" if diff else "gqa_attn") + ("" if bounded else "_runmax"),
    )(qkv, qkv, vt, gmat, ogain, lamp, li, shift)


def _pv_with_denominator(p, vw, own):
    return jnp.dot(p, jnp.where(own, vw, jnp.ones_like(vw)), preferred_element_type=F32)


def _na_kernel(q_ref, k_ref, v_ref, tab_ref, shift_ref, o_ref, *, nq, nk, bounded):
    i = pl.program_id(2)
    seq = k_ref.shape[0]
    start = jnp.clip(i * nq - (nk - nq) // 2, 0, seq - nk)
    start = pl.multiple_of(start, (nk - nq) // 2)
    kw = k_ref[pl.ds(start, nk), :]
    vw = v_ref[pl.ds(start, nk), :]
    q = q_ref[...]
    lane_h = lax.broadcasted_iota(jnp.int32, (1, 2 * HEAD_DIM), 1) >> int(math.log2(HEAD_DIM))
    outs = []
    for j in range(2):
        qm = jnp.where(lane_h == j, q, jnp.zeros_like(q))
        s = _dot_nt(qm, kw) + tab_ref[j]
        if bounded:
            p = jnp.exp2(s - shift_ref[...]).astype(BF16)
            pv = _pv_with_denominator(p, vw, lane_h == j)
            outs.append(pv * (1.0 / pltpu.roll(pv, HEAD_DIM, 1)))
        else:
            m = jnp.max(s, axis=1, keepdims=True)
            p = jnp.exp2(s - m)
            l = jnp.sum(p, axis=1, keepdims=True)
            outs.append(jnp.dot(p.astype(BF16), vw, preferred_element_type=F32) * (1.0 / l))
    o_ref[...] = jnp.where(lane_h == 0, outs[0], outs[1]).astype(BF16)


def _na_table(rel_bias, rows):
    kh = min(NA_WIN_H, rows)
    n_ro, n_co = 2 * NA_WIN_H - 1, 2 * NA_WIN_W - 1
    c = np.arange(GRID_W)[:, None]
    kc = np.arange(GRID_W)[None, :]
    cs = np.clip(c - NA_WIN_W // 2, 0, GRID_W - NA_WIN_W)
    col_valid = (kc >= cs) & (kc < cs + NA_WIN_W)
    col_sel = (kc - c + (NA_WIN_W - 1))[..., None] == np.arange(n_co)
    row_sel, row_valid = [], []
    for r0 in (0, NA_QROWS, rows - NA_QROWS):
        kr0 = int(np.clip(r0 - (NA_KROWS - NA_QROWS) // 2, 0, rows - NA_KROWS))
        r = r0 + np.arange(NA_QROWS)[:, None]
        kr = kr0 + np.arange(NA_KROWS)[None, :]
        rs = np.clip(r - kh // 2, 0, rows - kh)
        row_valid.append((kr >= rs) & (kr < rs + kh))
        row_sel.append((kr - r + (NA_WIN_H - 1))[..., None] == np.arange(n_ro))
    row_sel = jnp.asarray(np.stack(row_sel), F32)
    valid = (np.stack(row_valid)[:, None, :, None, :, None]
             & col_valid[None, None, None, :, None, :])
    hi = lax.Precision.HIGHEST
    cols = jnp.einsum("hab,cdb->hacd", rel_bias.astype(F32), jnp.asarray(col_sel, F32),
                      precision=hi)
    full = jnp.einsum("vqka,hacd->vhqckd", row_sel, cols, precision=hi)
    table = jnp.where(valid, full * LOG2E, MASK_VALUE)
    return table.reshape(3, rel_bias.shape[0], NA_QROWS * GRID_W, NA_KROWS * GRID_W)


def _na(qkv, table, shift, *, bounded):
    B, S, _ = qkv.shape
    nq, nk = NA_QROWS * GRID_W, NA_KROWS * GRID_W
    nblk = S // nq
    hp = 2 * HEAD_DIM

    def variant(i):
        return jnp.where(i == 0, 0, jnp.where(i == nblk - 1, 2, 1))

    return pl.pallas_call(
        functools.partial(_na_kernel, nq=nq, nk=nk, bounded=bounded),
        out_shape=jax.ShapeDtypeStruct((B, S, SEG), BF16),
        grid=(2, B, nblk),
        in_specs=[pl.BlockSpec((None, nq, hp), lambda p, b, i: (b, i, 2 * SEG_NA_Q + p)),
                  pl.BlockSpec((None, S, hp), lambda p, b, i: (b, 0, 2 * SEG_NA_K + p)),
                  pl.BlockSpec((None, S, hp), lambda p, b, i: (b, 0, 2 * SEG_NA_V + p)),
                  pl.BlockSpec((None, 2, nq, nk), lambda p, b, i: (variant(i), p, 0, 0)),
                  _resident(shift.shape)],
        out_specs=pl.BlockSpec((None, nq, hp), lambda p, b, i: (b, i, p)),
        compiler_params=_cparams(("parallel", "parallel", "arbitrary")),
        name="na_attn" if bounded else "na_attn_rowmax",
    )(qkv, qkv, qkv, table, shift)


def _dil_kernel(q_ref, k_ref, v_ref, shift_ref, o_ref, lse_ref, *, tq, halo, bounded):
    j = pl.program_id(2)
    length = k_ref.shape[0]
    win = tq + 2 * halo
    start = pl.multiple_of(jnp.clip(j * tq - halo, 0, length - win), halo)
    kw = k_ref[pl.ds(start, win), :]
    vw = v_ref[pl.ds(start, win), :]
    q = q_ref[...]
    qpos = j * tq + lax.broadcasted_iota(jnp.int32, (tq, win), 0)
    kpos = start + lax.broadcasted_iota(jnp.int32, (tq, win), 1)
    bias = jnp.where(jnp.abs(kpos - qpos) <= halo, 0.0, MASK_VALUE)
    lane_h = lax.broadcasted_iota(jnp.int32, (1, SEG), 1) >> int(math.log2(HEAD_DIM))
    out = jnp.zeros((tq, SEG), F32)
    lse = jnp.zeros((tq, SEG), F32)
    for hd in range(DIL_HEADS):
        qm = jnp.where(lane_h == hd, q, jnp.zeros_like(q))
        s = _dot_nt(qm, kw) + bias
        if bounded:
            p = jnp.exp2(s - shift_ref[...]).astype(BF16)
            pv = _pv_with_denominator(p, vw, lane_h == hd)
            out = jnp.where(lane_h == hd, pv, out)
            lse = jnp.where(lane_h == hd, pltpu.roll(pv, HEAD_DIM, 1), lse)
        else:
            m = jnp.max(s, axis=1, keepdims=True)
            p = jnp.exp2(s - m)
            l = jnp.sum(p, axis=1, keepdims=True)
            o = jnp.dot(p.astype(BF16), vw, preferred_element_type=F32) * (1.0 / l)
            out = jnp.where(lane_h == hd, o, out)
            lse = jnp.where(lane_h == hd, m + jnp.log(l) * LOG2E, lse)
    o_ref[...] = out
    lse_ref[...] = lse


def _dilated(src, first_seg, halo, shift, *, bounded, tq=256):
    if src.ndim == 3:
        (B, length, _), dil = src.shape, 1
        in_specs = [pl.BlockSpec((None, tq, SEG), lambda b, r, j: (b, j, first_seg)),
                    pl.BlockSpec((None, length, SEG), lambda b, r, j: (b, 0, first_seg + 1)),
                    pl.BlockSpec((None, length, SEG), lambda b, r, j: (b, 0, first_seg + 2))]
    else:
        B, dil, length, _ = src.shape
        in_specs = [pl.BlockSpec((None, None, tq, SEG), lambda b, r, j: (b, r, j, first_seg)),
                    pl.BlockSpec((None, None, length, SEG),
                                 lambda b, r, j: (b, r, 0, first_seg + 1)),
                    pl.BlockSpec((None, None, length, SEG),
                                 lambda b, r, j: (b, r, 0, first_seg + 2))]
    oshape = jax.ShapeDtypeStruct((B, dil, length, SEG), F32)
    ospec = pl.BlockSpec((None, None, tq, SEG), lambda b, r, j: (b, r, j, 0))
    return pl.pallas_call(
        functools.partial(_dil_kernel, tq=tq, halo=halo, bounded=bounded),
        out_shape=(oshape, oshape),
        grid=(B, dil, length // tq),
        in_specs=in_specs + [_resident(shift.shape)],
        out_specs=(ospec, ospec),
        compiler_params=_cparams(("parallel", "parallel", "arbitrary")),
        name=f"dil_attn_{dil}" + ("" if bounded else "_rowmax"),
    )(src, src, src, shift)


def _outproj_kernel(x_ref, oa_ref, ob_ref, oc_ref, d0_ref, d1_ref, d2_ref,
                    l0_ref, l1_ref, l2_ref, w_ref, o_ref, cat_ref, *row_scratch, bounded):
    row_scratch = list(row_scratch)

    def token_rows(ref):
        dil = ref.shape[0]
        if dil == 1:
            return ref[0]
        buf = row_scratch.pop()
        for r in range(dil):
            slab = ref[r]
            for hf in range(SEG // LANES):
                buf[hf, pl.ds(r, ref.shape[1], stride=dil), :] = slab[:, hf * LANES:(hf + 1) * LANES]
        return jnp.concatenate([buf[hf] for hf in range(SEG // LANES)], axis=1)

    l0, l1, l2 = token_rows(l0_ref), token_rows(l1_ref), token_rows(l2_ref)
    if bounded:
        od = ((token_rows(d0_ref) + token_rows(d1_ref) + token_rows(d2_ref))
              * (1.0 / (l0 + l1 + l2)))
    else:
        m = jnp.maximum(jnp.maximum(l0, l1), l2)
        e0, e1, e2 = jnp.exp2(l0 - m), jnp.exp2(l1 - m), jnp.exp2(l2 - m)
        od = ((e0 * token_rows(d0_ref) + e1 * token_rows(d1_ref) + e2 * token_rows(d2_ref))
              * (1.0 / (e0 + e1 + e2)))
    cat_ref[:, 0 * SEG:1 * SEG] = oa_ref[...]
    cat_ref[:, 1 * SEG:2 * SEG] = ob_ref[...]
    cat_ref[:, 2 * SEG:3 * SEG] = oc_ref[...]
    cat_ref[:, 3 * SEG:4 * SEG] = od.astype(BF16)
    o_ref[...] = x_ref[...] + jnp.dot(cat_ref[...], w_ref[...], preferred_element_type=F32)


def _outproj(x, oa, ob, oc, dils, w_out, *, bounded, tm=512):
    B, S, D = x.shape
    xspec = pl.BlockSpec((None, tm, D), lambda b, i: (b, i, 0))
    sspec = pl.BlockSpec((None, tm, SEG), lambda b, i: (b, i, 0))
    outs = [d[0] for d in dils]
    lses = [d[1] for d in dils]
    dspecs = [pl.BlockSpec((None, a.shape[1], tm // a.shape[1], SEG), lambda b, i: (b, 0, i, 0))
              for a in outs + lses]
    n_scratch = sum(a.shape[1] > 1 for a in outs + lses)
    return pl.pallas_call(
        functools.partial(_outproj_kernel, bounded=bounded),
        out_shape=jax.ShapeDtypeStruct(x.shape, F32),
        grid=(B, S // tm),
        in_specs=[xspec] + [sspec] * 3 + dspecs + [_resident(w_out.shape)],
        out_specs=xspec,
        scratch_shapes=[pltpu.VMEM((tm, 4 * SEG), BF16)]
        + [pltpu.VMEM((SEG // LANES, tm, LANES), F32)] * n_scratch,
        compiler_params=_cparams(("parallel", "parallel")),
        name="outproj" if bounded else "outproj_lse",
    )(x, oa, ob, oc, *outs, *lses, w_out)


def _rope_cs(pos, dim, theta):
    inv_freq = 1.0 / (theta ** (jnp.arange(0, dim, 2, dtype=F32) / dim))
    ang = pos.astype(F32)[:, None] * inv_freq[None, :]
    return jnp.cos(ang), jnp.sin(ang)


def _rope_tables(S):
    pos = jnp.arange(S, dtype=jnp.int32)

    def expand(parts_c, parts_s, reps):
        return (jnp.tile(jnp.concatenate(parts_c, axis=1), (1, reps)),
                jnp.tile(jnp.concatenate(parts_s, axis=1), (1, reps)))

    c, s = _rope_cs(pos, DIFF_QK_DIM // ROPE_FRACTION, ROPE_THETA)
    rest = DIFF_QK_DIM - 2 * c.shape[1]
    df = expand([c, c, jnp.ones((S, rest), F32)], [-s, s, jnp.zeros((S, rest), F32)],
                SEG // DIFF_QK_DIM)
    c, s = _rope_cs(pos, HEAD_DIM // ROPE_FRACTION, ROPE_THETA)
    rest = HEAD_DIM - 2 * c.shape[1]
    dl = expand([c, c, jnp.ones((S, rest), F32)], [-s, s, jnp.zeros((S, rest), F32)],
                SEG // HEAD_DIM)
    cr, sr = _rope_cs(pos // GRID_W, HEAD_DIM // 2, AXIAL_THETA)
    cc, sc = _rope_cs(pos % GRID_W, HEAD_DIM // 2, AXIAL_THETA)
    gq = expand([cr, cr, cc, cc], [-sr, sr, -sc, sc], SEG // HEAD_DIM)
    return df + gq + dl


def _group_matrices():
    lane = np.arange(SEG)
    mats = [(lane[:, None] // g == lane[None, :] // g).astype(np.float32) / g
            for g in (HEAD_DIM, DIFF_QK_DIM)]
    return jnp.asarray(np.stack(mats), BF16)


def _extend_w_in(w_in):
    kv0 = 7 * SEG
    hd = HEAD_DIM
    cols = np.concatenate([
        np.arange(0, kv0),
        kv0 + np.array([0, 0, 1, 1]).repeat(hd) * hd + np.tile(np.arange(hd), 4),
        kv0 + 2 * hd + np.array([0, 0, 1, 1]).repeat(hd) * hd + np.tile(np.arange(hd), 4),
        np.arange(kv0 + 4 * hd, w_in.shape[1]),
    ])
    return w_in[:, cols].astype(BF16)


def _gain_rows(na_q, na_k, df_q, df_k, gq_q, gq_k, dl_q, dl_k):
    def row(g, scale):
        return jnp.tile(g.astype(F32), SEG // g.shape[0]) * scale
    sc64 = HEAD_DIM ** -0.5 * LOG2E
    sc32 = DIFF_QK_DIM ** -0.5 * LOG2E
    return jnp.stack([row(na_q, sc64), row(na_k, 1.0), row(df_q, sc32), row(df_k, 1.0),
                      row(gq_q, sc64), row(gq_k, 1.0), row(dl_q, sc64), row(dl_k, 1.0)])


def kernel(x, ffn1_norm, ffn1_w_gate, ffn1_w_up, ffn1_w_down, mix_norm, w_in, w_out, na_q_norm, na_k_norm, na_rel_bias, diff_q_norm, diff_k_norm, diff_lambda_q1, diff_lambda_k1, diff_lambda_q2, diff_lambda_k2, diff_out_norm, gqa_q_norm, gqa_k_norm, dil_q_norm, dil_k_norm, ffn2_norm, ffn2_w_gate, ffn2_w_up, ffn2_w_down):
    B, S, D = x.shape
    depth = w_in.shape[0]
    rows = S // GRID_W
    tables = _rope_tables(S)
    gmats = _group_matrices()
    ones_row = jnp.ones((1, SEG), F32)
    zero_lamp = jnp.zeros((4, DIFF_QK_DIM), F32)
    zero_li = jnp.zeros((1, 1), F32)

    for l in range(depth):
        x = _ffn(x, ffn1_norm[l], ffn1_w_gate[l].astype(BF16), ffn1_w_up[l].astype(BF16),
                 ffn1_w_down[l].astype(BF16))

        gains = _gain_rows(na_q_norm[l], na_k_norm[l], diff_q_norm[l], diff_k_norm[l],
                           gqa_q_norm[l], gqa_k_norm[l], dil_q_norm[l], dil_k_norm[l])
        qkv, vt_df, vt_gq, *dl_classes = _inproj(x, mix_norm[l], _extend_w_in(w_in[l]), gmats,
                                                 gains, tables)

        na_table = _na_table(na_rel_bias[l], rows)
        lamp = jnp.stack([diff_lambda_q1[l], diff_lambda_k1[l], diff_lambda_q2[l],
                          diff_lambda_k2[l]]).astype(F32)
        li = jnp.full((1, 1), 0.8 - 0.6 * math.exp(-0.3 * l), F32)
        ogain = jnp.tile(diff_out_norm[l].astype(F32), SEG // HEAD_DIM).reshape(1, SEG)
        dl_src = {1: (qkv, SEG_DL_Q)}
        dl_src.update({d: (a, 0) for d, a in zip(DILATIONS, dl_classes)})
        w_out_l = w_out[l].astype(BF16)

        def score_bound(qi, width):
            return width * jnp.max(jnp.abs(gains[qi])) * jnp.max(jnp.abs(gains[qi + 1]))
        bounds = {"na": score_bound(0, HEAD_DIM) + LOG2E * jnp.max(jnp.abs(na_rel_bias[l])),
                  "df": score_bound(2, DIFF_QK_DIM), "gq": score_bound(4, HEAD_DIM),
                  "dl": score_bound(6, HEAD_DIM)}
        shifts = {k: v.reshape(1, 1).astype(F32) for k, v in bounds.items()}

        def mix(bounded, x=x, qkv=qkv, vt_df=vt_df, vt_gq=vt_gq, dl_src=dl_src, shifts=shifts,
                na_table=na_table, ogain=ogain, lamp=lamp, li=li, w_out_l=w_out_l):
            o_a = _na(qkv, na_table, shifts["na"], bounded=bounded)
            o_b = _flash(qkv, vt_df, (SEG_DF_Q, SEG_DF_K), gmats[0], ogain, lamp, li,
                         shifts["df"], groups=2 * DIFF_HEADS, diff=True, bounded=bounded)
            o_c = _flash(qkv, vt_gq, (SEG_GQ_Q, SEG_GQ_K), gmats[0], ones_row, zero_lamp,
                         zero_li, shifts["gq"], groups=GQA_Q_HEADS, diff=False, bounded=bounded)
            dils = [_dilated(*dl_src[dil], (window // 2) // dil, shifts["dl"], bounded=bounded)
                    for window, dil in DIL_PAIRS]
            return _outproj(x, o_a, o_b, o_c, dils, w_out_l, bounded=bounded)

        safe = functools.reduce(jnp.maximum, bounds.values()) <= MAX_FIXED_SHIFT
        x = lax.cond(safe, functools.partial(mix, True), functools.partial(mix, False))

        x = _ffn(x, ffn2_norm[l], ffn2_w_gate[l].astype(BF16), ffn2_w_up[l].astype(BF16),
                 ffn2_w_down[l].astype(BF16))
    return x
```

```python
import functools
import math

import numpy as np
import jax
import jax.numpy as jnp
from jax import lax
from jax.experimental import pallas as pl
from jax.experimental.pallas import tpu as pltpu

F32 = jnp.float32
BF16 = jnp.bfloat16

GRID_W = 64
HEAD_DIM = 64
RMS_EPS = 1e-6
MASK_VALUE = -1e30
LOG2E = math.log2(math.e)

NA_HEADS = 4
NA_WIN_H = 8
NA_WIN_W = 16
DIFF_HEADS = 4
DIFF_QK_DIM = HEAD_DIM // 2
GQA_Q_HEADS = 4
GQA_KV_HEADS = 2
AXIAL_THETA = 10000.0
DIL_HEADS = 4
DIL_PAIRS = ((128, 1), (512, 4), (2048, 16))
DILATIONS = tuple(d for _, d in DIL_PAIRS if d > 1)
ROPE_THETA = 500000.0
ROPE_FRACTION = 4

LANES = 128
SEG = 256
(SEG_NA_Q, SEG_NA_K, SEG_NA_V, SEG_DF_Q, SEG_DF_K, SEG_DF_V,
 SEG_GQ_Q, SEG_GQ_K, SEG_GQ_V, SEG_DL_Q, SEG_DL_K, SEG_DL_V) = range(12)
N_SEG = 12
EXT_WIDTH = N_SEG * SEG

VMEM_LIMIT = 52 * 1024 * 1024

VT_ROWS = HEAD_DIM + 16
COL_BLOCK = 256
QK_AHEAD = 4
KV_UNROLL = 4
FLASH_COLS = 4096
MAX_FIXED_SHIFT = 40.0

NA_QROWS = 8
NA_KROWS = 16


def _cparams(sem):
    return pltpu.CompilerParams(dimension_semantics=sem, vmem_limit_bytes=VMEM_LIMIT)


def _resident(shape):
    nd = len(shape)
    return pl.BlockSpec(shape, lambda *_: (0,) * nd, pipeline_mode=pl.Buffered(1))


def _rms_rows(x, gain):
    ms = jnp.mean(x * x, axis=-1, keepdims=True)
    return x * lax.rsqrt(ms + RMS_EPS) * gain


def _group_mean_sq(y, gmat):
    sq = y * y
    hi = sq.astype(BF16)
    lo = (sq - hi.astype(F32)).astype(BF16)
    return (jnp.dot(hi, gmat, preferred_element_type=F32)
            + jnp.dot(lo, gmat, preferred_element_type=F32))


def _dot_nt(a, b):
    return lax.dot_general(a, b, (((1,), (1,)), ((), ())), preferred_element_type=F32)


def _ffn_tile(x, g_ref, wg_ref, wu_ref, wd_ref, chunks):
    h = _rms_rows(x, g_ref[...]).astype(BF16)
    acc = None
    for c0, c1 in chunks:
        g = jnp.dot(h, wg_ref[:, c0:c1], preferred_element_type=F32)
        u = jnp.dot(h, wu_ref[:, c0:c1], preferred_element_type=F32)
        a = (g * (1.0 / (1.0 + jnp.exp(-g))) * u).astype(BF16)
        d = jnp.dot(a, wd_ref[c0:c1, :], preferred_element_type=F32)
        acc = d if acc is None else acc + d
    return x + 0.5 * acc


def _ffn_kernel(x_ref, g_ref, wg_ref, wu_ref, wd_ref, o_ref, *, chunks):
    o_ref[...] = _ffn_tile(x_ref[...], g_ref, wg_ref, wu_ref, wd_ref, chunks)


def _ffn_chunks(F, fc=512):
    return tuple((c, min(c + fc, F)) for c in range(0, F, fc))


def _ffn(x, gain, wg, wu, wd, *, tm=512):
    B, S, D = x.shape
    F = wg.shape[1]
    chunks = _ffn_chunks(F)
    xspec = pl.BlockSpec((None, tm, D), lambda b, i: (b, i, 0))
    return pl.pallas_call(
        functools.partial(_ffn_kernel, chunks=chunks),
        out_shape=jax.ShapeDtypeStruct(x.shape, F32),
        grid=(B, S // tm),
        in_specs=[xspec, _resident((1, D)), _resident((D, F)), _resident((D, F)),
                  _resident((F, D))],
        out_specs=xspec,
        compiler_params=_cparams(("parallel", "parallel")),
        name="ffn",
    )(x, gain.reshape(1, D), wg, wu, wd)


_SEG_PLAN = {
    SEG_NA_Q: (0, 0, None), SEG_NA_K: (1, 0, None),
    SEG_DF_Q: (2, 1, "df"), SEG_DF_K: (3, 1, "df"),
    SEG_GQ_Q: (4, 0, "gq"), SEG_GQ_K: (5, 0, "gq"),
    SEG_DL_Q: (6, 0, "dl"), SEG_DL_K: (7, 0, "dl"),
}
_ROPE_HALF = {"df": DIFF_QK_DIM // ROPE_FRACTION // 2, "gq": HEAD_DIM // 4,
              "dl": HEAD_DIM // ROPE_FRACTION // 2}


def _inproj_kernel(x_ref, gn_ref, w_ref, gmat_ref, gains_ref,
                   cdf_ref, sdf_ref, cgq_ref, sgq_ref, cdl_ref, sdl_ref,
                   o_ref, vtdf_ref, vtgq_ref, *dl_refs_and_scratch):
    *dl_refs, ys_ref = dl_refs_and_scratch
    tm = x_ref.shape[0]
    h = _rms_rows(x_ref[...], gn_ref[...]).astype(BF16)
    lane = lax.broadcasted_iota(jnp.int32, (1, SEG), 1)
    tables = {"df": (cdf_ref, sdf_ref), "gq": (cgq_ref, sgq_ref), "dl": (cdl_ref, sdl_ref)}
    vt_plan = {SEG_DF_V: (vtdf_ref, tuple(range(DIFF_HEADS))),
               SEG_GQ_V: (vtgq_ref, tuple(2 * j for j in range(GQA_KV_HEADS)))}
    for seg in range(N_SEG):
        y = jnp.dot(h, w_ref[:, seg * SEG:(seg + 1) * SEG], preferred_element_type=F32)
        if seg in vt_plan:
            vt_ref, lane_heads = vt_plan[seg]
            yt = y.T
            ones = jnp.ones((VT_ROWS - HEAD_DIM, yt.shape[1]), BF16)
            for j, lh in enumerate(lane_heads):
                vt_ref[j * VT_ROWS:j * VT_ROWS + HEAD_DIM, :] = (
                    yt[lh * HEAD_DIM:(lh + 1) * HEAD_DIM].astype(BF16))
                vt_ref[j * VT_ROWS + HEAD_DIM:(j + 1) * VT_ROWS, :] = ones
        plan = _SEG_PLAN.get(seg)
        if plan is not None:
            gi, mi, rope = plan
            ms = _group_mean_sq(y, gmat_ref[mi])
            y = y * lax.rsqrt(ms + RMS_EPS) * gains_ref[gi:gi + 1, :]
            if rope is not None:
                half = _ROPE_HALF[rope]
                c_ref, s_ref = tables[rope]
                first = (lane & (2 * half - 1)) < half
                partner = jnp.where(first, pltpu.roll(y, SEG - half, 1), pltpu.roll(y, half, 1))
                y = y * c_ref[...] + partner * s_ref[...]
        o_ref[:, seg * SEG:(seg + 1) * SEG] = y.astype(BF16)
        if SEG_DL_Q <= seg <= SEG_DL_V:
            for hf in range(SEG // LANES):
                ys_ref[hf] = y[:, hf * LANES:(hf + 1) * LANES]
            col = (seg - SEG_DL_Q) * SEG
            for dl_ref, dil in zip(dl_refs, DILATIONS):
                for r in range(dil):
                    for hf in range(SEG // LANES):
                        dl_ref[r, :, col + hf * LANES:col + (hf + 1) * LANES] = (
                            ys_ref[hf, pl.ds(r, tm // dil, stride=dil), :].astype(BF16))


def _inproj(x, gain, w_ext, gmats, gains, tables, *, tm=512):
    B, S, D = x.shape
    tspec = pl.BlockSpec((tm, SEG), lambda i, b: (i, 0))
    vt_rows = (DIFF_HEADS * VT_ROWS, GQA_KV_HEADS * VT_ROWS)
    return pl.pallas_call(
        _inproj_kernel,
        out_shape=(jax.ShapeDtypeStruct((B, S, EXT_WIDTH), BF16),)
        + tuple(jax.ShapeDtypeStruct((B, S // tm, r, tm), BF16) for r in vt_rows)
        + tuple(jax.ShapeDtypeStruct((B, d, S // d, 3 * SEG), BF16) for d in DILATIONS),
        grid=(S // tm, B),
        in_specs=[pl.BlockSpec((None, tm, D), lambda i, b: (b, i, 0)),
                  _resident((1, D)), _resident((D, EXT_WIDTH)),
                  _resident(gmats.shape), _resident(gains.shape)] + [tspec] * 6,
        out_specs=(pl.BlockSpec((None, tm, EXT_WIDTH), lambda i, b: (b, i, 0)),)
        + tuple(pl.BlockSpec((None, None, r, tm), lambda i, b: (b, i, 0, 0)) for r in vt_rows)
        + tuple(pl.BlockSpec((None, d, tm // d, 3 * SEG), lambda i, b: (b, 0, i, 0))
                for d in DILATIONS),
        scratch_shapes=[pltpu.VMEM((SEG // LANES, tm, LANES), F32)],
        compiler_params=_cparams(("parallel", "parallel")),
        name="inproj",
    )(x, gain.reshape(1, D), w_ext, gmats, gains, *tables)


def _flash_kernel(q_ref, k_ref, vt_ref, gmat_ref, ogain_ref, lamp_ref, li_ref, shift_ref, o_ref,
                  qmt_ref, m_ref, acc_ref, sbuf_ref, *, groups, tq, tk, diff, vheads, bounded):
    gw = SEG // groups
    qt = q_ref[...].astype(F32).T
    row_g = lax.broadcasted_iota(jnp.int32, (SEG, 1), 0) >> int(math.log2(gw))
    for r in range(groups):
        qmt_ref[:, r * tq:(r + 1) * tq] = jnp.where(row_g == r, qt, 0.0).astype(BF16)
    m_ref[...] = jnp.full(m_ref.shape, MASK_VALUE, F32)
    acc_ref[...] = jnp.zeros(acc_ref.shape, F32)
    ncol = groups * tq // COL_BLOCK

    nkv = k_ref.shape[0] // tk
    ahead = sbuf_ref.shape[0]

    def k_block(kv):
        return k_ref[pl.ds(pl.multiple_of(kv * tk, tk), tk), :]

    def qk(k, c):
        return jnp.dot(k, qmt_ref[:, c * COL_BLOCK:(c + 1) * COL_BLOCK],
                       preferred_element_type=F32)

    k0 = k_block(0)
    for c in range(ahead):
        sbuf_ref[c] = qk(k0, c)

    shift = shift_ref[...]

    def body(it, carry):
        pend = [sbuf_ref[c] for c in range(ahead)]
        for u in range(KV_UNROLL):
            kv = it * KV_UNROLL + u
            k = k_block(kv)
            k_next = k_block(jnp.minimum(kv + 1, nkv - 1))
            vt = vt_ref[kv]
            for c in range(ncol):
                hv = (c * COL_BLOCK // tq) * vheads // groups
                cs = slice(c * COL_BLOCK, (c + 1) * COL_BLOCK)
                s = pend.pop(0)
                if c + ahead < ncol:
                    pend.append(qk(k, c + ahead))
                elif u + 1 < KV_UNROLL:
                    pend.append(qk(k_next, c + ahead - ncol))
                else:
                    sbuf_ref[c + ahead - ncol] = qk(k_next, c + ahead - ncol)
                vth = vt[hv * VT_ROWS:(hv + 1) * VT_ROWS]
                if bounded:
                    p = jnp.exp2((s - shift).astype(BF16))
                    acc_ref[:, cs] = acc_ref[:, cs] + jnp.dot(vth, p, preferred_element_type=F32)
                else:
                    m_prev = m_ref[:, cs]
                    m_new = jnp.maximum(m_prev, jnp.max(s, axis=0, keepdims=True))
                    alpha = jnp.exp2(m_prev - m_new)
                    p = jnp.exp2(s - m_new).astype(BF16)
                    acc_ref[:, cs] = (acc_ref[:, cs] * alpha
                                      + jnp.dot(vth, p, preferred_element_type=F32))
                    m_ref[:, cs] = m_new
        return carry

    lax.fori_loop(0, nkv // KV_UNROLL, body, 0)

    acc = acc_ref[...]
    norm = acc[0:HEAD_DIM] * (1.0 / acc[HEAD_DIM:HEAD_DIM + 1])
    if diff:
        lp = lamp_ref[...]
        lam = (jnp.exp(jnp.sum(lp[0:1] * lp[1:2], axis=1, keepdims=True))
               - jnp.exp(jnp.sum(lp[2:3] * lp[3:4], axis=1, keepdims=True)) + li_ref[...])
        pieces = [norm[:, 2 * hd * tq:(2 * hd + 1) * tq]
                  - lam * norm[:, (2 * hd + 1) * tq:(2 * hd + 2) * tq] for hd in range(groups // 2)]
    else:
        pieces = [norm[:, r * tq:(r + 1) * tq] for r in range(groups)]
    out = jnp.concatenate(pieces, axis=0).T
    if diff:
        ms = _group_mean_sq(out, gmat_ref[...])
        out = out * lax.rsqrt(ms + RMS_EPS) * ogain_ref[...] * (1.0 - li_ref[...])
    o_ref[...] = out.astype(BF16)


def _flash(qkv, vt, segs, gmat, ogain, lamp, li, shift, *, groups, diff, bounded,
           cols=FLASH_COLS):
    B, S, _ = qkv.shape
    _, nkv, vrows, tk = vt.shape
    assert nkv % KV_UNROLL == 0 and S % (cols // groups) == 0
    tq = cols // groups
    qs, ks = segs
    return pl.pallas_call(
        functools.partial(_flash_kernel, groups=groups, tq=tq, tk=tk, diff=diff,
                          vheads=vrows // VT_ROWS, bounded=bounded),
        out_shape=jax.ShapeDtypeStruct((B, S, SEG), BF16),
        grid=(B, S // tq),
        in_specs=[pl.BlockSpec((None, tq, SEG), lambda b, i: (b, i, qs)),
                  pl.BlockSpec((None, S, SEG), lambda b, i: (b, 0, ks)),
                  pl.BlockSpec((None, nkv, vrows, tk), lambda b, i: (b, 0, 0, 0)),
                  _resident(gmat.shape), _resident(ogain.shape), _resident(lamp.shape),
                  _resident(li.shape), _resident(shift.shape)],
        out_specs=pl.BlockSpec((None, tq, SEG), lambda b, i: (b, i, 0)),
        scratch_shapes=[pltpu.VMEM((SEG, cols), BF16), pltpu.VMEM((1, cols), F32),
                        pltpu.VMEM((VT_ROWS, cols), F32),
                        pltpu.VMEM((QK_AHEAD, tk, COL_BLOCK), F32)],
        compiler_params=_cparams(("parallel", "parallel")),
        name=("diff_attn" if diff else "gqa_attn") + ("" if bounded else "_runmax"),
    )(qkv, qkv, vt, gmat, ogain, lamp, li, shift)


def _pv_with_denominator(p, vw, own):
    return jnp.dot(p, jnp.where(own, vw, jnp.ones_like(vw)), preferred_element_type=F32)


def _na_kernel(q_ref, k_ref, v_ref, tab_ref, shift_ref, o_ref, *, nq, nk, bounded):
    i = pl.program_id(2)
    seq = k_ref.shape[0]
    start = jnp.clip(i * nq - (nk - nq) // 2, 0, seq - nk)
    start = pl.multiple_of(start, (nk - nq) // 2)
    kw = k_ref[pl.ds(start, nk), :]
    vw = v_ref[pl.ds(start, nk), :]
    q = q_ref[...]
    lane_h = lax.broadcasted_iota(jnp.int32, (1, 2 * HEAD_DIM), 1) >> int(math.log2(HEAD_DIM))
    outs = []
    for j in range(2):
        qm = jnp.where(lane_h == j, q, jnp.zeros_like(q))
        s = _dot_nt(qm, kw) + tab_ref[j]
        if bounded:
            p = jnp.exp2(s - shift_ref[...]).astype(BF16)
            pv = _pv_with_denominator(p, vw, lane_h == j)
            outs.append(pv * (1.0 / pltpu.roll(pv, HEAD_DIM, 1)))
        else:
            m = jnp.max(s, axis=1, keepdims=True)
            p = jnp.exp2(s - m)
            l = jnp.sum(p, axis=1, keepdims=True)
            outs.append(jnp.dot(p.astype(BF16), vw, preferred_element_type=F32) * (1.0 / l))
    o_ref[...] = jnp.where(lane_h == 0, outs[0], outs[1]).astype(BF16)


def _na_table(rel_bias, rows):
    kh = min(NA_WIN_H, rows)
    n_ro, n_co = 2 * NA_WIN_H - 1, 2 * NA_WIN_W - 1
    c = np.arange(GRID_W)[:, None]
    kc = np.arange(GRID_W)[None, :]
    cs = np.clip(c - NA_WIN_W // 2, 0, GRID_W - NA_WIN_W)
    col_valid = (kc >= cs) & (kc < cs + NA_WIN_W)
    col_sel = (kc - c + (NA_WIN_W - 1))[..., None] == np.arange(n_co)
    row_sel, row_valid = [], []
    for r0 in (0, NA_QROWS, rows - NA_QROWS):
        kr0 = int(np.clip(r0 - (NA_KROWS - NA_QROWS) // 2, 0, rows - NA_KROWS))
        r = r0 + np.arange(NA_QROWS)[:, None]
        kr = kr0 + np.arange(NA_KROWS)[None, :]
        rs = np.clip(r - kh // 2, 0, rows - kh)
        row_valid.append((kr >= rs) & (kr < rs + kh))
        row_sel.append((kr - r + (NA_WIN_H - 1))[..., None] == np.arange(n_ro))
    row_sel = jnp.asarray(np.stack(row_sel), F32)
    valid = (np.stack(row_valid)[:, None, :, None, :, None]
             & col_valid[None, None, None, :, None, :])
    hi = lax.Precision.HIGHEST
    cols = jnp.einsum("hab,cdb->hacd", rel_bias.astype(F32), jnp.asarray(col_sel, F32),
                      precision=hi)
    full = jnp.einsum("vqka,hacd->vhqckd", row_sel, cols, precision=hi)
    table = jnp.where(valid, full * LOG2E, MASK_VALUE)
    return table.reshape(3, rel_bias.shape[0], NA_QROWS * GRID_W, NA_KROWS * GRID_W)


def _na(qkv, table, shift, *, bounded):
    B, S, _ = qkv.shape
    nq, nk = NA_QROWS * GRID_W, NA_KROWS * GRID_W
    nblk = S // nq
    hp = 2 * HEAD_DIM

    def variant(i):
        return jnp.where(i == 0, 0, jnp.where(i == nblk - 1, 2, 1))

    return pl.pallas_call(
        functools.partial(_na_kernel, nq=nq, nk=nk, bounded=bounded),
        out_shape=jax.ShapeDtypeStruct((B, S, SEG), BF16),
        grid=(2, B, nblk),
        in_specs=[pl.BlockSpec((None, nq, hp), lambda p, b, i: (b, i, 2 * SEG_NA_Q + p)),
                  pl.BlockSpec((None, S, hp), lambda p, b, i: (b, 0, 2 * SEG_NA_K + p)),
                  pl.BlockSpec((None, S, hp), lambda p, b, i: (b, 0, 2 * SEG_NA_V + p)),
                  pl.BlockSpec((None, 2, nq, nk), lambda p, b, i: (variant(i), p, 0, 0)),
                  _resident(shift.shape)],
        out_specs=pl.BlockSpec((None, nq, hp), lambda p, b, i: (b, i, p)),
        compiler_params=_cparams(("parallel", "parallel", "arbitrary")),
        name="na_attn" if bounded else "na_attn_rowmax",
    )(qkv, qkv, qkv, table, shift)


def _dil_kernel(q_ref, k_ref, v_ref, shift_ref, o_ref, lse_ref, *, tq, halo, bounded):
    j = pl.program_id(2)
    length = k_ref.shape[0]
    win = tq + 2 * halo
    start = pl.multiple_of(jnp.clip(j * tq - halo, 0, length - win), halo)
    kw = k_ref[pl.ds(start, win), :]
    vw = v_ref[pl.ds(start, win), :]
    q = q_ref[...]
    qpos = j * tq + lax.broadcasted_iota(jnp.int32, (tq, win), 0)
    kpos = start + lax.broadcasted_iota(jnp.int32, (tq, win), 1)
    bias = jnp.where(jnp.abs(kpos - qpos) <= halo, 0.0, MASK_VALUE)
    lane_h = lax.broadcasted_iota(jnp.int32, (1, SEG), 1) >> int(math.log2(HEAD_DIM))
    out = jnp.zeros((tq, SEG), F32)
    lse = jnp.zeros((tq, SEG), F32)
    for hd in range(DIL_HEADS):
        qm = jnp.where(lane_h == hd, q, jnp.zeros_like(q))
        s = _dot_nt(qm, kw) + bias
        if bounded:
            p = jnp.exp2(s - shift_ref[...]).astype(BF16)
            pv = _pv_with_denominator(p, vw, lane_h == hd)
            out = jnp.where(lane_h == hd, pv, out)
            lse = jnp.where(lane_h == hd, pltpu.roll(pv, HEAD_DIM, 1), lse)
        else:
            m = jnp.max(s, axis=1, keepdims=True)
            p = jnp.exp2(s - m)
            l = jnp.sum(p, axis=1, keepdims=True)
            o = jnp.dot(p.astype(BF16), vw, preferred_element_type=F32) * (1.0 / l)
            out = jnp.where(lane_h == hd, o, out)
            lse = jnp.where(lane_h == hd, m + jnp.log(l) * LOG2E, lse)
    o_ref[...] = out
    lse_ref[...] = lse


def _dilated(src, first_seg, halo, shift, *, bounded, tq=256):
    if src.ndim == 3:
        (B, length, _), dil = src.shape, 1
        in_specs = [pl.BlockSpec((None, tq, SEG), lambda b, r, j: (b, j, first_seg)),
                    pl.BlockSpec((None, length, SEG), lambda b, r, j: (b, 0, first_seg + 1)),
                    pl.BlockSpec((None, length, SEG), lambda b, r, j: (b, 0, first_seg + 2))]
    else:
        B, dil, length, _ = src.shape
        in_specs = [pl.BlockSpec((None, None, tq, SEG), lambda b, r, j: (b, r, j, first_seg)),
                    pl.BlockSpec((None, None, length, SEG),
                                 lambda b, r, j: (b, r, 0, first_seg + 1)),
                    pl.BlockSpec((None, None, length, SEG),
                                 lambda b, r, j: (b, r, 0, first_seg + 2))]
    oshape = jax.ShapeDtypeStruct((B, dil, length, SEG), F32)
    ospec = pl.BlockSpec((None, None, tq, SEG), lambda b, r, j: (b, r, j, 0))
    return pl.pallas_call(
        functools.partial(_dil_kernel, tq=tq, halo=halo, bounded=bounded),
        out_shape=(oshape, oshape),
        grid=(B, dil, length // tq),
        in_specs=in_specs + [_resident(shift.shape)],
        out_specs=(ospec, ospec),
        compiler_params=_cparams(("parallel", "parallel", "arbitrary")),
        name=f"dil_attn_{dil}" + ("" if bounded else "_rowmax"),
    )(src, src, src, shift)


def _outproj_kernel(x_ref, oa_ref, ob_ref, oc_ref, d0_ref, d1_ref, d2_ref,
                    l0_ref, l1_ref, l2_ref, w_ref, fg_ref, fwg_ref, fwu_ref, fwd_ref,
                    o_ref, cat_ref, *row_scratch, bounded, chunks):
    row_scratch = list(row_scratch)

    def token_rows(ref):
        dil = ref.shape[0]
        if dil == 1:
            return ref[0]
        buf = row_scratch.pop()
        for r in range(dil):
            slab = ref[r]
            for hf in range(SEG // LANES):
                buf[hf, pl.ds(r, ref.shape[1], stride=dil), :] = slab[:, hf * LANES:(hf + 1) * LANES]
        return jnp.concatenate([buf[hf] for hf in range(SEG // LANES)], axis=1)

    l0, l1, l2 = token_rows(l0_ref), token_rows(l1_ref), token_rows(l2_ref)
    if bounded:
        od = ((token_rows(d0_ref) + token_rows(d1_ref) + token_rows(d2_ref))
              * (1.0 / (l0 + l1 + l2)))
    else:
        m = jnp.maximum(jnp.maximum(l0, l1), l2)
        e0, e1, e2 = jnp.exp2(l0 - m), jnp.exp2(l1 - m), jnp.exp2(l2 - m)
        od = ((e0 * token_rows(d0_ref) + e1 * token_rows(d1_ref) + e2 * token_rows(d2_ref))
              * (1.0 / (e0 + e1 + e2)))
    cat_ref[:, 0 * SEG:1 * SEG] = oa_ref[...]
    cat_ref[:, 1 * SEG:2 * SEG] = ob_ref[...]
    cat_ref[:, 2 * SEG:3 * SEG] = oc_ref[...]
    cat_ref[:, 3 * SEG:4 * SEG] = od.astype(BF16)
    mixed = x_ref[...] + jnp.dot(cat_ref[...], w_ref[...], preferred_element_type=F32)
    o_ref[...] = _ffn_tile(mixed, fg_ref, fwg_ref, fwu_ref, fwd_ref, chunks)


def _outproj_ffn(x, oa, ob, oc, dils, w_out, fgain, fwg, fwu, fwd, *, bounded, tm=512):
    B, S, D = x.shape
    F = fwg.shape[1]
    xspec = pl.BlockSpec((None, tm, D), lambda b, i: (b, i, 0))
    sspec = pl.BlockSpec((None, tm, SEG), lambda b, i: (b, i, 0))
    outs = [d[0] for d in dils]
    lses = [d[1] for d in dils]
    dspecs = [pl.BlockSpec((None, a.shape[1], tm // a.shape[1], SEG), lambda b, i: (b, 0, i, 0))
              for a in outs + lses]
    n_scratch = sum(a.shape[1] > 1 for a in outs + lses)
    return pl.pallas_call(
        functools.partial(_outproj_kernel, bounded=bounded, chunks=_ffn_chunks(F)),
        out_shape=jax.ShapeDtypeStruct(x.shape, F32),
        grid=(B, S // tm),
        in_specs=[xspec] + [sspec] * 3 + dspecs
        + [_resident(w_out.shape), _resident((1, D)), _resident((D, F)), _resident((D, F)),
           _resident((F, D))],
        out_specs=xspec,
        scratch_shapes=[pltpu.VMEM((tm, 4 * SEG), BF16)]
        + [pltpu.VMEM((SEG // LANES, tm, LANES), F32)] * n_scratch,
        compiler_params=_cparams(("parallel", "parallel")),
        name="outproj_ffn" if bounded else "outproj_lse_ffn",
    )(x, oa, ob, oc, *outs, *lses, w_out, fgain.reshape(1, D), fwg, fwu, fwd)


def _rope_cs(pos, dim, theta):
    inv_freq = 1.0 / (theta ** (jnp.arange(0, dim, 2, dtype=F32) / dim))
    ang = pos.astype(F32)[:, None] * inv_freq[None, :]
    return jnp.cos(ang), jnp.sin(ang)


def _rope_tables(S):
    pos = jnp.arange(S, dtype=jnp.int32)

    def expand(parts_c, parts_s, reps):
        return (jnp.tile(jnp.concatenate(parts_c, axis=1), (1, reps)),
                jnp.tile(jnp.concatenate(parts_s, axis=1), (1, reps)))

    c, s = _rope_cs(pos, DIFF_QK_DIM // ROPE_FRACTION, ROPE_THETA)
    rest = DIFF_QK_DIM - 2 * c.shape[1]
    df = expand([c, c, jnp.ones((S, rest), F32)], [-s, s, jnp.zeros((S, rest), F32)],
                SEG // DIFF_QK_DIM)
    c, s = _rope_cs(pos, HEAD_DIM // ROPE_FRACTION, ROPE_THETA)
    rest = HEAD_DIM - 2 * c.shape[1]
    dl = expand([c, c, jnp.ones((S, rest), F32)], [-s, s, jnp.zeros((S, rest), F32)],
                SEG // HEAD_DIM)
    cr, sr = _rope_cs(pos // GRID_W, HEAD_DIM // 2, AXIAL_THETA)
    cc, sc = _rope_cs(pos % GRID_W, HEAD_DIM // 2, AXIAL_THETA)
    gq = expand([cr, cr, cc, cc], [-sr, sr, -sc, sc], SEG // HEAD_DIM)
    return df + gq + dl


def _group_matrices():
    lane = np.arange(SEG)
    mats = [(lane[:, None] // g == lane[None, :] // g).astype(np.float32) / g
            for g in (HEAD_DIM, DIFF_QK_DIM)]
    return jnp.asarray(np.stack(mats), BF16)


def _extend_w_in(w_in):
    kv0 = 7 * SEG
    hd = HEAD_DIM
    cols = np.concatenate([
        np.arange(0, kv0),
        kv0 + np.array([0, 0, 1, 1]).repeat(hd) * hd + np.tile(np.arange(hd), 4),
        kv0 + 2 * hd + np.array([0, 0, 1, 1]).repeat(hd) * hd + np.tile(np.arange(hd), 4),
        np.arange(kv0 + 4 * hd, w_in.shape[1]),
    ])
    return w_in[:, cols].astype(BF16)


def _gain_rows(na_q, na_k, df_q, df_k, gq_q, gq_k, dl_q, dl_k):
    def row(g, scale):
        return jnp.tile(g.astype(F32), SEG // g.shape[0]) * scale
    sc64 = HEAD_DIM ** -0.5 * LOG2E
    sc32 = DIFF_QK_DIM ** -0.5 * LOG2E
    return jnp.stack([row(na_q, sc64), row(na_k, 1.0), row(df_q, sc32), row(df_k, 1.0),
                      row(gq_q, sc64), row(gq_k, 1.0), row(dl_q, sc64), row(dl_k, 1.0)])


def kernel(x, ffn1_norm, ffn1_w_gate, ffn1_w_up, ffn1_w_down, mix_norm, w_in, w_out, na_q_norm, na_k_norm, na_rel_bias, diff_q_norm, diff_k_norm, diff_lambda_q1, diff_lambda_k1, diff_lambda_q2, diff_lambda_k2, diff_out_norm, gqa_q_norm, gqa_k_norm, dil_q_norm, dil_k_norm, ffn2_norm, ffn2_w_gate, ffn2_w_up, ffn2_w_down):
    B, S, D = x.shape
    depth = w_in.shape[0]
    rows = S // GRID_W
    tables = _rope_tables(S)
    gmats = _group_matrices()
    ones_row = jnp.ones((1, SEG), F32)
    zero_lamp = jnp.zeros((4, DIFF_QK_DIM), F32)
    zero_li = jnp.zeros((1, 1), F32)

    for l in range(depth):
        x = _ffn(x, ffn1_norm[l], ffn1_w_gate[l].astype(BF16), ffn1_w_up[l].astype(BF16),
                 ffn1_w_down[l].astype(BF16))

        gains = _gain_rows(na_q_norm[l], na_k_norm[l], diff_q_norm[l], diff_k_norm[l],
                           gqa_q_norm[l], gqa_k_norm[l], dil_q_norm[l], dil_k_norm[l])
        qkv, vt_df, vt_gq, *dl_classes = _inproj(x, mix_norm[l], _extend_w_in(w_in[l]), gmats,
                                                 gains, tables)

        na_table = _na_table(na_rel_bias[l], rows)
        lamp = jnp.stack([diff_lambda_q1[l], diff_lambda_k1[l], diff_lambda_q2[l],
                          diff_lambda_k2[l]]).astype(F32)
        li = jnp.full((1, 1), 0.8 - 0.6 * math.exp(-0.3 * l), F32)
        ogain = jnp.tile(diff_out_norm[l].astype(F32), SEG // HEAD_DIM).reshape(1, SEG)
        dl_src = {1: (qkv, SEG_DL_Q)}
        dl_src.update({d: (a, 0) for d, a in zip(DILATIONS, dl_classes)})
        w_out_l = w_out[l].astype(BF16)
        ffn2 = (ffn2_norm[l], ffn2_w_gate[l].astype(BF16), ffn2_w_up[l].astype(BF16),
                ffn2_w_down[l].astype(BF16))

        def score_bound(qi, width):
            return width * jnp.max(jnp.abs(gains[qi])) * jnp.max(jnp.abs(gains[qi + 1]))
        bounds = {"na": score_bound(0, HEAD_DIM) + LOG2E * jnp.max(jnp.abs(na_rel_bias[l])),
                  "df": score_bound(2, DIFF_QK_DIM), "gq": score_bound(4, HEAD_DIM),
                  "dl": score_bound(6, HEAD_DIM)}
        shifts = {k: v.reshape(1, 1).astype(F32) for k, v in bounds.items()}

        def mix(bounded, x=x, qkv=qkv, vt_df=vt_df, vt_gq=vt_gq, dl_src=dl_src, shifts=shifts,
                na_table=na_table, ogain=ogain, lamp=lamp, li=li, w_out_l=w_out_l, ffn2=ffn2):
            o_a = _na(qkv, na_table, shifts["na"], bounded=bounded)
            o_b = _flash(qkv, vt_df, (SEG_DF_Q, SEG_DF_K), gmats[0], ogain, lamp, li,
                         shifts["df"], groups=2 * DIFF_HEADS, diff=True, bounded=bounded)
            o_c = _flash(qkv, vt_gq, (SEG_GQ_Q, SEG_GQ_K), gmats[0], ones_row, zero_lamp,
                         zero_li, shifts["gq"], groups=GQA_Q_HEADS, diff=False, bounded=bounded)
            dils = [_dilated(*dl_src[dil], (window // 2) // dil, shifts["dl"], bounded=bounded)
                    for window, dil in DIL_PAIRS]
            return _outproj_ffn(x, o_a, o_b, o_c, dils, w_out_l, *ffn2, bounded=bounded)

        safe = functools.reduce(jnp.maximum, bounds.values()) <= MAX_FIXED_SHIFT
        x = lax.cond(safe, functools.partial(mix, True), functools.partial(mix, False))
    return x
```

```python
import functools
import math

import numpy as np
import jax
import jax.numpy as jnp
from jax import lax
from jax.experimental import pallas as pl
from jax.experimental.pallas import tpu as pltpu

F32 = jnp.float32
BF16 = jnp.bfloat16

GRID_W = 64
HEAD_DIM = 64
RMS_EPS = 1e-6
MASK_VALUE = -1e30
LOG2E = math.log2(math.e)

NA_HEADS = 4
NA_WIN_H = 8
NA_WIN_W = 16
DIFF_HEADS = 4
DIFF_QK_DIM = HEAD_DIM // 2
GQA_Q_HEADS = 4
GQA_KV_HEADS = 2
AXIAL_THETA = 10000.0
DIL_HEADS = 4
DIL_PAIRS = ((128, 1), (512, 4), (2048, 16))
DILATIONS = tuple(d for _, d in DIL_PAIRS if d > 1)
ROPE_THETA = 500000.0
ROPE_FRACTION = 4

LANES = 128
SEG = 256
(SEG_NA_Q, SEG_NA_K, SEG_NA_V, SEG_DF_Q, SEG_DF_K, SEG_DF_V,
 SEG_GQ_Q, SEG_GQ_K, SEG_GQ_V, SEG_DL_Q, SEG_DL_K, SEG_DL_V) = range(12)
N_SEG = 12
EXT_WIDTH = N_SEG * SEG

VMEM_LIMIT = 52 * 1024 * 1024

VT_ROWS = HEAD_DIM + 16
COL_BLOCK = 256
QK_AHEAD = 4
KV_UNROLL = 4
FLASH_COLS = 4096
MAX_FIXED_SHIFT = 40.0

NA_QROWS = 8
NA_KROWS = 16


def _cparams(sem):
    return pltpu.CompilerParams(dimension_semantics=sem, vmem_limit_bytes=VMEM_LIMIT)


def _resident(shape):
    nd = len(shape)
    return pl.BlockSpec(shape, lambda *_: (0,) * nd, pipeline_mode=pl.Buffered(1))


def _rms_rows(x, gain):
    ms = jnp.mean(x * x, axis=-1, keepdims=True)
    return x * lax.rsqrt(ms + RMS_EPS) * gain


def _group_mean_sq(y, gmat):
    sq = y * y
    hi = sq.astype(BF16)
    lo = (sq - hi.astype(F32)).astype(BF16)
    return (jnp.dot(hi, gmat, preferred_element_type=F32)
            + jnp.dot(lo, gmat, preferred_element_type=F32))


def _dot_nt(a, b):
    return lax.dot_general(a, b, (((1,), (1,)), ((), ())), preferred_element_type=F32)


def _ffn_tile(x, g_ref, wg_ref, wu_ref, wd_ref, chunks):
    h = _rms_rows(x, g_ref[...]).astype(BF16)
    acc = None
    for c0, c1 in chunks:
        g = jnp.dot(h, wg_ref[:, c0:c1], preferred_element_type=F32)
        u = jnp.dot(h, wu_ref[:, c0:c1], preferred_element_type=F32)
        a = (g * (1.0 / (1.0 + jnp.exp(-g))) * u).astype(BF16)
        d = jnp.dot(a, wd_ref[c0:c1, :], preferred_element_type=F32)
        acc = d if acc is None else acc + d
    return x + 0.5 * acc


def _ffn_kernel(x_ref, g_ref, wg_ref, wu_ref, wd_ref, o_ref, *, chunks):
    o_ref[...] = _ffn_tile(x_ref[...], g_ref, wg_ref, wu_ref, wd_ref, chunks)


def _ffn_chunks(F, fc=512):
    return tuple((c, min(c + fc, F)) for c in range(0, F, fc))


def _ffn(x, gain, wg, wu, wd, *, tm=512):
    B, S, D = x.shape
    F = wg.shape[1]
    chunks = _ffn_chunks(F)
    xspec = pl.BlockSpec((None, tm, D), lambda b, i: (b, i, 0))
    return pl.pallas_call(
        functools.partial(_ffn_kernel, chunks=chunks),
        out_shape=jax.ShapeDtypeStruct(x.shape, F32),
        grid=(B, S // tm),
        in_specs=[xspec, _resident((1, D)), _resident((D, F)), _resident((D, F)),
                  _resident((F, D))],
        out_specs=xspec,
        compiler_params=_cparams(("parallel", "parallel")),
        name="ffn",
    )(x, gain.reshape(1, D), wg, wu, wd)


_SEG_PLAN = {
    SEG_NA_Q: (0, 0, None), SEG_NA_K: (1, 0, None),
    SEG_DF_Q: (2, 1, "df"), SEG_DF_K: (3, 1, "df"),
    SEG_GQ_Q: (4, 0, "gq"), SEG_GQ_K: (5, 0, "gq"),
    SEG_DL_Q: (6, 0, "dl"), SEG_DL_K: (7, 0, "dl"),
}
_ROPE_HALF = {"df": DIFF_QK_DIM // ROPE_FRACTION // 2, "gq": HEAD_DIM // 4,
              "dl": HEAD_DIM // ROPE_FRACTION // 2}


def _inproj_kernel(x_ref, gn_ref, w_ref, gmat_ref, gains_ref,
                   cdf_ref, sdf_ref, cgq_ref, sgq_ref, cdl_ref, sdl_ref,
                   o_ref, vtdf_ref, vtgq_ref, *dl_refs_and_scratch):
    *dl_refs, ys_ref = dl_refs_and_scratch
    tm = x_ref.shape[0]
    h = _rms_rows(x_ref[...], gn_ref[...]).astype(BF16)
    lane = lax.broadcasted_iota(jnp.int32, (1, SEG), 1)
    tables = {"df": (cdf_ref, sdf_ref), "gq": (cgq_ref, sgq_ref), "dl": (cdl_ref, sdl_ref)}
    vt_plan = {SEG_DF_V: (vtdf_ref, tuple(range(DIFF_HEADS))),
               SEG_GQ_V: (vtgq_ref, tuple(2 * j for j in range(GQA_KV_HEADS)))}
    for seg in range(N_SEG):
        y = jnp.dot(h, w_ref[:, seg * SEG:(seg + 1) * SEG], preferred_element_type=F32)
        if seg in vt_plan:
            vt_ref, lane_heads = vt_plan[seg]
            yt = y.T
            ones = jnp.ones((VT_ROWS - HEAD_DIM, yt.shape[1]), BF16)
            for j, lh in enumerate(lane_heads):
                vt_ref[j * VT_ROWS:j * VT_ROWS + HEAD_DIM, :] = (
                    yt[lh * HEAD_DIM:(lh + 1) * HEAD_DIM].astype(BF16))
                vt_ref[j * VT_ROWS + HEAD_DIM:(j + 1) * VT_ROWS, :] = ones
        plan = _SEG_PLAN.get(seg)
        if plan is not None:
            gi, mi, rope = plan
            ms = _group_mean_sq(y, gmat_ref[mi])
            y = y * lax.rsqrt(ms + RMS_EPS) * gains_ref[gi:gi + 1, :]
            if rope is not None:
                half = _ROPE_HALF[rope]
                c_ref, s_ref = tables[rope]
                first = (lane & (2 * half - 1)) < half
                partner = jnp.where(first, pltpu.roll(y, SEG - half, 1), pltpu.roll(y, half, 1))
                y = y * c_ref[...] + partner * s_ref[...]
        o_ref[:, seg * SEG:(seg + 1) * SEG] = y.astype(BF16)
        if SEG_DL_Q <= seg <= SEG_DL_V:
            for hf in range(SEG // LANES):
                ys_ref[hf] = y[:, hf * LANES:(hf + 1) * LANES]
            col = (seg - SEG_DL_Q) * SEG
            for dl_ref, dil in zip(dl_refs, DILATIONS):
                for r in range(dil):
                    for hf in range(SEG // LANES):
                        dl_ref[r, :, col + hf * LANES:col + (hf + 1) * LANES] = (
                            ys_ref[hf, pl.ds(r, tm // dil, stride=dil), :].astype(BF16))


def _inproj(x, gain, w_ext, gmats, gains, tables, *, tm=512):
    B, S, D = x.shape
    tspec = pl.BlockSpec((tm, SEG), lambda i, b: (i, 0))
    vt_rows = (DIFF_HEADS * VT_ROWS, GQA_KV_HEADS * VT_ROWS)
    return pl.pallas_call(
        _inproj_kernel,
        out_shape=(jax.ShapeDtypeStruct((B, S, EXT_WIDTH), BF16),)
        + tuple(jax.ShapeDtypeStruct((B, S // tm, r, tm), BF16) for r in vt_rows)
        + tuple(jax.ShapeDtypeStruct((B, d, S // d, 3 * SEG), BF16) for d in DILATIONS),
        grid=(S // tm, B),
        in_specs=[pl.BlockSpec((None, tm, D), lambda i, b: (b, i, 0)),
                  _resident((1, D)), _resident((D, EXT_WIDTH)),
                  _resident(gmats.shape), _resident(gains.shape)] + [tspec] * 6,
        out_specs=(pl.BlockSpec((None, tm, EXT_WIDTH), lambda i, b: (b, i, 0)),)
        + tuple(pl.BlockSpec((None, None, r, tm), lambda i, b: (b, i, 0, 0)) for r in vt_rows)
        + tuple(pl.BlockSpec((None, d, tm // d, 3 * SEG), lambda i, b: (b, 0, i, 0))
                for d in DILATIONS),
        scratch_shapes=[pltpu.VMEM((SEG // LANES, tm, LANES), F32)],
        compiler_params=_cparams(("parallel", "parallel")),
        name="inproj",
    )(x, gain.reshape(1, D), w_ext, gmats, gains, *tables)


def _flash_kernel(q_ref, k_ref, vt_ref, gmat_ref, ogain_ref, lamp_ref, li_ref, shift_ref, o_ref,
                  qmt_ref, m_ref, acc_ref, sbuf_ref, *, groups, tq, tk, diff, vheads, bounded):
    gw = SEG // groups
    qt = q_ref[...].astype(F32).T
    row_g = lax.broadcasted_iota(jnp.int32, (SEG, 1), 0) >> int(math.log2(gw))
    for r in range(groups):
        qmt_ref[:, r * tq:(r + 1) * tq] = jnp.where(row_g == r, qt, 0.0).astype(BF16)
    m_ref[...] = jnp.full(m_ref.shape, MASK_VALUE, F32)
    acc_ref[...] = jnp.zeros(acc_ref.shape, F32)
    ncol = groups * tq // COL_BLOCK

    nkv = k_ref.shape[0] // tk
    ahead = sbuf_ref.shape[0]

    def k_block(kv):
        return k_ref[pl.ds(pl.multiple_of(kv * tk, tk), tk), :]

    def qk(k, c):
        return jnp.dot(k, qmt_ref[:, c * COL_BLOCK:(c + 1) * COL_BLOCK],
                       preferred_element_type=F32)

    k0 = k_block(0)
    for c in range(ahead):
        sbuf_ref[c] = qk(k0, c)

    shift = shift_ref[...]

    def body(it, carry):
        pend = [sbuf_ref[c] for c in range(ahead)]
        for u in range(KV_UNROLL):
            kv = it * KV_UNROLL + u
            k = k_block(kv)
            k_next = k_block(jnp.minimum(kv + 1, nkv - 1))
            vt = vt_ref[kv]
            for c in range(ncol):
                hv = (c * COL_BLOCK // tq) * vheads // groups
                cs = slice(c * COL_BLOCK, (c + 1) * COL_BLOCK)
                s = pend.pop(0)
                if c + ahead < ncol:
                    pend.append(qk(k, c + ahead))
                elif u + 1 < KV_UNROLL:
                    pend.append(qk(k_next, c + ahead - ncol))
                else:
                    sbuf_ref[c + ahead - ncol] = qk(k_next, c + ahead - ncol)
                vth = vt[hv * VT_ROWS:(hv + 1) * VT_ROWS]
                if bounded:
                    p = jnp.exp2((s - shift).astype(BF16))
                    acc_ref[:, cs] = acc_ref[:, cs] + jnp.dot(vth, p, preferred_element_type=F32)
                else:
                    m_prev = m_ref[:, cs]
                    m_new = jnp.maximum(m_prev, jnp.max(s, axis=0, keepdims=True))
                    alpha = jnp.exp2(m_prev - m_new)
                    p = jnp.exp2(s - m_new).astype(BF16)
                    acc_ref[:, cs] = (acc_ref[:, cs] * alpha
                                      + jnp.dot(vth, p, preferred_element_type=F32))
                    m_ref[:, cs] = m_new
        return carry

    lax.fori_loop(0, nkv // KV_UNROLL, body, 0)

    acc = acc_ref[...]
    norm = acc[0:HEAD_DIM] * (1.0 / acc[HEAD_DIM:HEAD_DIM + 1])
    if diff:
        lp = lamp_ref[...]
        lam = (jnp.exp(jnp.sum(lp[0:1] * lp[1:2], axis=1, keepdims=True))
               - jnp.exp(jnp.sum(lp[2:3] * lp[3:4], axis=1, keepdims=True)) + li_ref[...])
        pieces = [norm[:, 2 * hd * tq:(2 * hd + 1) * tq]
                  - lam * norm[:, (2 * hd + 1) * tq:(2 * hd + 2) * tq] for hd in range(groups // 2)]
    else:
        pieces = [norm[:, r * tq:(r + 1) * tq] for r in range(groups)]
    out = jnp.concatenate(pieces, axis=0).T
    if diff:
        ms = _group_mean_sq(out, gmat_ref[...])
        out = out * lax.rsqrt(ms + RMS_EPS) * ogain_ref[...] * (1.0 - li_ref[...])
    o_ref[...] = out.astype(BF16)


def _flash(qkv, vt, segs, gmat, ogain, lamp, li, shift, *, groups, diff, bounded,
           cols=FLASH_COLS):
    B, S, _ = qkv.shape
    _, nkv, vrows, tk = vt.shape
    assert nkv % KV_UNROLL == 0 and S % (cols // groups) == 0
    tq = cols // groups
    qs, ks = segs
    return pl.pallas_call(
        functools.partial(_flash_kernel, groups=groups, tq=tq, tk=tk, diff=diff,
                          vheads=vrows // VT_ROWS, bounded=bounded),
        out_shape=jax.ShapeDtypeStruct((B, S, SEG), BF16),
        grid=(B, S // tq),
        in_specs=[pl.BlockSpec((None, tq, SEG), lambda b, i: (b, i, qs)),
                  pl.BlockSpec((None, S, SEG), lambda b, i: (b, 0, ks)),
                  pl.BlockSpec((None, nkv, vrows, tk), lambda b, i: (b, 0, 0, 0)),
                  _resident(gmat.shape), _resident(ogain.shape), _resident(lamp.shape),
                  _resident(li.shape), _resident(shift.shape)],
        out_specs=pl.BlockSpec((None, tq, SEG), lambda b, i: (b, i, 0)),
        scratch_shapes=[pltpu.VMEM((SEG, cols), BF16), pltpu.VMEM((1, cols), F32),
                        pltpu.VMEM((VT_ROWS, cols), F32),
                        pltpu.VMEM((QK_AHEAD, tk, COL_BLOCK), F32)],
        compiler_params=_cparams(("parallel", "parallel")),
        name=("diff_attn" if diff else "gqa_attn") + ("" if bounded else "_runmax"),
    )(qkv, qkv, vt, gmat, ogain, lamp, li, shift)


def _na_kernel(q_ref, k_ref, v_ref, tab_ref, o_ref, *, nq, nk, bounded):
    i = pl.program_id(2)
    seq = k_ref.shape[0]
    start = jnp.clip(i * nq - (nk - nq) // 2, 0, seq - nk)
    start = pl.multiple_of(start, (nk - nq) // 2)
    kw = k_ref[pl.ds(start, nk), :]
    vw = v_ref[pl.ds(start, nk), :]
    q = q_ref[...]
    if bounded:
        qt = q.astype(F32).T
        vt = vw.astype(F32).T
        row_h = (lax.broadcasted_iota(jnp.int32, (2 * HEAD_DIM, 1), 0)
                 >> int(math.log2(HEAD_DIM)))
        ones = jnp.ones((VT_ROWS - HEAD_DIM, nk), BF16)
        outs = []
        scores = [jnp.dot(kw, jnp.where(row_h == j, qt, 0.0).astype(BF16),
                          preferred_element_type=F32) for j in range(2)]
        for j in range(2):
            p = jnp.exp2(scores[j] + tab_ref[j]).astype(BF16)
            vth = jnp.concatenate(
                [vt[j * HEAD_DIM:(j + 1) * HEAD_DIM].astype(BF16), ones], axis=0)
            pv = jnp.dot(vth, p, preferred_element_type=F32)
            outs.append(pv[0:HEAD_DIM] * (1.0 / pv[HEAD_DIM:HEAD_DIM + 1]))
        o_ref[...] = jnp.concatenate(outs, axis=0).T.astype(BF16)
        return
    lane_h = lax.broadcasted_iota(jnp.int32, (1, 2 * HEAD_DIM), 1) >> int(math.log2(HEAD_DIM))
    outs = []
    for j in range(2):
        qm = jnp.where(lane_h == j, q, jnp.zeros_like(q))
        s = _dot_nt(qm, kw) + tab_ref[j]
        m = jnp.max(s, axis=1, keepdims=True)
        p = jnp.exp2(s - m)
        l = jnp.sum(p, axis=1, keepdims=True)
        outs.append(jnp.dot(p.astype(BF16), vw, preferred_element_type=F32) * (1.0 / l))
    o_ref[...] = jnp.where(lane_h == 0, outs[0], outs[1]).astype(BF16)


def _na_table(rel_bias, rows, shift=None):
    kh = min(NA_WIN_H, rows)
    n_ro, n_co = 2 * NA_WIN_H - 1, 2 * NA_WIN_W - 1
    c = np.arange(GRID_W)[:, None]
    kc = np.arange(GRID_W)[None, :]
    cs = np.clip(c - NA_WIN_W // 2, 0, GRID_W - NA_WIN_W)
    col_valid = (kc >= cs) & (kc < cs + NA_WIN_W)
    col_sel = (kc - c + (NA_WIN_W - 1))[..., None] == np.arange(n_co)
    row_sel, row_valid = [], []
    for r0 in (0, NA_QROWS, rows - NA_QROWS):
        kr0 = int(np.clip(r0 - (NA_KROWS - NA_QROWS) // 2, 0, rows - NA_KROWS))
        r = r0 + np.arange(NA_QROWS)[:, None]
        kr = kr0 + np.arange(NA_KROWS)[None, :]
        rs = np.clip(r - kh // 2, 0, rows - kh)
        row_valid.append((kr >= rs) & (kr < rs + kh))
        row_sel.append((kr - r + (NA_WIN_H - 1))[..., None] == np.arange(n_ro))
    row_sel = jnp.asarray(np.stack(row_sel), F32)
    valid = (np.stack(row_valid)[:, None, :, None, :, None]
             & col_valid[None, None, None, :, None, :])
    hi = lax.Precision.HIGHEST
    cols = jnp.einsum("hab,cdb->hacd", rel_bias.astype(F32), jnp.asarray(col_sel, F32),
                      precision=hi)
    nq, nk = NA_QROWS * GRID_W, NA_KROWS * GRID_W
    if shift is None:
        full = jnp.einsum("vqka,hacd->vhqckd", row_sel, cols, precision=hi)
        table = jnp.where(valid, full * LOG2E, MASK_VALUE)
        return table.reshape(3, rel_bias.shape[0], nq, nk)
    full = jnp.einsum("vqka,hacd->vhkdqc", row_sel, cols, precision=hi)
    table = jnp.where(valid.transpose(0, 1, 4, 5, 2, 3), full * LOG2E - shift, MASK_VALUE)
    return table.reshape(3, rel_bias.shape[0], nk, nq)


def _na(qkv, table, *, bounded):
    B, S, _ = qkv.shape
    nq, nk = NA_QROWS * GRID_W, NA_KROWS * GRID_W
    nblk = S // nq
    hp = 2 * HEAD_DIM

    def variant(i):
        return jnp.where(i == 0, 0, jnp.where(i == nblk - 1, 2, 1))

    return pl.pallas_call(
        functools.partial(_na_kernel, nq=nq, nk=nk, bounded=bounded),
        out_shape=jax.ShapeDtypeStruct((B, S, SEG), BF16),
        grid=(2, B, nblk),
        in_specs=[pl.BlockSpec((None, nq, hp), lambda p, b, i: (b, i, 2 * SEG_NA_Q + p)),
                  pl.BlockSpec((None, S, hp), lambda p, b, i: (b, 0, 2 * SEG_NA_K + p)),
                  pl.BlockSpec((None, S, hp), lambda p, b, i: (b, 0, 2 * SEG_NA_V + p)),
                  pl.BlockSpec((None, 2) + table.shape[2:], lambda p, b, i: (variant(i), p, 0, 0))],
        out_specs=pl.BlockSpec((None, nq, hp), lambda p, b, i: (b, i, p)),
        compiler_params=_cparams(("parallel", "parallel", "arbitrary")),
        name="na_attn" if bounded else "na_attn_rowmax",
    )(qkv, qkv, qkv, table)


def _dil_kernel(q_ref, k_ref, v_ref, shift_ref, o_ref, lse_ref, *, tq, halo, bounded):
    j = pl.program_id(2)
    length = k_ref.shape[0]
    win = tq + 2 * halo
    start = pl.multiple_of(jnp.clip(j * tq - halo, 0, length - win), halo)
    kw = k_ref[pl.ds(start, win), :]
    vw = v_ref[pl.ds(start, win), :]
    q = q_ref[...]
    qpos = j * tq + lax.broadcasted_iota(jnp.int32, (tq, win), 0)
    kpos = start + lax.broadcasted_iota(jnp.int32, (tq, win), 1)
    inside = jnp.abs(kpos - qpos) <= halo
    bias = jnp.where(inside, -shift_ref[...] if bounded else 0.0, MASK_VALUE)
    lane_h = lax.broadcasted_iota(jnp.int32, (1, SEG), 1) >> int(math.log2(HEAD_DIM))
    out = jnp.zeros((tq, SEG), F32)
    lse = jnp.zeros((tq, SEG), F32)
    for hd in range(DIL_HEADS):
        qm = jnp.where(lane_h == hd, q, jnp.zeros_like(q))
        s = _dot_nt(qm, kw) + bias
        if bounded:
            p = jnp.exp2(s).astype(BF16)
            pv = jnp.dot(p, jnp.where(lane_h == hd, vw, jnp.ones_like(vw)),
                         preferred_element_type=F32)
            out = jnp.where(lane_h == hd, pv, out)
            lse = jnp.where(lane_h == hd, pltpu.roll(pv, HEAD_DIM, 1), lse)
        else:
            m = jnp.max(s, axis=1, keepdims=True)
            p = jnp.exp2(s - m)
            l = jnp.sum(p, axis=1, keepdims=True)
            o = jnp.dot(p.astype(BF16), vw, preferred_element_type=F32) * (1.0 / l)
            out = jnp.where(lane_h == hd, o, out)
            lse = jnp.where(lane_h == hd, m + jnp.log(l) * LOG2E, lse)
    o_ref[...] = out
    lse_ref[...] = lse


def _dilated(src, first_seg, halo, shift, *, bounded, tq=256):
    if src.ndim == 3:
        (B, length, _), dil = src.shape, 1
        in_specs = [pl.BlockSpec((None, tq, SEG), lambda b, r, j: (b, j, first_seg)),
                    pl.BlockSpec((None, length, SEG), lambda b, r, j: (b, 0, first_seg + 1)),
                    pl.BlockSpec((None, length, SEG), lambda b, r, j: (b, 0, first_seg + 2))]
    else:
        B, dil, length, _ = src.shape
        in_specs = [pl.BlockSpec((None, None, tq, SEG), lambda b, r, j: (b, r, j, first_seg)),
                    pl.BlockSpec((None, None, length, SEG),
                                 lambda b, r, j: (b, r, 0, first_seg + 1)),
                    pl.BlockSpec((None, None, length, SEG),
                                 lambda b, r, j: (b, r, 0, first_seg + 2))]
    oshape = jax.ShapeDtypeStruct((B, dil, length, SEG), F32)
    ospec = pl.BlockSpec((None, None, tq, SEG), lambda b, r, j: (b, r, j, 0))
    return pl.pallas_call(
        functools.partial(_dil_kernel, tq=tq, halo=halo, bounded=bounded),
        out_shape=(oshape, oshape),
        grid=(B, dil, length // tq),
        in_specs=in_specs + [_resident(shift.shape)],
        out_specs=(ospec, ospec),
        compiler_params=_cparams(("parallel", "parallel", "arbitrary")),
        name=f"dil_attn_{dil}" + ("" if bounded else "_rowmax"),
    )(src, src, src, shift)


def _outproj_kernel(x_ref, oa_ref, ob_ref, oc_ref, d0_ref, d1_ref, d2_ref,
                    l0_ref, l1_ref, l2_ref, w_ref, fg_ref, fwg_ref, fwu_ref, fwd_ref,
                    o_ref, cat_ref, *row_scratch, bounded, chunks):
    row_scratch = list(row_scratch)

    def token_rows(ref):
        dil = ref.shape[0]
        if dil == 1:
            return ref[0]
        buf = row_scratch.pop()
        for r in range(dil):
            slab = ref[r]
            for hf in range(SEG // LANES):
                buf[hf, pl.ds(r, ref.shape[1], stride=dil), :] = slab[:, hf * LANES:(hf + 1) * LANES]
        return jnp.concatenate([buf[hf] for hf in range(SEG // LANES)], axis=1)

    l0, l1, l2 = token_rows(l0_ref), token_rows(l1_ref), token_rows(l2_ref)
    if bounded:
        od = ((token_rows(d0_ref) + token_rows(d1_ref) + token_rows(d2_ref))
              * (1.0 / (l0 + l1 + l2)))
    else:
        m = jnp.maximum(jnp.maximum(l0, l1), l2)
        e0, e1, e2 = jnp.exp2(l0 - m), jnp.exp2(l1 - m), jnp.exp2(l2 - m)
        od = ((e0 * token_rows(d0_ref) + e1 * token_rows(d1_ref) + e2 * token_rows(d2_ref))
              * (1.0 / (e0 + e1 + e2)))
    cat_ref[:, 0 * SEG:1 * SEG] = oa_ref[...]
    cat_ref[:, 1 * SEG:2 * SEG] = ob_ref[...]
    cat_ref[:, 2 * SEG:3 * SEG] = oc_ref[...]
    cat_ref[:, 3 * SEG:4 * SEG] = od.astype(BF16)
    mixed = x_ref[...] + jnp.dot(cat_ref[...], w_ref[...], preferred_element_type=F32)
    o_ref[...] = _ffn_tile(mixed, fg_ref, fwg_ref, fwu_ref, fwd_ref, chunks)


def _outproj_ffn(x, oa, ob, oc, dils, w_out, fgain, fwg, fwu, fwd, *, bounded, tm=512):
    B, S, D = x.shape
    F = fwg.shape[1]
    xspec = pl.BlockSpec((None, tm, D), lambda b, i: (b, i, 0))
    sspec = pl.BlockSpec((None, tm, SEG), lambda b, i: (b, i, 0))
    outs = [d[0] for d in dils]
    lses = [d[1] for d in dils]
    dspecs = [pl.BlockSpec((None, a.shape[1], tm // a.shape[1], SEG), lambda b, i: (b, 0, i, 0))
              for a in outs + lses]
    n_scratch = sum(a.shape[1] > 1 for a in outs + lses)
    return pl.pallas_call(
        functools.partial(_outproj_kernel, bounded=bounded, chunks=_ffn_chunks(F)),
        out_shape=jax.ShapeDtypeStruct(x.shape, F32),
        grid=(B, S // tm),
        in_specs=[xspec] + [sspec] * 3 + dspecs
        + [_resident(w_out.shape), _resident((1, D)), _resident((D, F)), _resident((D, F)),
           _resident((F, D))],
        out_specs=xspec,
        scratch_shapes=[pltpu.VMEM((tm, 4 * SEG), BF16)]
        + [pltpu.VMEM((SEG // LANES, tm, LANES), F32)] * n_scratch,
        compiler_params=_cparams(("parallel", "parallel")),
        name="outproj_ffn" if bounded else "outproj_lse_ffn",
    )(x, oa, ob, oc, *outs, *lses, w_out, fgain.reshape(1, D), fwg, fwu, fwd)


def _rope_cs(pos, dim, theta):
    inv_freq = 1.0 / (theta ** (jnp.arange(0, dim, 2, dtype=F32) / dim))
    ang = pos.astype(F32)[:, None] * inv_freq[None, :]
    return jnp.cos(ang), jnp.sin(ang)


def _rope_tables(S):
    pos = jnp.arange(S, dtype=jnp.int32)

    def expand(parts_c, parts_s, reps):
        return (jnp.tile(jnp.concatenate(parts_c, axis=1), (1, reps)),
                jnp.tile(jnp.concatenate(parts_s, axis=1), (1, reps)))

    c, s = _rope_cs(pos, DIFF_QK_DIM // ROPE_FRACTION, ROPE_THETA)
    rest = DIFF_QK_DIM - 2 * c.shape[1]
    df = expand([c, c, jnp.ones((S, rest), F32)], [-s, s, jnp.zeros((S, rest), F32)],
                SEG // DIFF_QK_DIM)
    c, s = _rope_cs(pos, HEAD_DIM // ROPE_FRACTION, ROPE_THETA)
    rest = HEAD_DIM - 2 * c.shape[1]
    dl = expand([c, c, jnp.ones((S, rest), F32)], [-s, s, jnp.zeros((S, rest), F32)],
                SEG // HEAD_DIM)
    cr, sr = _rope_cs(pos // GRID_W, HEAD_DIM // 2, AXIAL_THETA)
    cc, sc = _rope_cs(pos % GRID_W, HEAD_DIM // 2, AXIAL_THETA)
    gq = expand([cr, cr, cc, cc], [-sr, sr, -sc, sc], SEG // HEAD_DIM)
    return df + gq + dl


def _group_matrices():
    lane = np.arange(SEG)
    mats = [(lane[:, None] // g == lane[None, :] // g).astype(np.float32) / g
            for g in (HEAD_DIM, DIFF_QK_DIM)]
    return jnp.asarray(np.stack(mats), BF16)


def _extend_w_in(w_in):
    kv0 = 7 * SEG
    hd = HEAD_DIM
    cols = np.concatenate([
        np.arange(0, kv0),
        kv0 + np.array([0, 0, 1, 1]).repeat(hd) * hd + np.tile(np.arange(hd), 4),
        kv0 + 2 * hd + np.array([0, 0, 1, 1]).repeat(hd) * hd + np.tile(np.arange(hd), 4),
        np.arange(kv0 + 4 * hd, w_in.shape[1]),
    ])
    return w_in[:, cols].astype(BF16)


def _gain_rows(na_q, na_k, df_q, df_k, gq_q, gq_k, dl_q, dl_k):
    def row(g, scale):
        return jnp.tile(g.astype(F32), SEG // g.shape[0]) * scale
    sc64 = HEAD_DIM ** -0.5 * LOG2E
    sc32 = DIFF_QK_DIM ** -0.5 * LOG2E
    return jnp.stack([row(na_q, sc64), row(na_k, 1.0), row(df_q, sc32), row(df_k, 1.0),
                      row(gq_q, sc64), row(gq_k, 1.0), row(dl_q, sc64), row(dl_k, 1.0)])


def kernel(x, ffn1_norm, ffn1_w_gate, ffn1_w_up, ffn1_w_down, mix_norm, w_in, w_out, na_q_norm, na_k_norm, na_rel_bias, diff_q_norm, diff_k_norm, diff_lambda_q1, diff_lambda_k1, diff_lambda_q2, diff_lambda_k2, diff_out_norm, gqa_q_norm, gqa_k_norm, dil_q_norm, dil_k_norm, ffn2_norm, ffn2_w_gate, ffn2_w_up, ffn2_w_down):
    B, S, D = x.shape
    depth = w_in.shape[0]
    rows = S // GRID_W
    tables = _rope_tables(S)
    gmats = _group_matrices()
    ones_row = jnp.ones((1, SEG), F32)
    zero_lamp = jnp.zeros((4, DIFF_QK_DIM), F32)
    zero_li = jnp.zeros((1, 1), F32)

    for l in range(depth):
        x = _ffn(x, ffn1_norm[l], ffn1_w_gate[l].astype(BF16), ffn1_w_up[l].astype(BF16),
                 ffn1_w_down[l].astype(BF16))

        gains = _gain_rows(na_q_norm[l], na_k_norm[l], diff_q_norm[l], diff_k_norm[l],
                           gqa_q_norm[l], gqa_k_norm[l], dil_q_norm[l], dil_k_norm[l])
        qkv, vt_df, vt_gq, *dl_classes = _inproj(x, mix_norm[l], _extend_w_in(w_in[l]), gmats,
                                                 gains, tables)

        na_bias = na_rel_bias[l]
        lamp = jnp.stack([diff_lambda_q1[l], diff_lambda_k1[l], diff_lambda_q2[l],
                          diff_lambda_k2[l]]).astype(F32)
        li = jnp.full((1, 1), 0.8 - 0.6 * math.exp(-0.3 * l), F32)
        ogain = jnp.tile(diff_out_norm[l].astype(F32), SEG // HEAD_DIM).reshape(1, SEG)
        dl_src = {1: (qkv, SEG_DL_Q)}
        dl_src.update({d: (a, 0) for d, a in zip(DILATIONS, dl_classes)})
        w_out_l = w_out[l].astype(BF16)
        ffn2 = (ffn2_norm[l], ffn2_w_gate[l].astype(BF16), ffn2_w_up[l].astype(BF16),
                ffn2_w_down[l].astype(BF16))

        def score_bound(qi, width):
            return width * jnp.max(jnp.abs(gains[qi])) * jnp.max(jnp.abs(gains[qi + 1]))
        bounds = {"na": score_bound(0, HEAD_DIM) + LOG2E * jnp.max(jnp.abs(na_rel_bias[l])),
                  "df": score_bound(2, DIFF_QK_DIM), "gq": score_bound(4, HEAD_DIM),
                  "dl": score_bound(6, HEAD_DIM)}
        shifts = {k: v.reshape(1, 1).astype(F32) for k, v in bounds.items()}

        def mix(bounded, x=x, qkv=qkv, vt_df=vt_df, vt_gq=vt_gq, dl_src=dl_src, shifts=shifts,
                na_bias=na_bias, ogain=ogain, lamp=lamp, li=li, w_out_l=w_out_l, ffn2=ffn2):
            na_table = _na_table(na_bias, rows, shifts["na"][0, 0] if bounded else None)
            o_a = _na(qkv, na_table, bounded=bounded)
            o_b = _flash(qkv, vt_df, (SEG_DF_Q, SEG_DF_K), gmats[0], ogain, lamp, li,
                         shifts["df"], groups=2 * DIFF_HEADS, diff=True, bounded=bounded)
            o_c = _flash(qkv, vt_gq, (SEG_GQ_Q, SEG_GQ_K), gmats[0], ones_row, zero_lamp,
                         zero_li, shifts["gq"], groups=GQA_Q_HEADS, diff=False, bounded=bounded)
            dils = [_dilated(*dl_src[dil], (window // 2) // dil, shifts["dl"], bounded=bounded)
                    for window, dil in DIL_PAIRS]
            return _outproj_ffn(x, o_a, o_b, o_c, dils, w_out_l, *ffn2, bounded=bounded)

        safe = functools.reduce(jnp.maximum, bounds.values()) <= MAX_FIXED_SHIFT
        x = lax.cond(safe, functools.partial(mix, True), functools.partial(mix, False))
    return x
```

```python
import functools
import math

import numpy as np
import jax
import jax.numpy as jnp
from jax import lax
from jax.experimental import pallas as pl
from jax.experimental.pallas import tpu as pltpu

F32 = jnp.float32
BF16 = jnp.bfloat16

GRID_W = 64
HEAD_DIM = 64
RMS_EPS = 1e-6
MASK_VALUE = -1e30
LOG2E = math.log2(math.e)

NA_HEADS = 4
NA_WIN_H = 8
NA_WIN_W = 16
DIFF_HEADS = 4
DIFF_QK_DIM = HEAD_DIM // 2
GQA_Q_HEADS = 4
GQA_KV_HEADS = 2
AXIAL_THETA = 10000.0
DIL_HEADS = 4
DIL_PAIRS = ((128, 1), (512, 4), (2048, 16))
DILATIONS = tuple(d for _, d in DIL_PAIRS if d > 1)
ROPE_THETA = 500000.0
ROPE_FRACTION = 4

LANES = 128
SEG = 256
(SEG_NA_Q, SEG_NA_K, SEG_NA_V, SEG_DF_Q, SEG_DF_K, SEG_DF_V,
 SEG_GQ_Q, SEG_GQ_K, SEG_GQ_V, SEG_DL_Q, SEG_DL_K, SEG_DL_V) = range(12)
N_SEG = 12
EXT_WIDTH = N_SEG * SEG

VMEM_LIMIT = 52 * 1024 * 1024

VT_ROWS = HEAD_DIM + 16
COL_BLOCK = 256
DIL_SUB = 128
DIL_TQ_FIXED_SHIFT = 1024
DIL_TQ_ROWMAX = 256
QK_AHEAD = 4
KV_UNROLL = 4
FLASH_COLS = 4096
MAX_FIXED_SHIFT = 40.0

NA_QROWS = 8
NA_KROWS = 16


def _cparams(sem):
    return pltpu.CompilerParams(dimension_semantics=sem, vmem_limit_bytes=VMEM_LIMIT)


def _resident(shape):
    nd = len(shape)
    return pl.BlockSpec(shape, lambda *_: (0,) * nd, pipeline_mode=pl.Buffered(1))


def _rms_rows(x, gain):
    ms = jnp.mean(x * x, axis=-1, keepdims=True)
    return x * lax.rsqrt(ms + RMS_EPS) * gain


def _group_mean_sq(y, gmat):
    sq = y * y
    hi = sq.astype(BF16)
    lo = (sq - hi.astype(F32)).astype(BF16)
    return (jnp.dot(hi, gmat, preferred_element_type=F32)
            + jnp.dot(lo, gmat, preferred_element_type=F32))


def _dot_nt(a, b):
    return lax.dot_general(a, b, (((1,), (1,)), ((), ())), preferred_element_type=F32)


def _ffn_tile(x, g_ref, wg_ref, wu_ref, wd_ref, chunks):
    h = _rms_rows(x, g_ref[...]).astype(BF16)
    acc = None
    for c0, c1 in chunks:
        g = jnp.dot(h, wg_ref[:, c0:c1], preferred_element_type=F32)
        u = jnp.dot(h, wu_ref[:, c0:c1], preferred_element_type=F32)
        a = (g * (1.0 / (1.0 + jnp.exp(-g))) * u).astype(BF16)
        d = jnp.dot(a, wd_ref[c0:c1, :], preferred_element_type=F32)
        acc = d if acc is None else acc + d
    return x + 0.5 * acc


def _ffn_kernel(x_ref, g_ref, wg_ref, wu_ref, wd_ref, o_ref, *, chunks):
    o_ref[...] = _ffn_tile(x_ref[...], g_ref, wg_ref, wu_ref, wd_ref, chunks)


def _ffn_chunks(F, fc=512):
    return tuple((c, min(c + fc, F)) for c in range(0, F, fc))


def _ffn(x, gain, wg, wu, wd, *, tm=512):
    B, S, D = x.shape
    F = wg.shape[1]
    chunks = _ffn_chunks(F)
    xspec = pl.BlockSpec((None, tm, D), lambda b, i: (b, i, 0))
    return pl.pallas_call(
        functools.partial(_ffn_kernel, chunks=chunks),
        out_shape=jax.ShapeDtypeStruct(x.shape, F32),
        grid=(B, S // tm),
        in_specs=[xspec, _resident((1, D)), _resident((D, F)), _resident((D, F)),
                  _resident((F, D))],
        out_specs=xspec,
        compiler_params=_cparams(("parallel", "parallel")),
        name="ffn",
    )(x, gain.reshape(1, D), wg, wu, wd)


_SEG_PLAN = {
    SEG_NA_Q: (0, 0, None), SEG_NA_K: (1, 0, None),
    SEG_DF_Q: (2, 1, "df"), SEG_DF_K: (3, 1, "df"),
    SEG_GQ_Q: (4, 0, "gq"), SEG_GQ_K: (5, 0, "gq"),
    SEG_DL_Q: (6, 0, "dl"), SEG_DL_K: (7, 0, "dl"),
}
_ROPE_HALF = {"df": DIFF_QK_DIM // ROPE_FRACTION // 2, "gq": HEAD_DIM // 4,
              "dl": HEAD_DIM // ROPE_FRACTION // 2}


def _inproj_kernel(x_ref, gn_ref, w_ref, gmat_ref, gains_ref,
                   cdf_ref, sdf_ref, cgq_ref, sgq_ref, cdl_ref, sdl_ref,
                   o_ref, vtdf_ref, vtgq_ref, *dl_refs_and_scratch):
    *dl_refs, ys_ref = dl_refs_and_scratch
    tm = x_ref.shape[0]
    h = _rms_rows(x_ref[...], gn_ref[...]).astype(BF16)
    lane = lax.broadcasted_iota(jnp.int32, (1, SEG), 1)
    tables = {"df": (cdf_ref, sdf_ref), "gq": (cgq_ref, sgq_ref), "dl": (cdl_ref, sdl_ref)}
    vt_plan = {SEG_DF_V: (vtdf_ref, tuple(range(DIFF_HEADS))),
               SEG_GQ_V: (vtgq_ref, tuple(2 * j for j in range(GQA_KV_HEADS)))}
    for seg in range(N_SEG):
        y = jnp.dot(h, w_ref[:, seg * SEG:(seg + 1) * SEG], preferred_element_type=F32)
        if seg in vt_plan:
            vt_ref, lane_heads = vt_plan[seg]
            yt = y.T
            ones = jnp.ones((VT_ROWS - HEAD_DIM, yt.shape[1]), BF16)
            for j, lh in enumerate(lane_heads):
                vt_ref[j * VT_ROWS:j * VT_ROWS + HEAD_DIM, :] = (
                    yt[lh * HEAD_DIM:(lh + 1) * HEAD_DIM].astype(BF16))
                vt_ref[j * VT_ROWS + HEAD_DIM:(j + 1) * VT_ROWS, :] = ones
        plan = _SEG_PLAN.get(seg)
        if plan is not None:
            gi, mi, rope = plan
            ms = _group_mean_sq(y, gmat_ref[mi])
            y = y * lax.rsqrt(ms + RMS_EPS) * gains_ref[gi:gi + 1, :]
            if rope is not None:
                half = _ROPE_HALF[rope]
                c_ref, s_ref = tables[rope]
                first = (lane & (2 * half - 1)) < half
                partner = jnp.where(first, pltpu.roll(y, SEG - half, 1), pltpu.roll(y, half, 1))
                y = y * c_ref[...] + partner * s_ref[...]
        o_ref[:, seg * SEG:(seg + 1) * SEG] = y.astype(BF16)
        if SEG_DL_Q <= seg <= SEG_DL_V:
            for hf in range(SEG // LANES):
                ys_ref[hf] = y[:, hf * LANES:(hf + 1) * LANES]
            col = (seg - SEG_DL_Q) * SEG
            for dl_ref, dil in zip(dl_refs, DILATIONS):
                for r in range(dil):
                    for hf in range(SEG // LANES):
                        dl_ref[r, :, col + hf * LANES:col + (hf + 1) * LANES] = (
                            ys_ref[hf, pl.ds(r, tm // dil, stride=dil), :].astype(BF16))


def _inproj(x, gain, w_ext, gmats, gains, tables, *, tm=512):
    B, S, D = x.shape
    tspec = pl.BlockSpec((tm, SEG), lambda i, b: (i, 0))
    vt_rows = (DIFF_HEADS * VT_ROWS, GQA_KV_HEADS * VT_ROWS)
    return pl.pallas_call(
        _inproj_kernel,
        out_shape=(jax.ShapeDtypeStruct((B, S, EXT_WIDTH), BF16),)
        + tuple(jax.ShapeDtypeStruct((B, S // tm, r, tm), BF16) for r in vt_rows)
        + tuple(jax.ShapeDtypeStruct((B, d, S // d, 3 * SEG), BF16) for d in DILATIONS),
        grid=(S // tm, B),
        in_specs=[pl.BlockSpec((None, tm, D), lambda i, b: (b, i, 0)),
                  _resident((1, D)), _resident((D, EXT_WIDTH)),
                  _resident(gmats.shape), _resident(gains.shape)] + [tspec] * 6,
        out_specs=(pl.BlockSpec((None, tm, EXT_WIDTH), lambda i, b: (b, i, 0)),)
        + tuple(pl.BlockSpec((None, None, r, tm), lambda i, b: (b, i, 0, 0)) for r in vt_rows)
        + tuple(pl.BlockSpec((None, d, tm // d, 3 * SEG), lambda i, b: (b, 0, i, 0))
                for d in DILATIONS),
        scratch_shapes=[pltpu.VMEM((SEG // LANES, tm, LANES), F32)],
        compiler_params=_cparams(("parallel", "parallel")),
        name="inproj",
    )(x, gain.reshape(1, D), w_ext, gmats, gains, *tables)


def _flash_kernel(q_ref, k_ref, vt_ref, gmat_ref, ogain_ref, lamp_ref, li_ref, shift_ref, o_ref,
                  qmt_ref, m_ref, acc_ref, sbuf_ref, *, groups, tq, tk, diff, vheads, bounded):
    gw = SEG // groups
    qt = q_ref[...].astype(F32).T
    row_g = lax.broadcasted_iota(jnp.int32, (SEG, 1), 0) >> int(math.log2(gw))
    for r in range(groups):
        qmt_ref[:, r * tq:(r + 1) * tq] = jnp.where(row_g == r, qt, 0.0).astype(BF16)
    m_ref[...] = jnp.full(m_ref.shape, MASK_VALUE, F32)
    acc_ref[...] = jnp.zeros(acc_ref.shape, F32)
    ncol = groups * tq // COL_BLOCK

    nkv = k_ref.shape[0] // tk
    ahead = sbuf_ref.shape[0]

    def k_block(kv):
        return k_ref[pl.ds(pl.multiple_of(kv * tk, tk), tk), :]

    def qk(k, c):
        return jnp.dot(k, qmt_ref[:, c * COL_BLOCK:(c + 1) * COL_BLOCK],
                       preferred_element_type=F32)

    k0 = k_block(0)
    for c in range(ahead):
        sbuf_ref[c] = qk(k0, c)

    shift = shift_ref[...]

    def body(it, carry):
        pend = [sbuf_ref[c] for c in range(ahead)]
        for u in range(KV_UNROLL):
            kv = it * KV_UNROLL + u
            k = k_block(kv)
            k_next = k_block(jnp.minimum(kv + 1, nkv - 1))
            vt = vt_ref[kv]
            for c in range(ncol):
                hv = (c * COL_BLOCK // tq) * vheads // groups
                cs = slice(c * COL_BLOCK, (c + 1) * COL_BLOCK)
                s = pend.pop(0)
                if c + ahead < ncol:
                    pend.append(qk(k, c + ahead))
                elif u + 1 < KV_UNROLL:
                    pend.append(qk(k_next, c + ahead - ncol))
                else:
                    sbuf_ref[c + ahead - ncol] = qk(k_next, c + ahead - ncol)
                vth = vt[hv * VT_ROWS:(hv + 1) * VT_ROWS]
                if bounded:
                    p = jnp.exp2((s - shift).astype(BF16))
                    acc_ref[:, cs] = acc_ref[:, cs] + jnp.dot(vth, p, preferred_element_type=F32)
                else:
                    m_prev = m_ref[:, cs]
                    m_new = jnp.maximum(m_prev, jnp.max(s, axis=0, keepdims=True))
                    alpha = jnp.exp2(m_prev - m_new)
                    p = jnp.exp2(s - m_new).astype(BF16)
                    acc_ref[:, cs] = (acc_ref[:, cs] * alpha
                                      + jnp.dot(vth, p, preferred_element_type=F32))
                    m_ref[:, cs] = m_new
        return carry

    lax.fori_loop(0, nkv // KV_UNROLL, body, 0)

    acc = acc_ref[...]
    norm = acc[0:HEAD_DIM] * (1.0 / acc[HEAD_DIM:HEAD_DIM + 1])
    if diff:
        lp = lamp_ref[...]
        lam = (jnp.exp(jnp.sum(lp[0:1] * lp[1:2], axis=1, keepdims=True))
               - jnp.exp(jnp.sum(lp[2:3] * lp[3:4], axis=1, keepdims=True)) + li_ref[...])
        pieces = [norm[:, 2 * hd * tq:(2 * hd + 1) * tq]
                  - lam * norm[:, (2 * hd + 1) * tq:(2 * hd + 2) * tq] for hd in range(groups // 2)]
    else:
        pieces = [norm[:, r * tq:(r + 1) * tq] for r in range(groups)]
    out = jnp.concatenate(pieces, axis=0).T
    if diff:
        ms = _group_mean_sq(out, gmat_ref[...])
        out = out * lax.rsqrt(ms + RMS_EPS) * ogain_ref[...] * (1.0 - li_ref[...])
    o_ref[...] = out.astype(BF16)


def _flash(qkv, vt, segs, gmat, ogain, lamp, li, shift, *, groups, diff, bounded,
           cols=FLASH_COLS):
    B, S, _ = qkv.shape
    _, nkv, vrows, tk = vt.shape
    assert nkv % KV_UNROLL == 0 and S % (cols // groups) == 0
    tq = cols // groups
    qs, ks = segs
    return pl.pallas_call(
        functools.partial(_flash_kernel, groups=groups, tq=tq, tk=tk, diff=diff,
                          vheads=vrows // VT_ROWS, bounded=bounded),
        out_shape=jax.ShapeDtypeStruct((B, S, SEG), BF16),
        grid=(B, S // tq),
        in_specs=[pl.BlockSpec((None, tq, SEG), lambda b, i: (b, i, qs)),
                  pl.BlockSpec((None, S, SEG), lambda b, i: (b, 0, ks)),
                  pl.BlockSpec((None, nkv, vrows, tk), lambda b, i: (b, 0, 0, 0)),
                  _resident(gmat.shape), _resident(ogain.shape), _resident(lamp.shape),
                  _resident(li.shape), _resident(shift.shape)],
        out_specs=pl.BlockSpec((None, tq, SEG), lambda b, i: (b, i, 0)),
        scratch_shapes=[pltpu.VMEM((SEG, cols), BF16), pltpu.VMEM((1, cols), F32),
                        pltpu.VMEM((VT_ROWS, cols), F32),
                        pltpu.VMEM((QK_AHEAD, tk, COL_BLOCK), F32)],
        compiler_params=_cparams(("parallel", "parallel")),
        name=("diff_attn" if diff else "gqa_attn") + ("" if bounded else "_runmax"),
    )(qkv, qkv, vt, gmat, ogain, lamp, li, shift)


def _na_kernel(q_ref, k_ref, v_ref, tab_ref, o_ref, *, nq, nk, bounded):
    i = pl.program_id(2)
    seq = k_ref.shape[0]
    start = jnp.clip(i * nq - (nk - nq) // 2, 0, seq - nk)
    start = pl.multiple_of(start, (nk - nq) // 2)
    kw = k_ref[pl.ds(start, nk), :]
    vw = v_ref[pl.ds(start, nk), :]
    q = q_ref[...]
    if bounded:
        qt = q.astype(F32).T
        vt = vw.astype(F32).T
        row_h = (lax.broadcasted_iota(jnp.int32, (2 * HEAD_DIM, 1), 0)
                 >> int(math.log2(HEAD_DIM)))
        ones = jnp.ones((VT_ROWS - HEAD_DIM, nk), BF16)
        outs = []
        scores = [jnp.dot(kw, jnp.where(row_h == j, qt, 0.0).astype(BF16),
                          preferred_element_type=F32) for j in range(2)]
        for j in range(2):
            p = jnp.exp2(scores[j] + tab_ref[j]).astype(BF16)
            vth = jnp.concatenate(
                [vt[j * HEAD_DIM:(j + 1) * HEAD_DIM].astype(BF16), ones], axis=0)
            pv = jnp.dot(vth, p, preferred_element_type=F32)
            outs.append(pv[0:HEAD_DIM] * (1.0 / pv[HEAD_DIM:HEAD_DIM + 1]))
        o_ref[...] = jnp.concatenate(outs, axis=0).T.astype(BF16)
        return
    lane_h = lax.broadcasted_iota(jnp.int32, (1, 2 * HEAD_DIM), 1) >> int(math.log2(HEAD_DIM))
    outs = []
    for j in range(2):
        qm = jnp.where(lane_h == j, q, jnp.zeros_like(q))
        s = _dot_nt(qm, kw) + tab_ref[j]
        m = jnp.max(s, axis=1, keepdims=True)
        p = jnp.exp2(s - m)
        l = jnp.sum(p, axis=1, keepdims=True)
        outs.append(jnp.dot(p.astype(BF16), vw, preferred_element_type=F32) * (1.0 / l))
    o_ref[...] = jnp.where(lane_h == 0, outs[0], outs[1]).astype(BF16)


def _na_table(rel_bias, rows, shift=None):
    kh = min(NA_WIN_H, rows)
    n_ro, n_co = 2 * NA_WIN_H - 1, 2 * NA_WIN_W - 1
    c = np.arange(GRID_W)[:, None]
    kc = np.arange(GRID_W)[None, :]
    cs = np.clip(c - NA_WIN_W // 2, 0, GRID_W - NA_WIN_W)
    col_valid = (kc >= cs) & (kc < cs + NA_WIN_W)
    col_sel = (kc - c + (NA_WIN_W - 1))[..., None] == np.arange(n_co)
    row_sel, row_valid = [], []
    for r0 in (0, NA_QROWS, rows - NA_QROWS):
        kr0 = int(np.clip(r0 - (NA_KROWS - NA_QROWS) // 2, 0, rows - NA_KROWS))
        r = r0 + np.arange(NA_QROWS)[:, None]
        kr = kr0 + np.arange(NA_KROWS)[None, :]
        rs = np.clip(r - kh // 2, 0, rows - kh)
        row_valid.append((kr >= rs) & (kr < rs + kh))
        row_sel.append((kr - r + (NA_WIN_H - 1))[..., None] == np.arange(n_ro))
    row_sel = jnp.asarray(np.stack(row_sel), F32)
    valid = (np.stack(row_valid)[:, None, :, None, :, None]
             & col_valid[None, None, None, :, None, :])
    hi = lax.Precision.HIGHEST
    cols = jnp.einsum("hab,cdb->hacd", rel_bias.astype(F32), jnp.asarray(col_sel, F32),
                      precision=hi)
    nq, nk = NA_QROWS * GRID_W, NA_KROWS * GRID_W
    if shift is None:
        full = jnp.einsum("vqka,hacd->vhqckd", row_sel, cols, precision=hi)
        table = jnp.where(valid, full * LOG2E, MASK_VALUE)
        return table.reshape(3, rel_bias.shape[0], nq, nk)
    full = jnp.einsum("vqka,hacd->vhkdqc", row_sel, cols, precision=hi)
    table = jnp.where(valid.transpose(0, 1, 4, 5, 2, 3), full * LOG2E - shift, MASK_VALUE)
    return table.reshape(3, rel_bias.shape[0], nk, nq)


def _na(qkv, table, *, bounded):
    B, S, _ = qkv.shape
    nq, nk = NA_QROWS * GRID_W, NA_KROWS * GRID_W
    nblk = S // nq
    hp = 2 * HEAD_DIM

    def variant(i):
        return jnp.where(i == 0, 0, jnp.where(i == nblk - 1, 2, 1))

    return pl.pallas_call(
        functools.partial(_na_kernel, nq=nq, nk=nk, bounded=bounded),
        out_shape=jax.ShapeDtypeStruct((B, S, SEG), BF16),
        grid=(2, B, nblk),
        in_specs=[pl.BlockSpec((None, nq, hp), lambda p, b, i: (b, i, 2 * SEG_NA_Q + p)),
                  pl.BlockSpec((None, S, hp), lambda p, b, i: (b, 0, 2 * SEG_NA_K + p)),
                  pl.BlockSpec((None, S, hp), lambda p, b, i: (b, 0, 2 * SEG_NA_V + p)),
                  pl.BlockSpec((None, 2) + table.shape[2:], lambda p, b, i: (variant(i), p, 0, 0))],
        out_specs=pl.BlockSpec((None, nq, hp), lambda p, b, i: (b, i, p)),
        compiler_params=_cparams(("parallel", "parallel", "arbitrary")),
        name="na_attn" if bounded else "na_attn_rowmax",
    )(qkv, qkv, qkv, table)


def _dil_fixed_shift_kernel(q_ref, k_ref, v_ref, shift_ref, o_ref, den_ref, *, tq, halo):
    j = pl.program_id(2)
    length = k_ref.shape[0]
    win = DIL_SUB + 2 * halo
    lane_h = lax.broadcasted_iota(jnp.int32, (1, SEG), 1) >> int(math.log2(HEAD_DIM))
    for u in range(tq // DIL_SUB):
        q0 = j * tq + u * DIL_SUB
        start = pl.multiple_of(jnp.clip(q0 - halo, 0, length - win), halo)
        kw = k_ref[pl.ds(start, win), :]
        vw = v_ref[pl.ds(start, win), :]
        q = q_ref[u * DIL_SUB:(u + 1) * DIL_SUB, :]
        qm = jnp.concatenate([jnp.where(lane_h == hd, q, jnp.zeros_like(q))
                              for hd in range(DIL_HEADS)], axis=0)
        qpos = q0 + lax.broadcasted_iota(jnp.int32, (DIL_SUB, win), 0)
        kpos = start + lax.broadcasted_iota(jnp.int32, (DIL_SUB, win), 1)
        bias = jnp.where(jnp.abs(kpos - qpos) <= halo, -shift_ref[...], MASK_VALUE)
        s = _dot_nt(qm, kw)
        p = jnp.exp2(s + jnp.concatenate([bias] * DIL_HEADS, axis=0))
        den = jnp.sum(p, axis=1, keepdims=True)
        pv = jnp.dot(p.astype(BF16), vw, preferred_element_type=F32)
        num_u = jnp.zeros((DIL_SUB, SEG), F32)
        den_u = jnp.zeros((DIL_SUB, SEG), F32)
        for hd in range(DIL_HEADS):
            rows = slice(hd * DIL_SUB, (hd + 1) * DIL_SUB)
            num_u = jnp.where(lane_h == hd, pv[rows], num_u)
            den_u = jnp.where(lane_h == hd, den[rows], den_u)
        o_ref[u * DIL_SUB:(u + 1) * DIL_SUB, :] = num_u
        den_ref[u * DIL_SUB:(u + 1) * DIL_SUB, :] = den_u


def _dil_kernel(q_ref, k_ref, v_ref, shift_ref, o_ref, lse_ref, *, tq, halo):
    del shift_ref
    j = pl.program_id(2)
    length = k_ref.shape[0]
    win = tq + 2 * halo
    start = pl.multiple_of(jnp.clip(j * tq - halo, 0, length - win), halo)
    kw = k_ref[pl.ds(start, win), :]
    vw = v_ref[pl.ds(start, win), :]
    q = q_ref[...]
    qpos = j * tq + lax.broadcasted_iota(jnp.int32, (tq, win), 0)
    kpos = start + lax.broadcasted_iota(jnp.int32, (tq, win), 1)
    bias = jnp.where(jnp.abs(kpos - qpos) <= halo, 0.0, MASK_VALUE)
    lane_h = lax.broadcasted_iota(jnp.int32, (1, SEG), 1) >> int(math.log2(HEAD_DIM))
    out = jnp.zeros((tq, SEG), F32)
    lse = jnp.zeros((tq, SEG), F32)
    for hd in range(DIL_HEADS):
        qm = jnp.where(lane_h == hd, q, jnp.zeros_like(q))
        s = _dot_nt(qm, kw) + bias
        m = jnp.max(s, axis=1, keepdims=True)
        p = jnp.exp2(s - m)
        l = jnp.sum(p, axis=1, keepdims=True)
        o = jnp.dot(p.astype(BF16), vw, preferred_element_type=F32) * (1.0 / l)
        out = jnp.where(lane_h == hd, o, out)
        lse = jnp.where(lane_h == hd, m + jnp.log(l) * LOG2E, lse)
    o_ref[...] = out
    lse_ref[...] = lse


def _dilated(src, first_seg, halo, shift, *, bounded):
    length = src.shape[-2]
    tq = min(DIL_TQ_FIXED_SHIFT, length) if bounded else DIL_TQ_ROWMAX
    if src.ndim == 3:
        (B, length, _), dil = src.shape, 1
        in_specs = [pl.BlockSpec((None, tq, SEG), lambda b, r, j: (b, j, first_seg)),
                    pl.BlockSpec((None, length, SEG), lambda b, r, j: (b, 0, first_seg + 1)),
                    pl.BlockSpec((None, length, SEG), lambda b, r, j: (b, 0, first_seg + 2))]
    else:
        B, dil, length, _ = src.shape
        in_specs = [pl.BlockSpec((None, None, tq, SEG), lambda b, r, j: (b, r, j, first_seg)),
                    pl.BlockSpec((None, None, length, SEG),
                                 lambda b, r, j: (b, r, 0, first_seg + 1)),
                    pl.BlockSpec((None, None, length, SEG),
                                 lambda b, r, j: (b, r, 0, first_seg + 2))]
    oshape = jax.ShapeDtypeStruct((B, dil, length, SEG), F32)
    ospec = pl.BlockSpec((None, None, tq, SEG), lambda b, r, j: (b, r, j, 0))
    return pl.pallas_call(
        functools.partial(_dil_fixed_shift_kernel if bounded else _dil_kernel, tq=tq, halo=halo),
        out_shape=(oshape, oshape),
        grid=(B, dil, length // tq),
        in_specs=in_specs + [_resident(shift.shape)],
        out_specs=(ospec, ospec),
        compiler_params=_cparams(("parallel", "parallel", "arbitrary")),
        name=f"dil_attn_{dil}" + ("" if bounded else "_rowmax"),
    )(src, src, src, shift)


def _outproj_kernel(x_ref, oa_ref, ob_ref, oc_ref, d0_ref, d1_ref, d2_ref,
                    l0_ref, l1_ref, l2_ref, w_ref, fg_ref, fwg_ref, fwu_ref, fwd_ref,
                    o_ref, cat_ref, *row_scratch, bounded, chunks):
    row_scratch = list(row_scratch)

    def token_rows(ref):
        dil = ref.shape[0]
        if dil == 1:
            return ref[0]
        buf = row_scratch.pop()
        for r in range(dil):
            slab = ref[r]
            for hf in range(SEG // LANES):
                buf[hf, pl.ds(r, ref.shape[1], stride=dil), :] = slab[:, hf * LANES:(hf + 1) * LANES]
        return jnp.concatenate([buf[hf] for hf in range(SEG // LANES)], axis=1)

    l0, l1, l2 = token_rows(l0_ref), token_rows(l1_ref), token_rows(l2_ref)
    if bounded:
        od = ((token_rows(d0_ref) + token_rows(d1_ref) + token_rows(d2_ref))
              * (1.0 / (l0 + l1 + l2)))
    else:
        m = jnp.maximum(jnp.maximum(l0, l1), l2)
        e0, e1, e2 = jnp.exp2(l0 - m), jnp.exp2(l1 - m), jnp.exp2(l2 - m)
        od = ((e0 * token_rows(d0_ref) + e1 * token_rows(d1_ref) + e2 * token_rows(d2_ref))
              * (1.0 / (e0 + e1 + e2)))
    cat_ref[:, 0 * SEG:1 * SEG] = oa_ref[...]
    cat_ref[:, 1 * SEG:2 * SEG] = ob_ref[...]
    cat_ref[:, 2 * SEG:3 * SEG] = oc_ref[...]
    cat_ref[:, 3 * SEG:4 * SEG] = od.astype(BF16)
    mixed = x_ref[...] + jnp.dot(cat_ref[...], w_ref[...], preferred_element_type=F32)
    o_ref[...] = _ffn_tile(mixed, fg_ref, fwg_ref, fwu_ref, fwd_ref, chunks)


def _outproj_ffn(x, oa, ob, oc, dils, w_out, fgain, fwg, fwu, fwd, *, bounded, tm=512):
    B, S, D = x.shape
    F = fwg.shape[1]
    xspec = pl.BlockSpec((None, tm, D), lambda b, i: (b, i, 0))
    sspec = pl.BlockSpec((None, tm, SEG), lambda b, i: (b, i, 0))
    outs = [d[0] for d in dils]
    lses = [d[1] for d in dils]
    dspecs = [pl.BlockSpec((None, a.shape[1], tm // a.shape[1], SEG), lambda b, i: (b, 0, i, 0))
              for a in outs + lses]
    n_scratch = sum(a.shape[1] > 1 for a in outs + lses)
    return pl.pallas_call(
        functools.partial(_outproj_kernel, bounded=bounded, chunks=_ffn_chunks(F)),
        out_shape=jax.ShapeDtypeStruct(x.shape, F32),
        grid=(B, S // tm),
        in_specs=[xspec] + [sspec] * 3 + dspecs
        + [_resident(w_out.shape), _resident((1, D)), _resident((D, F)), _resident((D, F)),
           _resident((F, D))],
        out_specs=xspec,
        scratch_shapes=[pltpu.VMEM((tm, 4 * SEG), BF16)]
        + [pltpu.VMEM((SEG // LANES, tm, LANES), F32)] * n_scratch,
        compiler_params=_cparams(("parallel", "parallel")),
        name="outproj_ffn" if bounded else "outproj_lse_ffn",
    )(x, oa, ob, oc, *outs, *lses, w_out, fgain.reshape(1, D), fwg, fwu, fwd)


def _rope_cs(pos, dim, theta):
    inv_freq = 1.0 / (theta ** (jnp.arange(0, dim, 2, dtype=F32) / dim))
    ang = pos.astype(F32)[:, None] * inv_freq[None, :]
    return jnp.cos(ang), jnp.sin(ang)


def _rope_tables(S):
    pos = jnp.arange(S, dtype=jnp.int32)

    def expand(parts_c, parts_s, reps):
        return (jnp.tile(jnp.concatenate(parts_c, axis=1), (1, reps)),
                jnp.tile(jnp.concatenate(parts_s, axis=1), (1, reps)))

    c, s = _rope_cs(pos, DIFF_QK_DIM // ROPE_FRACTION, ROPE_THETA)
    rest = DIFF_QK_DIM - 2 * c.shape[1]
    df = expand([c, c, jnp.ones((S, rest), F32)], [-s, s, jnp.zeros((S, rest), F32)],
                SEG // DIFF_QK_DIM)
    c, s = _rope_cs(pos, HEAD_DIM // ROPE_FRACTION, ROPE_THETA)
    rest = HEAD_DIM - 2 * c.shape[1]
    dl = expand([c, c, jnp.ones((S, rest), F32)], [-s, s, jnp.zeros((S, rest), F32)],
                SEG // HEAD_DIM)
    cr, sr = _rope_cs(pos // GRID_W, HEAD_DIM // 2, AXIAL_THETA)
    cc, sc = _rope_cs(pos % GRID_W, HEAD_DIM // 2, AXIAL_THETA)
    gq = expand([cr, cr, cc, cc], [-sr, sr, -sc, sc], SEG // HEAD_DIM)
    return df + gq + dl


def _group_matrices():
    lane = np.arange(SEG)
    mats = [(lane[:, None] // g == lane[None, :] // g).astype(np.float32) / g
            for g in (HEAD_DIM, DIFF_QK_DIM)]
    return jnp.asarray(np.stack(mats), BF16)


def _extend_w_in(w_in):
    kv0 = 7 * SEG
    hd = HEAD_DIM
    cols = np.concatenate([
        np.arange(0, kv0),
        kv0 + np.array([0, 0, 1, 1]).repeat(hd) * hd + np.tile(np.arange(hd), 4),
        kv0 + 2 * hd + np.array([0, 0, 1, 1]).repeat(hd) * hd + np.tile(np.arange(hd), 4),
        np.arange(kv0 + 4 * hd, w_in.shape[1]),
    ])
    return w_in[:, cols].astype(BF16)


def _gain_rows(na_q, na_k, df_q, df_k, gq_q, gq_k, dl_q, dl_k):
    def row(g, scale):
        return jnp.tile(g.astype(F32), SEG // g.shape[0]) * scale
    sc64 = HEAD_DIM ** -0.5 * LOG2E
    sc32 = DIFF_QK_DIM ** -0.5 * LOG2E
    return jnp.stack([row(na_q, sc64), row(na_k, 1.0), row(df_q, sc32), row(df_k, 1.0),
                      row(gq_q, sc64), row(gq_k, 1.0), row(dl_q, sc64), row(dl_k, 1.0)])


def kernel(x, ffn1_norm, ffn1_w_gate, ffn1_w_up, ffn1_w_down, mix_norm, w_in, w_out, na_q_norm, na_k_norm, na_rel_bias, diff_q_norm, diff_k_norm, diff_lambda_q1, diff_lambda_k1, diff_lambda_q2, diff_lambda_k2, diff_out_norm, gqa_q_norm, gqa_k_norm, dil_q_norm, dil_k_norm, ffn2_norm, ffn2_w_gate, ffn2_w_up, ffn2_w_down):
    B, S, D = x.shape
    depth = w_in.shape[0]
    rows = S // GRID_W
    tables = _rope_tables(S)
    gmats = _group_matrices()
    ones_row = jnp.ones((1, SEG), F32)
    zero_lamp = jnp.zeros((4, DIFF_QK_DIM), F32)
    zero_li = jnp.zeros((1, 1), F32)

    for l in range(depth):
        x = _ffn(x, ffn1_norm[l], ffn1_w_gate[l].astype(BF16), ffn1_w_up[l].astype(BF16),
                 ffn1_w_down[l].astype(BF16))

        gains = _gain_rows(na_q_norm[l], na_k_norm[l], diff_q_norm[l], diff_k_norm[l],
                           gqa_q_norm[l], gqa_k_norm[l], dil_q_norm[l], dil_k_norm[l])
        qkv, vt_df, vt_gq, *dl_classes = _inproj(x, mix_norm[l], _extend_w_in(w_in[l]), gmats,
                                                 gains, tables)

        na_bias = na_rel_bias[l]
        lamp = jnp.stack([diff_lambda_q1[l], diff_lambda_k1[l], diff_lambda_q2[l],
                          diff_lambda_k2[l]]).astype(F32)
        li = jnp.full((1, 1), 0.8 - 0.6 * math.exp(-0.3 * l), F32)
        ogain = jnp.tile(diff_out_norm[l].astype(F32), SEG // HEAD_DIM).reshape(1, SEG)
        dl_src = {1: (qkv, SEG_DL_Q)}
        dl_src.update({d: (a, 0) for d, a in zip(DILATIONS, dl_classes)})
        w_out_l = w_out[l].astype(BF16)
        ffn2 = (ffn2_norm[l], ffn2_w_gate[l].astype(BF16), ffn2_w_up[l].astype(BF16),
                ffn2_w_down[l].astype(BF16))

        def score_bound(qi, width):
            return width * jnp.max(jnp.abs(gains[qi])) * jnp.max(jnp.abs(gains[qi + 1]))
        bounds = {"na": score_bound(0, HEAD_DIM) + LOG2E * jnp.max(jnp.abs(na_rel_bias[l])),
                  "df": score_bound(2, DIFF_QK_DIM), "gq": score_bound(4, HEAD_DIM),
                  "dl": score_bound(6, HEAD_DIM)}
        shifts = {k: v.reshape(1, 1).astype(F32) for k, v in bounds.items()}

        def mix(bounded, x=x, qkv=qkv, vt_df=vt_df, vt_gq=vt_gq, dl_src=dl_src, shifts=shifts,
                na_bias=na_bias, ogain=ogain, lamp=lamp, li=li, w_out_l=w_out_l, ffn2=ffn2):
            na_table = _na_table(na_bias, rows, shifts["na"][0, 0] if bounded else None)
            o_a = _na(qkv, na_table, bounded=bounded)
            o_b = _flash(qkv, vt_df, (SEG_DF_Q, SEG_DF_K), gmats[0], ogain, lamp, li,
                         shifts["df"], groups=2 * DIFF_HEADS, diff=True, bounded=bounded)
            o_c = _flash(qkv, vt_gq, (SEG_GQ_Q, SEG_GQ_K), gmats[0], ones_row, zero_lamp,
                         zero_li, shifts["gq"], groups=GQA_Q_HEADS, diff=False, bounded=bounded)
            dils = [_dilated(*dl_src[dil], (window // 2) // dil, shifts["dl"], bounded=bounded)
                    for window, dil in DIL_PAIRS]
            return _outproj_ffn(x, o_a, o_b, o_c, dils, w_out_l, *ffn2, bounded=bounded)

        safe = functools.reduce(jnp.maximum, bounds.values()) <= MAX_FIXED_SHIFT
        x = lax.cond(safe, functools.partial(mix, True), functools.partial(mix, False))
    return x
```

```python
import functools
import math

import numpy as np
import jax
import jax.numpy as jnp
from jax import lax
from jax.experimental import pallas as pl
from jax.experimental.pallas import tpu as pltpu

F32 = jnp.float32
BF16 = jnp.bfloat16

GRID_W = 64
HEAD_DIM = 64
RMS_EPS = 1e-6
MASK_VALUE = -1e30
LOG2E = math.log2(math.e)

NA_HEADS = 4
NA_WIN_H = 8
NA_WIN_W = 16
DIFF_HEADS = 4
DIFF_QK_DIM = HEAD_DIM // 2
GQA_Q_HEADS = 4
GQA_KV_HEADS = 2
AXIAL_THETA = 10000.0
DIL_HEADS = 4
DIL_PAIRS = ((128, 1), (512, 4), (2048, 16))
DILATIONS = tuple(d for _, d in DIL_PAIRS if d > 1)
ROPE_THETA = 500000.0
ROPE_FRACTION = 4

LANES = 128
SEG = 256
(SEG_NA_Q, SEG_NA_K, SEG_NA_V, SEG_DF_Q, SEG_DF_K, SEG_DF_V,
 SEG_GQ_Q, SEG_GQ_K, SEG_GQ_V, SEG_DL_Q, SEG_DL_K, SEG_DL_V) = range(12)
N_SEG = 12
EXT_WIDTH = N_SEG * SEG

VMEM_LIMIT = 52 * 1024 * 1024

VT_ROWS = HEAD_DIM + 16
COL_BLOCK = 256
DIL_SUB = 128
DIL_TQ_FIXED_SHIFT = 1024
DIL_TQ_ROWMAX = 256
QK_AHEAD = 4
KV_UNROLL = 4
FLASH_COLS = 4096
MAX_FIXED_SHIFT = 40.0

NA_QROWS = 8
NA_KROWS = 16


def _cparams(sem):
    return pltpu.CompilerParams(dimension_semantics=sem, vmem_limit_bytes=VMEM_LIMIT)


def _resident(shape):
    nd = len(shape)
    return pl.BlockSpec(shape, lambda *_: (0,) * nd, pipeline_mode=pl.Buffered(1))


def _rms_rows(x, gain):
    ms = jnp.mean(x * x, axis=-1, keepdims=True)
    return x * lax.rsqrt(ms + RMS_EPS) * gain


def _group_mean_sq(y, gmat):
    sq = y * y
    hi = sq.astype(BF16)
    lo = (sq - hi.astype(F32)).astype(BF16)
    return (jnp.dot(hi, gmat, preferred_element_type=F32)
            + jnp.dot(lo, gmat, preferred_element_type=F32))


def _dot_nt(a, b):
    return lax.dot_general(a, b, (((1,), (1,)), ((), ())), preferred_element_type=F32)


def _ffn_tile(x, g_ref, wg_ref, wu_ref, wd_ref, chunks):
    h = _rms_rows(x, g_ref[...]).astype(BF16)
    acc = None
    for c0, c1 in chunks:
        g = jnp.dot(h, wg_ref[:, c0:c1], preferred_element_type=F32)
        u = jnp.dot(h, wu_ref[:, c0:c1], preferred_element_type=F32)
        a = (g * (1.0 / (1.0 + jnp.exp(-g))) * u).astype(BF16)
        d = jnp.dot(a, wd_ref[c0:c1, :], preferred_element_type=F32)
        acc = d if acc is None else acc + d
    return x + 0.5 * acc


def _ffn_kernel(x_ref, g_ref, wg_ref, wu_ref, wd_ref, o_ref, *, chunks):
    o_ref[...] = _ffn_tile(x_ref[...], g_ref, wg_ref, wu_ref, wd_ref, chunks)


def _ffn_chunks(F, fc=512):
    return tuple((c, min(c + fc, F)) for c in range(0, F, fc))


def _ffn(x, gain, wg, wu, wd, *, tm=512):
    B, S, D = x.shape
    F = wg.shape[1]
    chunks = _ffn_chunks(F)
    xspec = pl.BlockSpec((None, tm, D), lambda b, i: (b, i, 0))
    return pl.pallas_call(
        functools.partial(_ffn_kernel, chunks=chunks),
        out_shape=jax.ShapeDtypeStruct(x.shape, F32),
        grid=(B, S // tm),
        in_specs=[xspec, _resident((1, D)), _resident((D, F)), _resident((D, F)),
                  _resident((F, D))],
        out_specs=xspec,
        compiler_params=_cparams(("parallel", "parallel")),
        name="ffn",
    )(x, gain.reshape(1, D), wg, wu, wd)


_SEG_PLAN = {
    SEG_NA_Q: (0, 0, None), SEG_NA_K: (1, 0, None),
    SEG_DF_Q: (2, 1, "df"), SEG_DF_K: (3, 1, "df"),
    SEG_GQ_Q: (4, 0, "gq"), SEG_GQ_K: (5, 0, "gq"),
    SEG_DL_Q: (6, 0, "dl"), SEG_DL_K: (7, 0, "dl"),
}
_ROPE_HALF = {"df": DIFF_QK_DIM // ROPE_FRACTION // 2, "gq": HEAD_DIM // 4,
              "dl": HEAD_DIM // ROPE_FRACTION // 2}


def _inproj_kernel(x_ref, gn_ref, w_ref, gmat_ref, gains_ref,
                   cdf_ref, sdf_ref, cgq_ref, sgq_ref, cdl_ref, sdl_ref,
                   o_ref, vtdf_ref, vtgq_ref, *dl_refs_and_scratch):
    *dl_refs, ys_ref = dl_refs_and_scratch
    tm = x_ref.shape[0]
    h = _rms_rows(x_ref[...], gn_ref[...]).astype(BF16)
    lane = lax.broadcasted_iota(jnp.int32, (1, SEG), 1)
    tables = {"df": (cdf_ref, sdf_ref), "gq": (cgq_ref, sgq_ref), "dl": (cdl_ref, sdl_ref)}
    vt_plan = {SEG_DF_V: (vtdf_ref, tuple(range(DIFF_HEADS))),
               SEG_GQ_V: (vtgq_ref, tuple(2 * j for j in range(GQA_KV_HEADS)))}
    for seg in range(N_SEG):
        y = jnp.dot(h, w_ref[:, seg * SEG:(seg + 1) * SEG], preferred_element_type=F32)
        if seg in vt_plan:
            vt_ref, lane_heads = vt_plan[seg]
            yt = y.T
            ones = jnp.ones((VT_ROWS - HEAD_DIM, yt.shape[1]), BF16)
            for j, lh in enumerate(lane_heads):
                vt_ref[j * VT_ROWS:j * VT_ROWS + HEAD_DIM, :] = (
                    yt[lh * HEAD_DIM:(lh + 1) * HEAD_DIM].astype(BF16))
                vt_ref[j * VT_ROWS + HEAD_DIM:(j + 1) * VT_ROWS, :] = ones
        plan = _SEG_PLAN.get(seg)
        if plan is not None:
            gi, mi, rope = plan
            ms = _group_mean_sq(y, gmat_ref[mi])
            y = y * lax.rsqrt(ms + RMS_EPS) * gains_ref[gi:gi + 1, :]
            if rope is not None:
                half = _ROPE_HALF[rope]
                c_ref, s_ref = tables[rope]
                first = (lane & (2 * half - 1)) < half
                partner = jnp.where(first, pltpu.roll(y, SEG - half, 1), pltpu.roll(y, half, 1))
                y = y * c_ref[...] + partner * s_ref[...]
        o_ref[:, seg * SEG:(seg + 1) * SEG] = y.astype(BF16)
        if SEG_DL_Q <= seg <= SEG_DL_V:
            for hf in range(SEG // LANES):
                ys_ref[hf] = y[:, hf * LANES:(hf + 1) * LANES]
            col = (seg - SEG_DL_Q) * SEG
            for dl_ref, dil in zip(dl_refs, DILATIONS):
                for r in range(dil):
                    for hf in range(SEG // LANES):
                        dl_ref[r, :, col + hf * LANES:col + (hf + 1) * LANES] = (
                            ys_ref[hf, pl.ds(r, tm // dil, stride=dil), :].astype(BF16))


def _inproj(x, gain, w_ext, gmats, gains, tables, *, tm=512):
    B, S, D = x.shape
    tspec = pl.BlockSpec((tm, SEG), lambda i, b: (i, 0))
    vt_rows = (DIFF_HEADS * VT_ROWS, GQA_KV_HEADS * VT_ROWS)
    return pl.pallas_call(
        _inproj_kernel,
        out_shape=(jax.ShapeDtypeStruct((B, S, EXT_WIDTH), BF16),)
        + tuple(jax.ShapeDtypeStruct((B, S // tm, r, tm), BF16) for r in vt_rows)
        + tuple(jax.ShapeDtypeStruct((B, d, S // d, 3 * SEG), BF16) for d in DILATIONS),
        grid=(S // tm, B),
        in_specs=[pl.BlockSpec((None, tm, D), lambda i, b: (b, i, 0)),
                  _resident((1, D)), _resident((D, EXT_WIDTH)),
                  _resident(gmats.shape), _resident(gains.shape)] + [tspec] * 6,
        out_specs=(pl.BlockSpec((None, tm, EXT_WIDTH), lambda i, b: (b, i, 0)),)
        + tuple(pl.BlockSpec((None, None, r, tm), lambda i, b: (b, i, 0, 0)) for r in vt_rows)
        + tuple(pl.BlockSpec((None, d, tm // d, 3 * SEG), lambda i, b: (b, 0, i, 0))
                for d in DILATIONS),
        scratch_shapes=[pltpu.VMEM((SEG // LANES, tm, LANES), F32)],
        compiler_params=_cparams(("parallel", "parallel")),
        name="inproj",
    )(x, gain.reshape(1, D), w_ext, gmats, gains, *tables)


def _flash_kernel(q_ref, k_ref, vt_ref, gmat_ref, ogain_ref, lamp_ref, li_ref, shift_ref, o_ref,
                  qmt_ref, m_ref, acc_ref, sbuf_ref, *, groups, tq, tk, diff, vheads, bounded):
    gw = SEG // groups
    qt = q_ref[...].astype(F32).T
    row_g = lax.broadcasted_iota(jnp.int32, (SEG, 1), 0) >> int(math.log2(gw))
    for r in range(groups):
        qmt_ref[:, r * tq:(r + 1) * tq] = jnp.where(row_g == r, qt, 0.0).astype(BF16)
    m_ref[...] = jnp.full(m_ref.shape, MASK_VALUE, F32)
    acc_ref[...] = jnp.zeros(acc_ref.shape, F32)
    ncol = groups * tq // COL_BLOCK

    nkv = k_ref.shape[0] // tk
    ahead = sbuf_ref.shape[0]

    def k_block(kv):
        return k_ref[pl.ds(pl.multiple_of(kv * tk, tk), tk), :]

    def qk(k, c):
        return jnp.dot(k, qmt_ref[:, c * COL_BLOCK:(c + 1) * COL_BLOCK],
                       preferred_element_type=F32)

    k0 = k_block(0)
    for c in range(ahead):
        sbuf_ref[c] = qk(k0, c)

    shift = shift_ref[...]

    def body(it, carry):
        pend = [sbuf_ref[c] for c in range(ahead)]
        for u in range(KV_UNROLL):
            kv = it * KV_UNROLL + u
            k = k_block(kv)
            k_next = k_block(jnp.minimum(kv + 1, nkv - 1))
            vt = vt_ref[kv]
            for c in range(ncol):
                hv = (c * COL_BLOCK // tq) * vheads // groups
                cs = slice(c * COL_BLOCK, (c + 1) * COL_BLOCK)
                s = pend.pop(0)
                if c + ahead < ncol:
                    pend.append(qk(k, c + ahead))
                elif u + 1 < KV_UNROLL:
                    pend.append(qk(k_next, c + ahead - ncol))
                else:
                    sbuf_ref[c + ahead - ncol] = qk(k_next, c + ahead - ncol)
                vth = vt[hv * VT_ROWS:(hv + 1) * VT_ROWS]
                if bounded:
                    p = jnp.exp2((s - shift).astype(BF16))
                    acc_ref[:, cs] = acc_ref[:, cs] + jnp.dot(vth, p, preferred_element_type=F32)
                else:
                    m_prev = m_ref[:, cs]
                    m_new = jnp.maximum(m_prev, jnp.max(s, axis=0, keepdims=True))
                    alpha = jnp.exp2(m_prev - m_new)
                    p = jnp.exp2(s - m_new).astype(BF16)
                    acc_ref[:, cs] = (acc_ref[:, cs] * alpha
                                      + jnp.dot(vth, p, preferred_element_type=F32))
                    m_ref[:, cs] = m_new
        return carry

    lax.fori_loop(0, nkv // KV_UNROLL, body, 0)

    acc = acc_ref[...]
    norm = acc[0:HEAD_DIM] * (1.0 / acc[HEAD_DIM:HEAD_DIM + 1])
    if diff:
        lp = lamp_ref[...]
        lam = (jnp.exp(jnp.sum(lp[0:1] * lp[1:2], axis=1, keepdims=True))
               - jnp.exp(jnp.sum(lp[2:3] * lp[3:4], axis=1, keepdims=True)) + li_ref[...])
        pieces = [norm[:, 2 * hd * tq:(2 * hd + 1) * tq]
                  - lam * norm[:, (2 * hd + 1) * tq:(2 * hd + 2) * tq] for hd in range(groups // 2)]
    else:
        pieces = [norm[:, r * tq:(r + 1) * tq] for r in range(groups)]
    out = jnp.concatenate(pieces, axis=0).T
    if diff:
        ms = _group_mean_sq(out, gmat_ref[...])
        out = out * lax.rsqrt(ms + RMS_EPS) * ogain_ref[...] * (1.0 - li_ref[...])
    o_ref[...] = out.astype(BF16)


def _flash(qkv, vt, segs, gmat, ogain, lamp, li, shift, *, groups, diff, bounded,
           cols=FLASH_COLS):
    B, S, _ = qkv.shape
    _, nkv, vrows, tk = vt.shape
    assert nkv % KV_UNROLL == 0 and S % (cols // groups) == 0
    tq = cols // groups
    qs, ks = segs
    return pl.pallas_call(
        functools.partial(_flash_kernel, groups=groups, tq=tq, tk=tk, diff=diff,
                          vheads=vrows // VT_ROWS, bounded=bounded),
        out_shape=jax.ShapeDtypeStruct((B, S, SEG), BF16),
        grid=(B, S // tq),
        in_specs=[pl.BlockSpec((None, tq, SEG), lambda b, i: (b, i, qs)),
                  pl.BlockSpec((None, S, SEG), lambda b, i: (b, 0, ks)),
                  pl.BlockSpec((None, nkv, vrows, tk), lambda b, i: (b, 0, 0, 0)),
                  _resident(gmat.shape), _resident(ogain.shape), _resident(lamp.shape),
                  _resident(li.shape), _resident(shift.shape)],
        out_specs=pl.BlockSpec((None, tq, SEG), lambda b, i: (b, i, 0)),
        scratch_shapes=[pltpu.VMEM((SEG, cols), BF16), pltpu.VMEM((1, cols), F32),
                        pltpu.VMEM((VT_ROWS, cols), F32),
                        pltpu.VMEM((QK_AHEAD, tk, COL_BLOCK), F32)],
        compiler_params=_cparams(("parallel", "parallel")),
        name=("diff_attn" if diff else "gqa_attn") + ("" if bounded else "_runmax"),
    )(qkv, qkv, vt, gmat, ogain, lamp, li, shift)


def _na_kernel(q_ref, k_ref, v_ref, tab_ref, o_ref, *, nq, nk, bounded):
    i = pl.program_id(2)
    seq = k_ref.shape[0]
    start = jnp.clip(i * nq - (nk - nq) // 2, 0, seq - nk)
    start = pl.multiple_of(start, (nk - nq) // 2)
    kw = k_ref[pl.ds(start, nk), :]
    vw = v_ref[pl.ds(start, nk), :]
    q = q_ref[...]
    if bounded:
        qt = q.astype(F32).T
        vt = vw.astype(F32).T
        row_h = (lax.broadcasted_iota(jnp.int32, (2 * HEAD_DIM, 1), 0)
                 >> int(math.log2(HEAD_DIM)))
        ones = jnp.ones((VT_ROWS - HEAD_DIM, nk), BF16)
        outs = []
        scores = [jnp.dot(kw, jnp.where(row_h == j, qt, 0.0).astype(BF16),
                          preferred_element_type=F32) for j in range(2)]
        for j in range(2):
            p = jnp.exp2(scores[j] + tab_ref[j]).astype(BF16)
            vth = jnp.concatenate(
                [vt[j * HEAD_DIM:(j + 1) * HEAD_DIM].astype(BF16), ones], axis=0)
            pv = jnp.dot(vth, p, preferred_element_type=F32)
            outs.append(pv[0:HEAD_DIM] * (1.0 / pv[HEAD_DIM:HEAD_DIM + 1]))
        o_ref[...] = jnp.concatenate(outs, axis=0).T.astype(BF16)
        return
    lane_h = lax.broadcasted_iota(jnp.int32, (1, 2 * HEAD_DIM), 1) >> int(math.log2(HEAD_DIM))
    outs = []
    for j in range(2):
        qm = jnp.where(lane_h == j, q, jnp.zeros_like(q))
        s = _dot_nt(qm, kw) + tab_ref[j]
        m = jnp.max(s, axis=1, keepdims=True)
        p = jnp.exp2(s - m)
        l = jnp.sum(p, axis=1, keepdims=True)
        outs.append(jnp.dot(p.astype(BF16), vw, preferred_element_type=F32) * (1.0 / l))
    o_ref[...] = jnp.where(lane_h == 0, outs[0], outs[1]).astype(BF16)


def _na_table(rel_bias, rows, shift=None):
    kh = min(NA_WIN_H, rows)
    n_ro, n_co = 2 * NA_WIN_H - 1, 2 * NA_WIN_W - 1
    c = np.arange(GRID_W)[:, None]
    kc = np.arange(GRID_W)[None, :]
    cs = np.clip(c - NA_WIN_W // 2, 0, GRID_W - NA_WIN_W)
    col_valid = (kc >= cs) & (kc < cs + NA_WIN_W)
    col_sel = (kc - c + (NA_WIN_W - 1))[..., None] == np.arange(n_co)
    row_sel, row_valid, row_off = [], [], []
    for r0 in (0, NA_QROWS, rows - NA_QROWS):
        kr0 = int(np.clip(r0 - (NA_KROWS - NA_QROWS) // 2, 0, rows - NA_KROWS))
        r = r0 + np.arange(NA_QROWS)[:, None]
        kr = kr0 + np.arange(NA_KROWS)[None, :]
        rs = np.clip(r - kh // 2, 0, rows - kh)
        row_valid.append((kr >= rs) & (kr < rs + kh))
        row_off.append(kr - r + (NA_WIN_H - 1))
        row_sel.append(row_off[-1][..., None] == np.arange(n_ro))
    row_sel = jnp.asarray(np.stack(row_sel), F32)
    valid = (np.stack(row_valid)[:, None, :, None, :, None]
             & col_valid[None, None, None, :, None, :])
    hi = lax.Precision.HIGHEST
    cols = jnp.einsum("hab,cdb->hacd", rel_bias.astype(F32), jnp.asarray(col_sel, F32),
                      precision=hi)
    nq, nk = NA_QROWS * GRID_W, NA_KROWS * GRID_W
    if shift is None:
        full = jnp.einsum("vqka,hacd->vhqckd", row_sel, cols, precision=hi)
        table = jnp.where(valid, full * LOG2E, MASK_VALUE)
        return table.reshape(3, rel_bias.shape[0], nq, nk)
    blocks = jnp.where(col_valid[None, None], cols * LOG2E - shift, MASK_VALUE)
    blocks = jnp.concatenate(
        [blocks.transpose(0, 1, 3, 2),
         jnp.full((rel_bias.shape[0], 1, GRID_W, GRID_W), MASK_VALUE, F32)], axis=1)
    block_id = np.where(np.stack(row_valid), np.stack(row_off), n_ro).transpose(0, 2, 1)
    table = jnp.take(blocks, jnp.asarray(block_id), axis=1)
    return table.transpose(1, 0, 2, 4, 3, 5).reshape(3, rel_bias.shape[0], nk, nq)


def _na(qkv, table, *, bounded):
    B, S, _ = qkv.shape
    nq, nk = NA_QROWS * GRID_W, NA_KROWS * GRID_W
    nblk = S // nq
    hp = 2 * HEAD_DIM

    def variant(i):
        return jnp.where(i == 0, 0, jnp.where(i == nblk - 1, 2, 1))

    return pl.pallas_call(
        functools.partial(_na_kernel, nq=nq, nk=nk, bounded=bounded),
        out_shape=jax.ShapeDtypeStruct((B, S, SEG), BF16),
        grid=(2, B, nblk),
        in_specs=[pl.BlockSpec((None, nq, hp), lambda p, b, i: (b, i, 2 * SEG_NA_Q + p)),
                  pl.BlockSpec((None, S, hp), lambda p, b, i: (b, 0, 2 * SEG_NA_K + p)),
                  pl.BlockSpec((None, S, hp), lambda p, b, i: (b, 0, 2 * SEG_NA_V + p)),
                  pl.BlockSpec((None, 2) + table.shape[2:], lambda p, b, i: (variant(i), p, 0, 0))],
        out_specs=pl.BlockSpec((None, nq, hp), lambda p, b, i: (b, i, p)),
        compiler_params=_cparams(("parallel", "parallel", "arbitrary")),
        name="na_attn" if bounded else "na_attn_rowmax",
    )(qkv, qkv, qkv, table)


def _dil_fixed_shift_kernel(q_ref, k_ref, v_ref, shift_ref, o_ref, den_ref, *, tq, halo):
    j = pl.program_id(2)
    length = k_ref.shape[0]
    win = DIL_SUB + 2 * halo
    lane_h = lax.broadcasted_iota(jnp.int32, (1, SEG), 1) >> int(math.log2(HEAD_DIM))
    for u in range(tq // DIL_SUB):
        q0 = j * tq + u * DIL_SUB
        start = pl.multiple_of(jnp.clip(q0 - halo, 0, length - win), halo)
        kw = k_ref[pl.ds(start, win), :]
        vw = v_ref[pl.ds(start, win), :]
        q = q_ref[u * DIL_SUB:(u + 1) * DIL_SUB, :]
        qm = jnp.concatenate([jnp.where(lane_h == hd, q, jnp.zeros_like(q))
                              for hd in range(DIL_HEADS)], axis=0)
        qpos = q0 + lax.broadcasted_iota(jnp.int32, (DIL_SUB, win), 0)
        kpos = start + lax.broadcasted_iota(jnp.int32, (DIL_SUB, win), 1)
        bias = jnp.where(jnp.abs(kpos - qpos) <= halo, -shift_ref[...], MASK_VALUE)
        s = _dot_nt(qm, kw)
        p = jnp.exp2(s + jnp.concatenate([bias] * DIL_HEADS, axis=0))
        den = jnp.sum(p, axis=1, keepdims=True)
        pv = jnp.dot(p.astype(BF16), vw, preferred_element_type=F32)
        num_u = jnp.zeros((DIL_SUB, SEG), F32)
        den_u = jnp.zeros((DIL_SUB, SEG), F32)
        for hd in range(DIL_HEADS):
            rows = slice(hd * DIL_SUB, (hd + 1) * DIL_SUB)
            num_u = jnp.where(lane_h == hd, pv[rows], num_u)
            den_u = jnp.where(lane_h == hd, den[rows], den_u)
        o_ref[u * DIL_SUB:(u + 1) * DIL_SUB, :] = num_u
        den_ref[u * DIL_SUB:(u + 1) * DIL_SUB, :] = den_u


def _dil_kernel(q_ref, k_ref, v_ref, shift_ref, o_ref, lse_ref, *, tq, halo):
    del shift_ref
    j = pl.program_id(2)
    length = k_ref.shape[0]
    win = tq + 2 * halo
    start = pl.multiple_of(jnp.clip(j * tq - halo, 0, length - win), halo)
    kw = k_ref[pl.ds(start, win), :]
    vw = v_ref[pl.ds(start, win), :]
    q = q_ref[...]
    qpos = j * tq + lax.broadcasted_iota(jnp.int32, (tq, win), 0)
    kpos = start + lax.broadcasted_iota(jnp.int32, (tq, win), 1)
    bias = jnp.where(jnp.abs(kpos - qpos) <= halo, 0.0, MASK_VALUE)
    lane_h = lax.broadcasted_iota(jnp.int32, (1, SEG), 1) >> int(math.log2(HEAD_DIM))
    out = jnp.zeros((tq, SEG), F32)
    lse = jnp.zeros((tq, SEG), F32)
    for hd in range(DIL_HEADS):
        qm = jnp.where(lane_h == hd, q, jnp.zeros_like(q))
        s = _dot_nt(qm, kw) + bias
        m = jnp.max(s, axis=1, keepdims=True)
        p = jnp.exp2(s - m)
        l = jnp.sum(p, axis=1, keepdims=True)
        o = jnp.dot(p.astype(BF16), vw, preferred_element_type=F32) * (1.0 / l)
        out = jnp.where(lane_h == hd, o, out)
        lse = jnp.where(lane_h == hd, m + jnp.log(l) * LOG2E, lse)
    o_ref[...] = out
    lse_ref[...] = lse


def _dilated(src, first_seg, halo, shift, *, bounded):
    length = src.shape[-2]
    tq = min(DIL_TQ_FIXED_SHIFT, length) if bounded else DIL_TQ_ROWMAX
    if src.ndim == 3:
        (B, length, _), dil = src.shape, 1
        in_specs = [pl.BlockSpec((None, tq, SEG), lambda b, r, j: (b, j, first_seg)),
                    pl.BlockSpec((None, length, SEG), lambda b, r, j: (b, 0, first_seg + 1)),
                    pl.BlockSpec((None, length, SEG), lambda b, r, j: (b, 0, first_seg + 2))]
    else:
        B, dil, length, _ = src.shape
        in_specs = [pl.BlockSpec((None, None, tq, SEG), lambda b, r, j: (b, r, j, first_seg)),
                    pl.BlockSpec((None, None, length, SEG),
                                 lambda b, r, j: (b, r, 0, first_seg + 1)),
                    pl.BlockSpec((None, None, length, SEG),
                                 lambda b, r, j: (b, r, 0, first_seg + 2))]
    oshape = jax.ShapeDtypeStruct((B, dil, length, SEG), F32)
    ospec = pl.BlockSpec((None, None, tq, SEG), lambda b, r, j: (b, r, j, 0))
    return pl.pallas_call(
        functools.partial(_dil_fixed_shift_kernel if bounded else _dil_kernel, tq=tq, halo=halo),
        out_shape=(oshape, oshape),
        grid=(B, dil, length // tq),
        in_specs=in_specs + [_resident(shift.shape)],
        out_specs=(ospec, ospec),
        compiler_params=_cparams(("parallel", "parallel", "arbitrary")),
        name=f"dil_attn_{dil}" + ("" if bounded else "_rowmax"),
    )(src, src, src, shift)


def _outproj_kernel(x_ref, oa_ref, ob_ref, oc_ref, d0_ref, d1_ref, d2_ref,
                    l0_ref, l1_ref, l2_ref, w_ref, fg_ref, fwg_ref, fwu_ref, fwd_ref,
                    o_ref, cat_ref, *row_scratch, bounded, chunks):
    row_scratch = list(row_scratch)

    def token_rows(ref):
        dil = ref.shape[0]
        if dil == 1:
            return ref[0]
        buf = row_scratch.pop()
        for r in range(dil):
            slab = ref[r]
            for hf in range(SEG // LANES):
                buf[hf, pl.ds(r, ref.shape[1], stride=dil), :] = slab[:, hf * LANES:(hf + 1) * LANES]
        return jnp.concatenate([buf[hf] for hf in range(SEG // LANES)], axis=1)

    l0, l1, l2 = token_rows(l0_ref), token_rows(l1_ref), token_rows(l2_ref)
    if bounded:
        od = ((token_rows(d0_ref) + token_rows(d1_ref) + token_rows(d2_ref))
              * (1.0 / (l0 + l1 + l2)))
    else:
        m = jnp.maximum(jnp.maximum(l0, l1), l2)
        e0, e1, e2 = jnp.exp2(l0 - m), jnp.exp2(l1 - m), jnp.exp2(l2 - m)
        od = ((e0 * token_rows(d0_ref) + e1 * token_rows(d1_ref) + e2 * token_rows(d2_ref))
              * (1.0 / (e0 + e1 + e2)))
    cat_ref[:, 0 * SEG:1 * SEG] = oa_ref[...]
    cat_ref[:, 1 * SEG:2 * SEG] = ob_ref[...]
    cat_ref[:, 2 * SEG:3 * SEG] = oc_ref[...]
    cat_ref[:, 3 * SEG:4 * SEG] = od.astype(BF16)
    mixed = x_ref[...] + jnp.dot(cat_ref[...], w_ref[...], preferred_element_type=F32)
    o_ref[...] = _ffn_tile(mixed, fg_ref, fwg_ref, fwu_ref, fwd_ref, chunks)


def _outproj_ffn(x, oa, ob, oc, dils, w_out, fgain, fwg, fwu, fwd, *, bounded, tm=512):
    B, S, D = x.shape
    F = fwg.shape[1]
    xspec = pl.BlockSpec((None, tm, D), lambda b, i: (b, i, 0))
    sspec = pl.BlockSpec((None, tm, SEG), lambda b, i: (b, i, 0))
    outs = [d[0] for d in dils]
    lses = [d[1] for d in dils]
    dspecs = [pl.BlockSpec((None, a.shape[1], tm // a.shape[1], SEG), lambda b, i: (b, 0, i, 0))
              for a in outs + lses]
    n_scratch = sum(a.shape[1] > 1 for a in outs + lses)
    return pl.pallas_call(
        functools.partial(_outproj_kernel, bounded=bounded, chunks=_ffn_chunks(F)),
        out_shape=jax.ShapeDtypeStruct(x.shape, F32),
        grid=(B, S // tm),
        in_specs=[xspec] + [sspec] * 3 + dspecs
        + [_resident(w_out.shape), _resident((1, D)), _resident((D, F)), _resident((D, F)),
           _resident((F, D))],
        out_specs=xspec,
        scratch_shapes=[pltpu.VMEM((tm, 4 * SEG), BF16)]
        + [pltpu.VMEM((SEG // LANES, tm, LANES), F32)] * n_scratch,
        compiler_params=_cparams(("parallel", "parallel")),
        name="outproj_ffn" if bounded else "outproj_lse_ffn",
    )(x, oa, ob, oc, *outs, *lses, w_out, fgain.reshape(1, D), fwg, fwu, fwd)


def _rope_cs(pos, dim, theta):
    inv_freq = 1.0 / (theta ** (jnp.arange(0, dim, 2, dtype=F32) / dim))
    ang = pos.astype(F32)[:, None] * inv_freq[None, :]
    return jnp.cos(ang), jnp.sin(ang)


def _rope_tables(S):
    pos = jnp.arange(S, dtype=jnp.int32)

    def expand(parts_c, parts_s, reps):
        return (jnp.tile(jnp.concatenate(parts_c, axis=1), (1, reps)),
                jnp.tile(jnp.concatenate(parts_s, axis=1), (1, reps)))

    c, s = _rope_cs(pos, DIFF_QK_DIM // ROPE_FRACTION, ROPE_THETA)
    rest = DIFF_QK_DIM - 2 * c.shape[1]
    df = expand([c, c, jnp.ones((S, rest), F32)], [-s, s, jnp.zeros((S, rest), F32)],
                SEG // DIFF_QK_DIM)
    c, s = _rope_cs(pos, HEAD_DIM // ROPE_FRACTION, ROPE_THETA)
    rest = HEAD_DIM - 2 * c.shape[1]
    dl = expand([c, c, jnp.ones((S, rest), F32)], [-s, s, jnp.zeros((S, rest), F32)],
                SEG // HEAD_DIM)
    cr, sr = _rope_cs(pos // GRID_W, HEAD_DIM // 2, AXIAL_THETA)
    cc, sc = _rope_cs(pos % GRID_W, HEAD_DIM // 2, AXIAL_THETA)
    gq = expand([cr, cr, cc, cc], [-sr, sr, -sc, sc], SEG // HEAD_DIM)
    return df + gq + dl


def _group_matrices():
    lane = np.arange(SEG)
    mats = [(lane[:, None] // g == lane[None, :] // g).astype(np.float32) / g
            for g in (HEAD_DIM, DIFF_QK_DIM)]
    return jnp.asarray(np.stack(mats), BF16)


def _extend_w_in(w_in):
    kv0 = 7 * SEG
    hd = HEAD_DIM
    cols = np.concatenate([
        np.arange(0, kv0),
        kv0 + np.array([0, 0, 1, 1]).repeat(hd) * hd + np.tile(np.arange(hd), 4),
        kv0 + 2 * hd + np.array([0, 0, 1, 1]).repeat(hd) * hd + np.tile(np.arange(hd), 4),
        np.arange(kv0 + 4 * hd, w_in.shape[1]),
    ])
    return w_in[:, cols].astype(BF16)


def _gain_rows(na_q, na_k, df_q, df_k, gq_q, gq_k, dl_q, dl_k):
    def row(g, scale):
        return jnp.tile(g.astype(F32), SEG // g.shape[0]) * scale
    sc64 = HEAD_DIM ** -0.5 * LOG2E
    sc32 = DIFF_QK_DIM ** -0.5 * LOG2E
    return jnp.stack([row(na_q, sc64), row(na_k, 1.0), row(df_q, sc32), row(df_k, 1.0),
                      row(gq_q, sc64), row(gq_k, 1.0), row(dl_q, sc64), row(dl_k, 1.0)])


def kernel(x, ffn1_norm, ffn1_w_gate, ffn1_w_up, ffn1_w_down, mix_norm, w_in, w_out, na_q_norm, na_k_norm, na_rel_bias, diff_q_norm, diff_k_norm, diff_lambda_q1, diff_lambda_k1, diff_lambda_q2, diff_lambda_k2, diff_out_norm, gqa_q_norm, gqa_k_norm, dil_q_norm, dil_k_norm, ffn2_norm, ffn2_w_gate, ffn2_w_up, ffn2_w_down):
    B, S, D = x.shape
    depth = w_in.shape[0]
    rows = S // GRID_W
    tables = _rope_tables(S)
    gmats = _group_matrices()
    ones_row = jnp.ones((1, SEG), F32)
    zero_lamp = jnp.zeros((4, DIFF_QK_DIM), F32)
    zero_li = jnp.zeros((1, 1), F32)

    for l in range(depth):
        x = _ffn(x, ffn1_norm[l], ffn1_w_gate[l].astype(BF16), ffn1_w_up[l].astype(BF16),
                 ffn1_w_down[l].astype(BF16))

        gains = _gain_rows(na_q_norm[l], na_k_norm[l], diff_q_norm[l], diff_k_norm[l],
                           gqa_q_norm[l], gqa_k_norm[l], dil_q_norm[l], dil_k_norm[l])
        qkv, vt_df, vt_gq, *dl_classes = _inproj(x, mix_norm[l], _extend_w_in(w_in[l]), gmats,
                                                 gains, tables)

        na_bias = na_rel_bias[l]
        lamp = jnp.stack([diff_lambda_q1[l], diff_lambda_k1[l], diff_lambda_q2[l],
                          diff_lambda_k2[l]]).astype(F32)
        li = jnp.full((1, 1), 0.8 - 0.6 * math.exp(-0.3 * l), F32)
        ogain = jnp.tile(diff_out_norm[l].astype(F32), SEG // HEAD_DIM).reshape(1, SEG)
        dl_src = {1: (qkv, SEG_DL_Q)}
        dl_src.update({d: (a, 0) for d, a in zip(DILATIONS, dl_classes)})
        w_out_l = w_out[l].astype(BF16)
        ffn2 = (ffn2_norm[l], ffn2_w_gate[l].astype(BF16), ffn2_w_up[l].astype(BF16),
                ffn2_w_down[l].astype(BF16))

        def score_bound(qi, width):
            return width * jnp.max(jnp.abs(gains[qi])) * jnp.max(jnp.abs(gains[qi + 1]))
        bounds = {"na": score_bound(0, HEAD_DIM) + LOG2E * jnp.max(jnp.abs(na_rel_bias[l])),
                  "df": score_bound(2, DIFF_QK_DIM), "gq": score_bound(4, HEAD_DIM),
                  "dl": score_bound(6, HEAD_DIM)}
        shifts = {k: v.reshape(1, 1).astype(F32) for k, v in bounds.items()}

        def mix(bounded, x=x, qkv=qkv, vt_df=vt_df, vt_gq=vt_gq, dl_src=dl_src, shifts=shifts,
                na_bias=na_bias, ogain=ogain, lamp=lamp, li=li, w_out_l=w_out_l, ffn2=ffn2):
            na_table = _na_table(na_bias, rows, shifts["na"][0, 0] if bounded else None)
            o_a = _na(qkv, na_table, bounded=bounded)
            o_b = _flash(qkv, vt_df, (SEG_DF_Q, SEG_DF_K), gmats[0], ogain, lamp, li,
                         shifts["df"], groups=2 * DIFF_HEADS, diff=True, bounded=bounded)
            o_c = _flash(qkv, vt_gq, (SEG_GQ_Q, SEG_GQ_K), gmats[0], ones_row, zero_lamp,
                         zero_li, shifts["gq"], groups=GQA_Q_HEADS, diff=False, bounded=bounded)
            dils = [_dilated(*dl_src[dil], (window // 2) // dil, shifts["dl"], bounded=bounded)
                    for window, dil in DIL_PAIRS]
            return _outproj_ffn(x, o_a, o_b, o_c, dils, w_out_l, *ffn2, bounded=bounded)

        safe = functools.reduce(jnp.maximum, bounds.values()) <= MAX_FIXED_SHIFT
        x = lax.cond(safe, functools.partial(mix, True), functools.partial(mix, False))
    return x
```

```python
import functools
import math

import numpy as np
import jax
import jax.numpy as jnp
from jax import lax
from jax.experimental import pallas as pl
from jax.experimental.pallas import tpu as pltpu

F32 = jnp.float32
BF16 = jnp.bfloat16

GRID_W = 64
HEAD_DIM = 64
RMS_EPS = 1e-6
MASK_VALUE = -1e30
LOG2E = math.log2(math.e)

NA_HEADS = 4
NA_WIN_H = 8
NA_WIN_W = 16
DIFF_HEADS = 4
DIFF_QK_DIM = HEAD_DIM // 2
GQA_Q_HEADS = 4
GQA_KV_HEADS = 2
AXIAL_THETA = 10000.0
DIL_HEADS = 4
DIL_PAIRS = ((128, 1), (512, 4), (2048, 16))
DILATIONS = tuple(d for _, d in DIL_PAIRS if d > 1)
ROPE_THETA = 500000.0
ROPE_FRACTION = 4

LANES = 128
SEG = 256
(SEG_NA_Q, SEG_NA_K, SEG_NA_V, SEG_DF_Q, SEG_DF_K, SEG_DF_V,
 SEG_GQ_Q, SEG_GQ_K, SEG_GQ_V, SEG_DL_Q, SEG_DL_K, SEG_DL_V) = range(12)
N_SEG = 12
EXT_WIDTH = N_SEG * SEG

VMEM_LIMIT = 52 * 1024 * 1024

VT_ROWS = HEAD_DIM + 16
COL_BLOCK = 256
DIL_SUB = 128
DIL_TQ_FIXED_SHIFT = 1024
DIL_TQ_ROWMAX = 256
QK_AHEAD = 4
KV_UNROLL = 4
FLASH_COLS = 4096
MAX_FIXED_SHIFT = 40.0

NA_QROWS = 8
NA_KROWS = 16


def _cparams(sem):
    return pltpu.CompilerParams(dimension_semantics=sem, vmem_limit_bytes=VMEM_LIMIT)


def _resident(shape):
    nd = len(shape)
    return pl.BlockSpec(shape, lambda *_: (0,) * nd, pipeline_mode=pl.Buffered(1))


def _rms_rows(x, gain):
    ms = jnp.mean(x * x, axis=-1, keepdims=True)
    return x * lax.rsqrt(ms + RMS_EPS) * gain


def _group_mean_sq(y, gmat):
    sq = y * y
    hi = sq.astype(BF16)
    lo = (sq - hi.astype(F32)).astype(BF16)
    return (jnp.dot(hi, gmat, preferred_element_type=F32)
            + jnp.dot(lo, gmat, preferred_element_type=F32))


def _dot_nt(a, b):
    return lax.dot_general(a, b, (((1,), (1,)), ((), ())), preferred_element_type=F32)


def _ffn_tile(x, g_ref, wg_ref, wu_ref, wd_ref, chunks):
    h = _rms_rows(x, g_ref[...]).astype(BF16)
    acc = None
    for c0, c1 in chunks:
        g = jnp.dot(h, wg_ref[:, c0:c1], preferred_element_type=F32)
        u = jnp.dot(h, wu_ref[:, c0:c1], preferred_element_type=F32)
        a = (g * (1.0 / (1.0 + jnp.exp(-g))) * u).astype(BF16)
        d = jnp.dot(a, wd_ref[c0:c1, :], preferred_element_type=F32)
        acc = d if acc is None else acc + d
    return x + 0.5 * acc


def _ffn_kernel(x_ref, g_ref, wg_ref, wu_ref, wd_ref, o_ref, *, chunks):
    o_ref[...] = _ffn_tile(x_ref[...], g_ref, wg_ref, wu_ref, wd_ref, chunks)


def _ffn_chunks(F, fc=512):
    return tuple((c, min(c + fc, F)) for c in range(0, F, fc))


def _ffn(x, gain, wg, wu, wd, *, tm=512):
    B, S, D = x.shape
    F = wg.shape[1]
    chunks = _ffn_chunks(F)
    xspec = pl.BlockSpec((None, tm, D), lambda b, i: (b, i, 0))
    return pl.pallas_call(
        functools.partial(_ffn_kernel, chunks=chunks),
        out_shape=jax.ShapeDtypeStruct(x.shape, F32),
        grid=(B, S // tm),
        in_specs=[xspec, _resident((1, D)), _resident((D, F)), _resident((D, F)),
                  _resident((F, D))],
        out_specs=xspec,
        compiler_params=_cparams(("parallel", "parallel")),
        name="ffn",
    )(x, gain.reshape(1, D), wg, wu, wd)


_SEG_PLAN = {
    SEG_NA_Q: (0, 0, None), SEG_NA_K: (1, 0, None),
    SEG_DF_Q: (2, 1, "df"), SEG_DF_K: (3, 1, "df"),
    SEG_GQ_Q: (4, 0, "gq"), SEG_GQ_K: (5, 0, "gq"),
    SEG_DL_Q: (6, 0, "dl"), SEG_DL_K: (7, 0, "dl"),
}
_ROPE_HALF = {"df": DIFF_QK_DIM // ROPE_FRACTION // 2, "gq": HEAD_DIM // 4,
              "dl": HEAD_DIM // ROPE_FRACTION // 2}


def _inproj_kernel(x_ref, gn_ref, w_ref, gmat_ref, gains_ref,
                   cdf_ref, sdf_ref, cgq_ref, sgq_ref, cdl_ref, sdl_ref,
                   o_ref, vtdf_ref, vtgq_ref, *dl_refs_and_scratch):
    *dl_refs, ys_ref = dl_refs_and_scratch
    tm = x_ref.shape[0]
    h = _rms_rows(x_ref[...], gn_ref[...]).astype(BF16)
    lane = lax.broadcasted_iota(jnp.int32, (1, SEG), 1)
    tables = {"df": (cdf_ref, sdf_ref), "gq": (cgq_ref, sgq_ref), "dl": (cdl_ref, sdl_ref)}
    vt_plan = {SEG_DF_V: (vtdf_ref, tuple(range(DIFF_HEADS))),
               SEG_GQ_V: (vtgq_ref, tuple(2 * j for j in range(GQA_KV_HEADS)))}
    for seg in range(N_SEG):
        y = jnp.dot(h, w_ref[:, seg * SEG:(seg + 1) * SEG], preferred_element_type=F32)
        if seg in vt_plan:
            vt_ref, lane_heads = vt_plan[seg]
            yt = y.T
            ones = jnp.ones((VT_ROWS - HEAD_DIM, yt.shape[1]), BF16)
            for j, lh in enumerate(lane_heads):
                vt_ref[j * VT_ROWS:j * VT_ROWS + HEAD_DIM, :] = (
                    yt[lh * HEAD_DIM:(lh + 1) * HEAD_DIM].astype(BF16))
                vt_ref[j * VT_ROWS + HEAD_DIM:(j + 1) * VT_ROWS, :] = ones
        plan = _SEG_PLAN.get(seg)
        if plan is not None:
            gi, mi, rope = plan
            ms = _group_mean_sq(y, gmat_ref[mi])
            y = y * lax.rsqrt(ms + RMS_EPS) * gains_ref[gi:gi + 1, :]
            if rope is not None:
                half = _ROPE_HALF[rope]
                c_ref, s_ref = tables[rope]
                first = (lane & (2 * half - 1)) < half
                partner = jnp.where(first, pltpu.roll(y, SEG - half, 1), pltpu.roll(y, half, 1))
                y = y * c_ref[...] + partner * s_ref[...]
        o_ref[:, seg * SEG:(seg + 1) * SEG] = y.astype(BF16)
        if SEG_DL_Q <= seg <= SEG_DL_V:
            for hf in range(SEG // LANES):
                ys_ref[hf] = y[:, hf * LANES:(hf + 1) * LANES]
            col = (seg - SEG_DL_Q) * SEG
            for dl_ref, dil in zip(dl_refs, DILATIONS):
                for r in range(dil):
                    for hf in range(SEG // LANES):
                        dl_ref[r, :, col + hf * LANES:col + (hf + 1) * LANES] = (
                            ys_ref[hf, pl.ds(r, tm // dil, stride=dil), :].astype(BF16))


def _inproj(x, gain, w_ext, gmats, gains, tables, *, tm=512):
    B, S, D = x.shape
    tspec = pl.BlockSpec((tm, SEG), lambda i, b: (i, 0))
    vt_rows = (DIFF_HEADS * VT_ROWS, GQA_KV_HEADS * VT_ROWS)
    return pl.pallas_call(
        _inproj_kernel,
        out_shape=(jax.ShapeDtypeStruct((B, S, EXT_WIDTH), BF16),)
        + tuple(jax.ShapeDtypeStruct((B, S // tm, r, tm), BF16) for r in vt_rows)
        + tuple(jax.ShapeDtypeStruct((B, d, S // d, 3 * SEG), BF16) for d in DILATIONS),
        grid=(S // tm, B),
        in_specs=[pl.BlockSpec((None, tm, D), lambda i, b: (b, i, 0)),
                  _resident((1, D)), _resident((D, EXT_WIDTH)),
                  _resident(gmats.shape), _resident(gains.shape)] + [tspec] * 6,
        out_specs=(pl.BlockSpec((None, tm, EXT_WIDTH), lambda i, b: (b, i, 0)),)
        + tuple(pl.BlockSpec((None, None, r, tm), lambda i, b: (b, i, 0, 0)) for r in vt_rows)
        + tuple(pl.BlockSpec((None, d, tm // d, 3 * SEG), lambda i, b: (b, 0, i, 0))
                for d in DILATIONS),
        scratch_shapes=[pltpu.VMEM((SEG // LANES, tm, LANES), F32)],
        compiler_params=_cparams(("parallel", "parallel")),
        name="inproj",
    )(x, gain.reshape(1, D), w_ext, gmats, gains, *tables)


def _flash_kernel(q_ref, k_ref, vt_ref, gmat_ref, ogain_ref, lamp_ref, li_ref, shift_ref, o_ref,
                  qmt_ref, m_ref, acc_ref, sbuf_ref, *, groups, tq, tk, diff, vheads, bounded):
    gw = SEG // groups
    qt = q_ref[...].astype(F32).T
    row_g = lax.broadcasted_iota(jnp.int32, (SEG, 1), 0) >> int(math.log2(gw))
    for r in range(groups):
        qmt_ref[:, r * tq:(r + 1) * tq] = jnp.where(row_g == r, qt, 0.0).astype(BF16)
    m_ref[...] = jnp.full(m_ref.shape, MASK_VALUE, F32)
    acc_ref[...] = jnp.zeros(acc_ref.shape, F32)
    ncol = groups * tq // COL_BLOCK

    nkv = k_ref.shape[0] // tk
    ahead = sbuf_ref.shape[0]

    def k_block(kv):
        return k_ref[pl.ds(pl.multiple_of(kv * tk, tk), tk), :]

    def qk(k, c):
        return jnp.dot(k, qmt_ref[:, c * COL_BLOCK:(c + 1) * COL_BLOCK],
                       preferred_element_type=F32)

    k0 = k_block(0)
    for c in range(ahead):
        sbuf_ref[c] = qk(k0, c)

    shift = shift_ref[...]

    def body(it, carry):
        pend = [sbuf_ref[c] for c in range(ahead)]
        for u in range(KV_UNROLL):
            kv = it * KV_UNROLL + u
            k = k_block(kv)
            k_next = k_block(jnp.minimum(kv + 1, nkv - 1))
            vt = vt_ref[kv]
            for c in range(ncol):
                hv = (c * COL_BLOCK // tq) * vheads // groups
                cs = slice(c * COL_BLOCK, (c + 1) * COL_BLOCK)
                s = pend.pop(0)
                if c + ahead < ncol:
                    pend.append(qk(k, c + ahead))
                elif u + 1 < KV_UNROLL:
                    pend.append(qk(k_next, c + ahead - ncol))
                else:
                    sbuf_ref[c + ahead - ncol] = qk(k_next, c + ahead - ncol)
                vth = vt[hv * VT_ROWS:(hv + 1) * VT_ROWS]
                if bounded:
                    p = jnp.exp2((s - shift).astype(BF16))
                    acc_ref[:, cs] = acc_ref[:, cs] + jnp.dot(vth, p, preferred_element_type=F32)
                else:
                    m_prev = m_ref[:, cs]
                    m_new = jnp.maximum(m_prev, jnp.max(s, axis=0, keepdims=True))
                    alpha = jnp.exp2(m_prev - m_new)
                    p = jnp.exp2(s - m_new).astype(BF16)
                    acc_ref[:, cs] = (acc_ref[:, cs] * alpha
                                      + jnp.dot(vth, p, preferred_element_type=F32))
                    m_ref[:, cs] = m_new
        return carry

    lax.fori_loop(0, nkv // KV_UNROLL, body, 0)

    acc = acc_ref[...]
    norm = acc[0:HEAD_DIM] * (1.0 / acc[HEAD_DIM:HEAD_DIM + 1])
    if diff:
        lp = lamp_ref[...]
        lam = (jnp.exp(jnp.sum(lp[0:1] * lp[1:2], axis=1, keepdims=True))
               - jnp.exp(jnp.sum(lp[2:3] * lp[3:4], axis=1, keepdims=True)) + li_ref[...])
        pieces = [norm[:, 2 * hd * tq:(2 * hd + 1) * tq]
                  - lam * norm[:, (2 * hd + 1) * tq:(2 * hd + 2) * tq] for hd in range(groups // 2)]
    else:
        pieces = [norm[:, r * tq:(r + 1) * tq] for r in range(groups)]
    out = jnp.concatenate(pieces, axis=0).T
    if diff:
        ms = _group_mean_sq(out, gmat_ref[...])
        out = out * lax.rsqrt(ms + RMS_EPS) * ogain_ref[...] * (1.0 - li_ref[...])
    o_ref[...] = out.astype(BF16)


def _flash(qkv, vt, segs, gmat, ogain, lamp, li, shift, *, groups, diff, bounded,
           cols=FLASH_COLS):
    B, S, _ = qkv.shape
    _, nkv, vrows, tk = vt.shape
    assert nkv % KV_UNROLL == 0 and S % (cols // groups) == 0
    tq = cols // groups
    qs, ks = segs
    return pl.pallas_call(
        functools.partial(_flash_kernel, groups=groups, tq=tq, tk=tk, diff=diff,
                          vheads=vrows // VT_ROWS, bounded=bounded),
        out_shape=jax.ShapeDtypeStruct((B, S, SEG), BF16),
        grid=(B, S // tq),
        in_specs=[pl.BlockSpec((None, tq, SEG), lambda b, i: (b, i, qs)),
                  pl.BlockSpec((None, S, SEG), lambda b, i: (b, 0, ks)),
                  pl.BlockSpec((None, nkv, vrows, tk), lambda b, i: (b, 0, 0, 0)),
                  _resident(gmat.shape), _resident(ogain.shape), _resident(lamp.shape),
                  _resident(li.shape), _resident(shift.shape)],
        out_specs=pl.BlockSpec((None, tq, SEG), lambda b, i: (b, i, 0)),
        scratch_shapes=[pltpu.VMEM((SEG, cols), BF16), pltpu.VMEM((1, cols), F32),
                        pltpu.VMEM((VT_ROWS, cols), F32),
                        pltpu.VMEM((QK_AHEAD, tk, COL_BLOCK), F32)],
        compiler_params=_cparams(("parallel", "parallel")),
        name=("diff_attn" if diff else "gqa_attn") + ("" if bounded else "_runmax"),
    )(qkv, qkv, vt, gmat, ogain, lamp, li, shift)


def _na_kernel(q_ref, k_ref, v_ref, tab_ref, o_ref, *, nq, nk, bounded):
    i = pl.program_id(2)
    seq = k_ref.shape[0]
    start = jnp.clip(i * nq - (nk - nq) // 2, 0, seq - nk)
    start = pl.multiple_of(start, (nk - nq) // 2)
    kw = k_ref[pl.ds(start, nk), :]
    vw = v_ref[pl.ds(start, nk), :]
    q = q_ref[...]
    if bounded:
        qt = q.astype(F32).T
        vt = vw.astype(F32).T
        row_h = (lax.broadcasted_iota(jnp.int32, (2 * HEAD_DIM, 1), 0)
                 >> int(math.log2(HEAD_DIM)))
        ones = jnp.ones((VT_ROWS - HEAD_DIM, nk), BF16)
        outs = []
        scores = [jnp.dot(kw, jnp.where(row_h == j, qt, 0.0).astype(BF16),
                          preferred_element_type=F32) for j in range(2)]
        for j in range(2):
            p = jnp.exp2(scores[j] + tab_ref[j]).astype(BF16)
            vth = jnp.concatenate(
                [vt[j * HEAD_DIM:(j + 1) * HEAD_DIM].astype(BF16), ones], axis=0)
            pv = jnp.dot(vth, p, preferred_element_type=F32)
            outs.append(pv[0:HEAD_DIM] * (1.0 / pv[HEAD_DIM:HEAD_DIM + 1]))
        o_ref[...] = jnp.concatenate(outs, axis=0).T.astype(BF16)
        return
    lane_h = lax.broadcasted_iota(jnp.int32, (1, 2 * HEAD_DIM), 1) >> int(math.log2(HEAD_DIM))
    outs = []
    for j in range(2):
        qm = jnp.where(lane_h == j, q, jnp.zeros_like(q))
        s = _dot_nt(qm, kw) + tab_ref[j]
        m = jnp.max(s, axis=1, keepdims=True)
        p = jnp.exp2(s - m)
        l = jnp.sum(p, axis=1, keepdims=True)
        outs.append(jnp.dot(p.astype(BF16), vw, preferred_element_type=F32) * (1.0 / l))
    o_ref[...] = jnp.where(lane_h == 0, outs[0], outs[1]).astype(BF16)


def _na_table(rel_bias, rows, shift=None):
    kh = min(NA_WIN_H, rows)
    n_ro, n_co = 2 * NA_WIN_H - 1, 2 * NA_WIN_W - 1
    c = np.arange(GRID_W)[:, None]
    kc = np.arange(GRID_W)[None, :]
    cs = np.clip(c - NA_WIN_W // 2, 0, GRID_W - NA_WIN_W)
    col_valid = (kc >= cs) & (kc < cs + NA_WIN_W)
    col_sel = (kc - c + (NA_WIN_W - 1))[..., None] == np.arange(n_co)
    row_sel, row_valid, row_off = [], [], []
    for r0 in (0, NA_QROWS, rows - NA_QROWS):
        kr0 = int(np.clip(r0 - (NA_KROWS - NA_QROWS) // 2, 0, rows - NA_KROWS))
        r = r0 + np.arange(NA_QROWS)[:, None]
        kr = kr0 + np.arange(NA_KROWS)[None, :]
        rs = np.clip(r - kh // 2, 0, rows - kh)
        row_valid.append((kr >= rs) & (kr < rs + kh))
        row_off.append(kr - r + (NA_WIN_H - 1))
        row_sel.append(row_off[-1][..., None] == np.arange(n_ro))
    row_sel = jnp.asarray(np.stack(row_sel), F32)
    valid = (np.stack(row_valid)[:, None, :, None, :, None]
             & col_valid[None, None, None, :, None, :])
    hi = lax.Precision.HIGHEST
    cols = jnp.einsum("hab,cdb->hacd", rel_bias.astype(F32), jnp.asarray(col_sel, F32),
                      precision=hi)
    nq, nk = NA_QROWS * GRID_W, NA_KROWS * GRID_W
    if shift is None:
        full = jnp.einsum("vqka,hacd->vhqckd", row_sel, cols, precision=hi)
        table = jnp.where(valid, full * LOG2E, MASK_VALUE)
        return table.reshape(3, rel_bias.shape[0], nq, nk)
    heads = rel_bias.shape[0]
    blocks = jnp.where(col_valid[None, None], cols * LOG2E - shift, MASK_VALUE)
    blocks = jnp.concatenate([blocks.transpose(0, 1, 3, 2),
                              jnp.full((heads, 1, GRID_W, GRID_W), MASK_VALUE, F32)], axis=1)
    block_id = np.where(np.stack(row_valid), np.stack(row_off), n_ro).transpose(0, 2, 1)
    pairs = block_id.reshape(3, NA_KROWS, NA_QROWS // 2, 2)
    uniq, pair_id = np.unique(pairs.reshape(-1, 2), axis=0, return_inverse=True)
    pair_id = pair_id.reshape(3, NA_KROWS, NA_QROWS // 2)
    pair_blocks = jnp.concatenate([jnp.take(blocks, jnp.asarray(uniq[:, 0]), axis=1),
                                   jnp.take(blocks, jnp.asarray(uniq[:, 1]), axis=1)], axis=-1)

    def tile_kernel(pb_ref, o_ref):
        for v in range(3):
            for k in range(NA_KROWS):
                for qp in range(NA_QROWS // 2):
                    o_ref[v, k * GRID_W:(k + 1) * GRID_W,
                          qp * 2 * GRID_W:(qp + 1) * 2 * GRID_W] = pb_ref[int(pair_id[v, k, qp])]

    return pl.pallas_call(
        tile_kernel,
        out_shape=jax.ShapeDtypeStruct((3, heads, nk, nq), F32),
        grid=(heads,),
        in_specs=[pl.BlockSpec((None,) + pair_blocks.shape[1:], lambda h: (h, 0, 0, 0))],
        out_specs=pl.BlockSpec((3, None, nk, nq), lambda h: (0, h, 0, 0)),
        compiler_params=_cparams(("parallel",)),
        name="na_table",
    )(pair_blocks)


def _na(qkv, table, *, bounded):
    B, S, _ = qkv.shape
    nq, nk = NA_QROWS * GRID_W, NA_KROWS * GRID_W
    nblk = S // nq
    hp = 2 * HEAD_DIM

    def variant(i):
        return jnp.where(i == 0, 0, jnp.where(i == nblk - 1, 2, 1))

    return pl.pallas_call(
        functools.partial(_na_kernel, nq=nq, nk=nk, bounded=bounded),
        out_shape=jax.ShapeDtypeStruct((B, S, SEG), BF16),
        grid=(2, B, nblk),
        in_specs=[pl.BlockSpec((None, nq, hp), lambda p, b, i: (b, i, 2 * SEG_NA_Q + p)),
                  pl.BlockSpec((None, S, hp), lambda p, b, i: (b, 0, 2 * SEG_NA_K + p)),
                  pl.BlockSpec((None, S, hp), lambda p, b, i: (b, 0, 2 * SEG_NA_V + p)),
                  pl.BlockSpec((None, 2) + table.shape[2:], lambda p, b, i: (variant(i), p, 0, 0))],
        out_specs=pl.BlockSpec((None, nq, hp), lambda p, b, i: (b, i, p)),
        compiler_params=_cparams(("parallel", "parallel", "arbitrary")),
        name="na_attn" if bounded else "na_attn_rowmax",
    )(qkv, qkv, qkv, table)


def _dil_fixed_shift_kernel(q_ref, k_ref, v_ref, shift_ref, o_ref, den_ref, *, tq, halo):
    j = pl.program_id(2)
    length = k_ref.shape[0]
    win = DIL_SUB + 2 * halo
    lane_h = lax.broadcasted_iota(jnp.int32, (1, SEG), 1) >> int(math.log2(HEAD_DIM))
    for u in range(tq // DIL_SUB):
        q0 = j * tq + u * DIL_SUB
        start = pl.multiple_of(jnp.clip(q0 - halo, 0, length - win), halo)
        kw = k_ref[pl.ds(start, win), :]
        vw = v_ref[pl.ds(start, win), :]
        q = q_ref[u * DIL_SUB:(u + 1) * DIL_SUB, :]
        qm = jnp.concatenate([jnp.where(lane_h == hd, q, jnp.zeros_like(q))
                              for hd in range(DIL_HEADS)], axis=0)
        qpos = q0 + lax.broadcasted_iota(jnp.int32, (DIL_SUB, win), 0)
        kpos = start + lax.broadcasted_iota(jnp.int32, (DIL_SUB, win), 1)
        bias = jnp.where(jnp.abs(kpos - qpos) <= halo, -shift_ref[...], MASK_VALUE)
        s = _dot_nt(qm, kw)
        p = jnp.exp2(s + jnp.concatenate([bias] * DIL_HEADS, axis=0))
        den = jnp.sum(p, axis=1, keepdims=True)
        pv = jnp.dot(p.astype(BF16), vw, preferred_element_type=F32)
        num_u = jnp.zeros((DIL_SUB, SEG), F32)
        den_u = jnp.zeros((DIL_SUB, SEG), F32)
        for hd in range(DIL_HEADS):
            rows = slice(hd * DIL_SUB, (hd + 1) * DIL_SUB)
            num_u = jnp.where(lane_h == hd, pv[rows], num_u)
            den_u = jnp.where(lane_h == hd, den[rows], den_u)
        o_ref[u * DIL_SUB:(u + 1) * DIL_SUB, :] = num_u
        den_ref[u * DIL_SUB:(u + 1) * DIL_SUB, :] = den_u


def _dil_kernel(q_ref, k_ref, v_ref, shift_ref, o_ref, lse_ref, *, tq, halo):
    del shift_ref
    j = pl.program_id(2)
    length = k_ref.shape[0]
    win = tq + 2 * halo
    start = pl.multiple_of(jnp.clip(j * tq - halo, 0, length - win), halo)
    kw = k_ref[pl.ds(start, win), :]
    vw = v_ref[pl.ds(start, win), :]
    q = q_ref[...]
    qpos = j * tq + lax.broadcasted_iota(jnp.int32, (tq, win), 0)
    kpos = start + lax.broadcasted_iota(jnp.int32, (tq, win), 1)
    bias = jnp.where(jnp.abs(kpos - qpos) <= halo, 0.0, MASK_VALUE)
    lane_h = lax.broadcasted_iota(jnp.int32, (1, SEG), 1) >> int(math.log2(HEAD_DIM))
    out = jnp.zeros((tq, SEG), F32)
    lse = jnp.zeros((tq, SEG), F32)
    for hd in range(DIL_HEADS):
        qm = jnp.where(lane_h == hd, q, jnp.zeros_like(q))
        s = _dot_nt(qm, kw) + bias
        m = jnp.max(s, axis=1, keepdims=True)
        p = jnp.exp2(s - m)
        l = jnp.sum(p, axis=1, keepdims=True)
        o = jnp.dot(p.astype(BF16), vw, preferred_element_type=F32) * (1.0 / l)
        out = jnp.where(lane_h == hd, o, out)
        lse = jnp.where(lane_h == hd, m + jnp.log(l) * LOG2E, lse)
    o_ref[...] = out
    lse_ref[...] = lse


def _dilated(src, first_seg, halo, shift, *, bounded):
    length = src.shape[-2]
    tq = min(DIL_TQ_FIXED_SHIFT, length) if bounded else DIL_TQ_ROWMAX
    assert length % tq == 0 and tq % DIL_SUB == 0
    if src.ndim == 3:
        (B, length, _), dil = src.shape, 1
        in_specs = [pl.BlockSpec((None, tq, SEG), lambda b, r, j: (b, j, first_seg)),
                    pl.BlockSpec((None, length, SEG), lambda b, r, j: (b, 0, first_seg + 1)),
                    pl.BlockSpec((None, length, SEG), lambda b, r, j: (b, 0, first_seg + 2))]
    else:
        B, dil, length, _ = src.shape
        in_specs = [pl.BlockSpec((None, None, tq, SEG), lambda b, r, j: (b, r, j, first_seg)),
                    pl.BlockSpec((None, None, length, SEG),
                                 lambda b, r, j: (b, r, 0, first_seg + 1)),
                    pl.BlockSpec((None, None, length, SEG),
                                 lambda b, r, j: (b, r, 0, first_seg + 2))]
    oshape = jax.ShapeDtypeStruct((B, dil, length, SEG), F32)
    ospec = pl.BlockSpec((None, None, tq, SEG), lambda b, r, j: (b, r, j, 0))
    return pl.pallas_call(
        functools.partial(_dil_fixed_shift_kernel if bounded else _dil_kernel, tq=tq, halo=halo),
        out_shape=(oshape, oshape),
        grid=(B, dil, length // tq),
        in_specs=in_specs + [_resident(shift.shape)],
        out_specs=(ospec, ospec),
        compiler_params=_cparams(("parallel", "parallel", "arbitrary")),
        name=f"dil_attn_{dil}" + ("" if bounded else "_rowmax"),
    )(src, src, src, shift)


def _outproj_kernel(x_ref, oa_ref, ob_ref, oc_ref, d0_ref, d1_ref, d2_ref,
                    l0_ref, l1_ref, l2_ref, w_ref, fg_ref, fwg_ref, fwu_ref, fwd_ref,
                    o_ref, cat_ref, *row_scratch, bounded, chunks):
    row_scratch = list(row_scratch)

    def token_rows(ref):
        dil = ref.shape[0]
        if dil == 1:
            return ref[0]
        buf = row_scratch.pop()
        for r in range(dil):
            slab = ref[r]
            for hf in range(SEG // LANES):
                buf[hf, pl.ds(r, ref.shape[1], stride=dil), :] = slab[:, hf * LANES:(hf + 1) * LANES]
        return jnp.concatenate([buf[hf] for hf in range(SEG // LANES)], axis=1)

    l0, l1, l2 = token_rows(l0_ref), token_rows(l1_ref), token_rows(l2_ref)
    if bounded:
        od = ((token_rows(d0_ref) + token_rows(d1_ref) + token_rows(d2_ref))
              * (1.0 / (l0 + l1 + l2)))
    else:
        m = jnp.maximum(jnp.maximum(l0, l1), l2)
        e0, e1, e2 = jnp.exp2(l0 - m), jnp.exp2(l1 - m), jnp.exp2(l2 - m)
        od = ((e0 * token_rows(d0_ref) + e1 * token_rows(d1_ref) + e2 * token_rows(d2_ref))
              * (1.0 / (e0 + e1 + e2)))
    cat_ref[:, 0 * SEG:1 * SEG] = oa_ref[...]
    cat_ref[:, 1 * SEG:2 * SEG] = ob_ref[...]
    cat_ref[:, 2 * SEG:3 * SEG] = oc_ref[...]
    cat_ref[:, 3 * SEG:4 * SEG] = od.astype(BF16)
    mixed = x_ref[...] + jnp.dot(cat_ref[...], w_ref[...], preferred_element_type=F32)
    o_ref[...] = _ffn_tile(mixed, fg_ref, fwg_ref, fwu_ref, fwd_ref, chunks)


def _outproj_ffn(x, oa, ob, oc, dils, w_out, fgain, fwg, fwu, fwd, *, bounded, tm=512):
    B, S, D = x.shape
    F = fwg.shape[1]
    xspec = pl.BlockSpec((None, tm, D), lambda b, i: (b, i, 0))
    sspec = pl.BlockSpec((None, tm, SEG), lambda b, i: (b, i, 0))
    outs = [d[0] for d in dils]
    lses = [d[1] for d in dils]
    dspecs = [pl.BlockSpec((None, a.shape[1], tm // a.shape[1], SEG), lambda b, i: (b, 0, i, 0))
              for a in outs + lses]
    n_scratch = sum(a.shape[1] > 1 for a in outs + lses)
    return pl.pallas_call(
        functools.partial(_outproj_kernel, bounded=bounded, chunks=_ffn_chunks(F)),
        out_shape=jax.ShapeDtypeStruct(x.shape, F32),
        grid=(B, S // tm),
        in_specs=[xspec] + [sspec] * 3 + dspecs
        + [_resident(w_out.shape), _resident((1, D)), _resident((D, F)), _resident((D, F)),
           _resident((F, D))],
        out_specs=xspec,
        scratch_shapes=[pltpu.VMEM((tm, 4 * SEG), BF16)]
        + [pltpu.VMEM((SEG // LANES, tm, LANES), F32)] * n_scratch,
        compiler_params=_cparams(("parallel", "parallel")),
        name="outproj_ffn" if bounded else "outproj_lse_ffn",
    )(x, oa, ob, oc, *outs, *lses, w_out, fgain.reshape(1, D), fwg, fwu, fwd)


def _rope_cs(pos, dim, theta):
    inv_freq = 1.0 / (theta ** (jnp.arange(0, dim, 2, dtype=F32) / dim))
    ang = pos.astype(F32)[:, None] * inv_freq[None, :]
    return jnp.cos(ang), jnp.sin(ang)


def _rope_tables(S):
    pos = jnp.arange(S, dtype=jnp.int32)

    def expand(parts_c, parts_s, reps):
        return (jnp.tile(jnp.concatenate(parts_c, axis=1), (1, reps)),
                jnp.tile(jnp.concatenate(parts_s, axis=1), (1, reps)))

    c, s = _rope_cs(pos, DIFF_QK_DIM // ROPE_FRACTION, ROPE_THETA)
    rest = DIFF_QK_DIM - 2 * c.shape[1]
    df = expand([c, c, jnp.ones((S, rest), F32)], [-s, s, jnp.zeros((S, rest), F32)],
                SEG // DIFF_QK_DIM)
    c, s = _rope_cs(pos, HEAD_DIM // ROPE_FRACTION, ROPE_THETA)
    rest = HEAD_DIM - 2 * c.shape[1]
    dl = expand([c, c, jnp.ones((S, rest), F32)], [-s, s, jnp.zeros((S, rest), F32)],
                SEG // HEAD_DIM)
    cr, sr = _rope_cs(pos // GRID_W, HEAD_DIM // 2, AXIAL_THETA)
    cc, sc = _rope_cs(pos % GRID_W, HEAD_DIM // 2, AXIAL_THETA)
    gq = expand([cr, cr, cc, cc], [-sr, sr, -sc, sc], SEG // HEAD_DIM)
    return df + gq + dl


def _group_matrices():
    lane = np.arange(SEG)
    mats = [(lane[:, None] // g == lane[None, :] // g).astype(np.float32) / g
            for g in (HEAD_DIM, DIFF_QK_DIM)]
    return jnp.asarray(np.stack(mats), BF16)


def _extend_w_in(w_in):
    kv0 = 7 * SEG
    hd = HEAD_DIM
    cols = np.concatenate([
        np.arange(0, kv0),
        kv0 + np.array([0, 0, 1, 1]).repeat(hd) * hd + np.tile(np.arange(hd), 4),
        kv0 + 2 * hd + np.array([0, 0, 1, 1]).repeat(hd) * hd + np.tile(np.arange(hd), 4),
        np.arange(kv0 + 4 * hd, w_in.shape[1]),
    ])
    return w_in[:, cols].astype(BF16)


def _gain_rows(na_q, na_k, df_q, df_k, gq_q, gq_k, dl_q, dl_k):
    def row(g, scale):
        return jnp.tile(g.astype(F32), SEG // g.shape[0]) * scale
    sc64 = HEAD_DIM ** -0.5 * LOG2E
    sc32 = DIFF_QK_DIM ** -0.5 * LOG2E
    return jnp.stack([row(na_q, sc64), row(na_k, 1.0), row(df_q, sc32), row(df_k, 1.0),
                      row(gq_q, sc64), row(gq_k, 1.0), row(dl_q, sc64), row(dl_k, 1.0)])


def kernel(x, ffn1_norm, ffn1_w_gate, ffn1_w_up, ffn1_w_down, mix_norm, w_in, w_out, na_q_norm, na_k_norm, na_rel_bias, diff_q_norm, diff_k_norm, diff_lambda_q1, diff_lambda_k1, diff_lambda_q2, diff_lambda_k2, diff_out_norm, gqa_q_norm, gqa_k_norm, dil_q_norm, dil_k_norm, ffn2_norm, ffn2_w_gate, ffn2_w_up, ffn2_w_down):
    B, S, D = x.shape
    depth = w_in.shape[0]
    rows = S // GRID_W
    tables = _rope_tables(S)
    gmats = _group_matrices()
    ones_row = jnp.ones((1, SEG), F32)
    zero_lamp = jnp.zeros((4, DIFF_QK_DIM), F32)
    zero_li = jnp.zeros((1, 1), F32)

    for l in range(depth):
        x = _ffn(x, ffn1_norm[l], ffn1_w_gate[l].astype(BF16), ffn1_w_up[l].astype(BF16),
                 ffn1_w_down[l].astype(BF16))

        gains = _gain_rows(na_q_norm[l], na_k_norm[l], diff_q_norm[l], diff_k_norm[l],
                           gqa_q_norm[l], gqa_k_norm[l], dil_q_norm[l], dil_k_norm[l])
        qkv, vt_df, vt_gq, *dl_classes = _inproj(x, mix_norm[l], _extend_w_in(w_in[l]), gmats,
                                                 gains, tables)

        na_bias = na_rel_bias[l]
        lamp = jnp.stack([diff_lambda_q1[l], diff_lambda_k1[l], diff_lambda_q2[l],
                          diff_lambda_k2[l]]).astype(F32)
        li = jnp.full((1, 1), 0.8 - 0.6 * math.exp(-0.3 * l), F32)
        ogain = jnp.tile(diff_out_norm[l].astype(F32), SEG // HEAD_DIM).reshape(1, SEG)
        dl_src = {1: (qkv, SEG_DL_Q)}
        dl_src.update({d: (a, 0) for d, a in zip(DILATIONS, dl_classes)})
        w_out_l = w_out[l].astype(BF16)
        ffn2 = (ffn2_norm[l], ffn2_w_gate[l].astype(BF16), ffn2_w_up[l].astype(BF16),
                ffn2_w_down[l].astype(BF16))

        def score_bound(qi, width):
            return width * jnp.max(jnp.abs(gains[qi])) * jnp.max(jnp.abs(gains[qi + 1]))
        bounds = {"na": score_bound(0, HEAD_DIM) + LOG2E * jnp.max(jnp.abs(na_rel_bias[l])),
                  "df": score_bound(2, DIFF_QK_DIM), "gq": score_bound(4, HEAD_DIM),
                  "dl": score_bound(6, HEAD_DIM)}
        shifts = {k: v.reshape(1, 1).astype(F32) for k, v in bounds.items()}

        def mix(bounded, x=x, qkv=qkv, vt_df=vt_df, vt_gq=vt_gq, dl_src=dl_src, shifts=shifts,
                na_bias=na_bias, ogain=ogain, lamp=lamp, li=li, w_out_l=w_out_l, ffn2=ffn2):
            na_table = _na_table(na_bias, rows, shifts["na"][0, 0] if bounded else None)
            o_a = _na(qkv, na_table, bounded=bounded)
            o_b = _flash(qkv, vt_df, (SEG_DF_Q, SEG_DF_K), gmats[0], ogain, lamp, li,
                         shifts["df"], groups=2 * DIFF_HEADS, diff=True, bounded=bounded)
            o_c = _flash(qkv, vt_gq, (SEG_GQ_Q, SEG_GQ_K), gmats[0], ones_row, zero_lamp,
                         zero_li, shifts["gq"], groups=GQA_Q_HEADS, diff=False, bounded=bounded)
            dils = [_dilated(*dl_src[dil], (window // 2) // dil, shifts["dl"], bounded=bounded)
                    for window, dil in DIL_PAIRS]
            return _outproj_ffn(x, o_a, o_b, o_c, dils, w_out_l, *ffn2, bounded=bounded)

        safe = functools.reduce(jnp.maximum, bounds.values()) <= MAX_FIXED_SHIFT
        x = lax.cond(safe, functools.partial(mix, True), functools.partial(mix, False))
    return x
```

```python
import functools
import math

import numpy as np
import jax
import jax.numpy as jnp
from jax import lax
from jax.experimental import pallas as pl
from jax.experimental.pallas import tpu as pltpu

F32 = jnp.float32
BF16 = jnp.bfloat16

GRID_W = 64
HEAD_DIM = 64
RMS_EPS = 1e-6
MASK_VALUE = -1e30
LOG2E = math.log2(math.e)

NA_HEADS = 4
NA_WIN_H = 8
NA_WIN_W = 16
DIFF_HEADS = 4
DIFF_QK_DIM = HEAD_DIM // 2
GQA_Q_HEADS = 4
GQA_KV_HEADS = 2
AXIAL_THETA = 10000.0
DIL_HEADS = 4
DIL_PAIRS = ((128, 1), (512, 4), (2048, 16))
DILATIONS = tuple(d for _, d in DIL_PAIRS if d > 1)
ROPE_THETA = 500000.0
ROPE_FRACTION = 4

LANES = 128
SEG = 256
(SEG_NA_Q, SEG_NA_K, SEG_NA_V, SEG_DF_Q, SEG_DF_K, SEG_DF_V,
 SEG_GQ_Q, SEG_GQ_K, SEG_GQ_V, SEG_DL_Q, SEG_DL_K, SEG_DL_V) = range(12)
N_SEG = 12
EXT_WIDTH = N_SEG * SEG

VMEM_LIMIT = 52 * 1024 * 1024

VT_ROWS = HEAD_DIM + 16
COL_BLOCK = 256
DIL_SUB = 128
DIL_TQ_FIXED_SHIFT = 1024
DIL_TQ_ROWMAX = 256
QK_AHEAD = 4
KV_UNROLL = 4
FLASH_COLS = 4096
MAX_FIXED_SHIFT = 40.0
MAX_BF16_EXP_SHIFT = 16.0

NA_QROWS = 8
NA_KROWS = 16


def _cparams(sem):
    return pltpu.CompilerParams(dimension_semantics=sem, vmem_limit_bytes=VMEM_LIMIT)


def _resident(shape):
    nd = len(shape)
    return pl.BlockSpec(shape, lambda *_: (0,) * nd, pipeline_mode=pl.Buffered(1))


def _rms_rows(x, gain):
    ms = jnp.mean(x * x, axis=-1, keepdims=True)
    return x * lax.rsqrt(ms + RMS_EPS) * gain


def _group_mean_sq(y, gmat):
    sq = y * y
    hi = sq.astype(BF16)
    lo = (sq - hi.astype(F32)).astype(BF16)
    return (jnp.dot(hi, gmat, preferred_element_type=F32)
            + jnp.dot(lo, gmat, preferred_element_type=F32))


def _dot_nt(a, b):
    return lax.dot_general(a, b, (((1,), (1,)), ((), ())), preferred_element_type=F32)


def _ffn_tile(x, g_ref, wg_ref, wu_ref, wd_ref, chunks):
    h = _rms_rows(x, g_ref[...]).astype(BF16)
    acc = None
    for c0, c1 in chunks:
        g = jnp.dot(h, wg_ref[:, c0:c1], preferred_element_type=F32)
        u = jnp.dot(h, wu_ref[:, c0:c1], preferred_element_type=F32)
        a = (g * (1.0 / (1.0 + jnp.exp(-g))) * u).astype(BF16)
        d = jnp.dot(a, wd_ref[c0:c1, :], preferred_element_type=F32)
        acc = d if acc is None else acc + d
    return x + 0.5 * acc


def _ffn_kernel(x_ref, g_ref, wg_ref, wu_ref, wd_ref, o_ref, *, chunks):
    o_ref[...] = _ffn_tile(x_ref[...], g_ref, wg_ref, wu_ref, wd_ref, chunks)


def _ffn_chunks(F, fc=512):
    return tuple((c, min(c + fc, F)) for c in range(0, F, fc))


def _ffn(x, gain, wg, wu, wd, *, tm=512):
    B, S, D = x.shape
    F = wg.shape[1]
    chunks = _ffn_chunks(F)
    xspec = pl.BlockSpec((None, tm, D), lambda b, i: (b, i, 0))
    return pl.pallas_call(
        functools.partial(_ffn_kernel, chunks=chunks),
        out_shape=jax.ShapeDtypeStruct(x.shape, F32),
        grid=(B, S // tm),
        in_specs=[xspec, _resident((1, D)), _resident((D, F)), _resident((D, F)),
                  _resident((F, D))],
        out_specs=xspec,
        compiler_params=_cparams(("parallel", "parallel")),
        name="ffn",
    )(x, gain.reshape(1, D), wg, wu, wd)


_SEG_PLAN = {
    SEG_NA_Q: (0, 0, None), SEG_NA_K: (1, 0, None),
    SEG_DF_Q: (2, 1, "df"), SEG_DF_K: (3, 1, "df"),
    SEG_GQ_Q: (4, 0, "gq"), SEG_GQ_K: (5, 0, "gq"),
    SEG_DL_Q: (6, 0, "dl"), SEG_DL_K: (7, 0, "dl"),
}
_ROPE_HALF = {"df": DIFF_QK_DIM // ROPE_FRACTION // 2, "gq": HEAD_DIM // 4,
              "dl": HEAD_DIM // ROPE_FRACTION // 2}


def _inproj_kernel(x_ref, gn_ref, w_ref, gmat_ref, gains_ref,
                   cdf_ref, sdf_ref, cgq_ref, sgq_ref, cdl_ref, sdl_ref,
                   o_ref, vtdf_ref, vtgq_ref, *dl_refs_and_scratch):
    *dl_refs, ys_ref = dl_refs_and_scratch
    tm = x_ref.shape[0]
    h = _rms_rows(x_ref[...], gn_ref[...]).astype(BF16)
    lane = lax.broadcasted_iota(jnp.int32, (1, SEG), 1)
    tables = {"df": (cdf_ref, sdf_ref), "gq": (cgq_ref, sgq_ref), "dl": (cdl_ref, sdl_ref)}
    vt_plan = {SEG_DF_V: (vtdf_ref, tuple(range(DIFF_HEADS))),
               SEG_GQ_V: (vtgq_ref, tuple(2 * j for j in range(GQA_KV_HEADS)))}
    for seg in range(N_SEG):
        y = jnp.dot(h, w_ref[:, seg * SEG:(seg + 1) * SEG], preferred_element_type=F32)
        if seg in vt_plan:
            vt_ref, lane_heads = vt_plan[seg]
            yt = y.T
            ones = jnp.ones((VT_ROWS - HEAD_DIM, yt.shape[1]), BF16)
            for j, lh in enumerate(lane_heads):
                vt_ref[j * VT_ROWS:j * VT_ROWS + HEAD_DIM, :] = (
                    yt[lh * HEAD_DIM:(lh + 1) * HEAD_DIM].astype(BF16))
                vt_ref[j * VT_ROWS + HEAD_DIM:(j + 1) * VT_ROWS, :] = ones
        plan = _SEG_PLAN.get(seg)
        if plan is not None:
            gi, mi, rope = plan
            ms = _group_mean_sq(y, gmat_ref[mi])
            y = y * lax.rsqrt(ms + RMS_EPS) * gains_ref[gi:gi + 1, :]
            if rope is not None:
                half = _ROPE_HALF[rope]
                c_ref, s_ref = tables[rope]
                first = (lane & (2 * half - 1)) < half
                partner = jnp.where(first, pltpu.roll(y, SEG - half, 1), pltpu.roll(y, half, 1))
                y = y * c_ref[...] + partner * s_ref[...]
        o_ref[:, seg * SEG:(seg + 1) * SEG] = y.astype(BF16)
        if SEG_DL_Q <= seg <= SEG_DL_V:
            for hf in range(SEG // LANES):
                ys_ref[hf] = y[:, hf * LANES:(hf + 1) * LANES]
            col = (seg - SEG_DL_Q) * SEG
            for dl_ref, dil in zip(dl_refs, DILATIONS):
                for r in range(dil):
                    for hf in range(SEG // LANES):
                        dl_ref[r, :, col + hf * LANES:col + (hf + 1) * LANES] = (
                            ys_ref[hf, pl.ds(r, tm // dil, stride=dil), :].astype(BF16))


def _inproj(x, gain, w_ext, gmats, gains, tables, *, tm=512):
    B, S, D = x.shape
    tspec = pl.BlockSpec((tm, SEG), lambda i, b: (i, 0))
    vt_rows = (DIFF_HEADS * VT_ROWS, GQA_KV_HEADS * VT_ROWS)
    return pl.pallas_call(
        _inproj_kernel,
        out_shape=(jax.ShapeDtypeStruct((B, S, EXT_WIDTH), BF16),)
        + tuple(jax.ShapeDtypeStruct((B, S // tm, r, tm), BF16) for r in vt_rows)
        + tuple(jax.ShapeDtypeStruct((B, d, S // d, 3 * SEG), BF16) for d in DILATIONS),
        grid=(S // tm, B),
        in_specs=[pl.BlockSpec((None, tm, D), lambda i, b: (b, i, 0)),
                  _resident((1, D)), _resident((D, EXT_WIDTH)),
                  _resident(gmats.shape), _resident(gains.shape)] + [tspec] * 6,
        out_specs=(pl.BlockSpec((None, tm, EXT_WIDTH), lambda i, b: (b, i, 0)),)
        + tuple(pl.BlockSpec((None, None, r, tm), lambda i, b: (b, i, 0, 0)) for r in vt_rows)
        + tuple(pl.BlockSpec((None, d, tm // d, 3 * SEG), lambda i, b: (b, 0, i, 0))
                for d in DILATIONS),
        scratch_shapes=[pltpu.VMEM((SEG // LANES, tm, LANES), F32)],
        compiler_params=_cparams(("parallel", "parallel")),
        name="inproj",
    )(x, gain.reshape(1, D), w_ext, gmats, gains, *tables)


def _flash_kernel(q_ref, k_ref, vt_ref, gmat_ref, ogain_ref, lamp_ref, li_ref, shift_ref, o_ref,
                  qmt_ref, m_ref, acc_ref, sbuf_ref, *, groups, tq, tk, diff, vheads, bounded,
                  bf16_exp):
    gw = SEG // groups
    qt = q_ref[...].astype(F32).T
    row_g = lax.broadcasted_iota(jnp.int32, (SEG, 1), 0) >> int(math.log2(gw))
    for r in range(groups):
        qmt_ref[:, r * tq:(r + 1) * tq] = jnp.where(row_g == r, qt, 0.0).astype(BF16)
    m_ref[...] = jnp.full(m_ref.shape, MASK_VALUE, F32)
    acc_ref[...] = jnp.zeros(acc_ref.shape, F32)
    ncol = groups * tq // COL_BLOCK

    nkv = k_ref.shape[0] // tk
    ahead = sbuf_ref.shape[0]

    def k_block(kv):
        return k_ref[pl.ds(pl.multiple_of(kv * tk, tk), tk), :]

    def qk(k, c):
        return jnp.dot(k, qmt_ref[:, c * COL_BLOCK:(c + 1) * COL_BLOCK],
                       preferred_element_type=F32)

    k0 = k_block(0)
    for c in range(ahead):
        sbuf_ref[c] = qk(k0, c)

    shift = shift_ref[...]

    def body(it, carry):
        pend = [sbuf_ref[c] for c in range(ahead)]
        for u in range(KV_UNROLL):
            kv = it * KV_UNROLL + u
            k = k_block(kv)
            k_next = k_block(jnp.minimum(kv + 1, nkv - 1))
            vt = vt_ref[kv]
            for c in range(ncol):
                hv = (c * COL_BLOCK // tq) * vheads // groups
                cs = slice(c * COL_BLOCK, (c + 1) * COL_BLOCK)
                s = pend.pop(0)
                if c + ahead < ncol:
                    pend.append(qk(k, c + ahead))
                elif u + 1 < KV_UNROLL:
                    pend.append(qk(k_next, c + ahead - ncol))
                else:
                    sbuf_ref[c + ahead - ncol] = qk(k_next, c + ahead - ncol)
                vth = vt[hv * VT_ROWS:(hv + 1) * VT_ROWS]
                if bounded:
                    p = (jnp.exp2((s - shift).astype(BF16)) if bf16_exp
                         else jnp.exp2(s - shift).astype(BF16))
                    acc_ref[:, cs] = acc_ref[:, cs] + jnp.dot(vth, p, preferred_element_type=F32)
                else:
                    m_prev = m_ref[:, cs]
                    m_new = jnp.maximum(m_prev, jnp.max(s, axis=0, keepdims=True))
                    alpha = jnp.exp2(m_prev - m_new)
                    p = jnp.exp2(s - m_new).astype(BF16)
                    acc_ref[:, cs] = (acc_ref[:, cs] * alpha
                                      + jnp.dot(vth, p, preferred_element_type=F32))
                    m_ref[:, cs] = m_new
        return carry

    lax.fori_loop(0, nkv // KV_UNROLL, body, 0)

    acc = acc_ref[...]
    norm = acc[0:HEAD_DIM] * (1.0 / acc[HEAD_DIM:HEAD_DIM + 1])
    if diff:
        lp = lamp_ref[...]
        lam = (jnp.exp(jnp.sum(lp[0:1] * lp[1:2], axis=1, keepdims=True))
               - jnp.exp(jnp.sum(lp[2:3] * lp[3:4], axis=1, keepdims=True)) + li_ref[...])
        pieces = [norm[:, 2 * hd * tq:(2 * hd + 1) * tq]
                  - lam * norm[:, (2 * hd + 1) * tq:(2 * hd + 2) * tq] for hd in range(groups // 2)]
    else:
        pieces = [norm[:, r * tq:(r + 1) * tq] for r in range(groups)]
    out = jnp.concatenate(pieces, axis=0).T
    if diff:
        ms = _group_mean_sq(out, gmat_ref[...])
        out = out * lax.rsqrt(ms + RMS_EPS) * ogain_ref[...] * (1.0 - li_ref[...])
    o_ref[...] = out.astype(BF16)


def _flash(qkv, vt, segs, gmat, ogain, lamp, li, shift, *, groups, diff, bounded, bf16_exp,
           cols=FLASH_COLS):
    B, S, _ = qkv.shape
    _, nkv, vrows, tk = vt.shape
    assert nkv % KV_UNROLL == 0 and S % (cols // groups) == 0
    tq = cols // groups
    qs, ks = segs
    return pl.pallas_call(
        functools.partial(_flash_kernel, groups=groups, tq=tq, tk=tk, diff=diff,
                          vheads=vrows // VT_ROWS, bounded=bounded, bf16_exp=bf16_exp),
        out_shape=jax.ShapeDtypeStruct((B, S, SEG), BF16),
        grid=(B, S // tq),
        in_specs=[pl.BlockSpec((None, tq, SEG), lambda b, i: (b, i, qs)),
                  pl.BlockSpec((None, S, SEG), lambda b, i: (b, 0, ks)),
                  pl.BlockSpec((None, nkv, vrows, tk), lambda b, i: (b, 0, 0, 0)),
                  _resident(gmat.shape), _resident(ogain.shape), _resident(lamp.shape),
                  _resident(li.shape), _resident(shift.shape)],
        out_specs=pl.BlockSpec((None, tq, SEG), lambda b, i: (b, i, 0)),
        scratch_shapes=[pltpu.VMEM((SEG, cols), BF16), pltpu.VMEM((1, cols), F32),
                        pltpu.VMEM((VT_ROWS, cols), F32),
                        pltpu.VMEM((QK_AHEAD, tk, COL_BLOCK), F32)],
        compiler_params=_cparams(("parallel", "parallel")),
        name=(("diff_attn" if diff else "gqa_attn")
              + ("" if bf16_exp else "_f32exp" if bounded else "_runmax")),
    )(qkv, qkv, vt, gmat, ogain, lamp, li, shift)


def _na_kernel(q_ref, k_ref, v_ref, tab_ref, o_ref, *, nq, nk, bounded):
    i = pl.program_id(2)
    seq = k_ref.shape[0]
    start = jnp.clip(i * nq - (nk - nq) // 2, 0, seq - nk)
    start = pl.multiple_of(start, (nk - nq) // 2)
    kw = k_ref[pl.ds(start, nk), :]
    vw = v_ref[pl.ds(start, nk), :]
    q = q_ref[...]
    if bounded:
        qt = q.astype(F32).T
        vt = vw.astype(F32).T
        row_h = (lax.broadcasted_iota(jnp.int32, (2 * HEAD_DIM, 1), 0)
                 >> int(math.log2(HEAD_DIM)))
        ones = jnp.ones((VT_ROWS - HEAD_DIM, nk), BF16)
        outs = []
        scores = [jnp.dot(kw, jnp.where(row_h == j, qt, 0.0).astype(BF16),
                          preferred_element_type=F32) for j in range(2)]
        for j in range(2):
            p = jnp.exp2(scores[j] + tab_ref[j]).astype(BF16)
            vth = jnp.concatenate(
                [vt[j * HEAD_DIM:(j + 1) * HEAD_DIM].astype(BF16), ones], axis=0)
            pv = jnp.dot(vth, p, preferred_element_type=F32)
            outs.append(pv[0:HEAD_DIM] * (1.0 / pv[HEAD_DIM:HEAD_DIM + 1]))
        o_ref[...] = jnp.concatenate(outs, axis=0).T.astype(BF16)
        return
    lane_h = lax.broadcasted_iota(jnp.int32, (1, 2 * HEAD_DIM), 1) >> int(math.log2(HEAD_DIM))
    outs = []
    for j in range(2):
        qm = jnp.where(lane_h == j, q, jnp.zeros_like(q))
        s = _dot_nt(qm, kw) + tab_ref[j]
        m = jnp.max(s, axis=1, keepdims=True)
        p = jnp.exp2(s - m)
        l = jnp.sum(p, axis=1, keepdims=True)
        outs.append(jnp.dot(p.astype(BF16), vw, preferred_element_type=F32) * (1.0 / l))
    o_ref[...] = jnp.where(lane_h == 0, outs[0], outs[1]).astype(BF16)


def _na_table(rel_bias, rows, shift=None):
    kh = min(NA_WIN_H, rows)
    n_ro, n_co = 2 * NA_WIN_H - 1, 2 * NA_WIN_W - 1
    c = np.arange(GRID_W)[:, None]
    kc = np.arange(GRID_W)[None, :]
    cs = np.clip(c - NA_WIN_W // 2, 0, GRID_W - NA_WIN_W)
    col_valid = (kc >= cs) & (kc < cs + NA_WIN_W)
    col_sel = (kc - c + (NA_WIN_W - 1))[..., None] == np.arange(n_co)
    row_sel, row_valid, row_off = [], [], []
    for r0 in (0, NA_QROWS, rows - NA_QROWS):
        kr0 = int(np.clip(r0 - (NA_KROWS - NA_QROWS) // 2, 0, rows - NA_KROWS))
        r = r0 + np.arange(NA_QROWS)[:, None]
        kr = kr0 + np.arange(NA_KROWS)[None, :]
        rs = np.clip(r - kh // 2, 0, rows - kh)
        row_valid.append((kr >= rs) & (kr < rs + kh))
        row_off.append(kr - r + (NA_WIN_H - 1))
        row_sel.append(row_off[-1][..., None] == np.arange(n_ro))
    row_sel = jnp.asarray(np.stack(row_sel), F32)
    valid = (np.stack(row_valid)[:, None, :, None, :, None]
             & col_valid[None, None, None, :, None, :])
    hi = lax.Precision.HIGHEST
    cols = jnp.einsum("hab,cdb->hacd", rel_bias.astype(F32), jnp.asarray(col_sel, F32),
                      precision=hi)
    nq, nk = NA_QROWS * GRID_W, NA_KROWS * GRID_W
    if shift is None:
        full = jnp.einsum("vqka,hacd->vhqckd", row_sel, cols, precision=hi)
        table = jnp.where(valid, full * LOG2E, MASK_VALUE)
        return table.reshape(3, rel_bias.shape[0], nq, nk)
    heads = rel_bias.shape[0]
    blocks = jnp.where(col_valid[None, None], cols * LOG2E - shift, MASK_VALUE)
    blocks = jnp.concatenate([blocks.transpose(0, 1, 3, 2),
                              jnp.full((heads, 1, GRID_W, GRID_W), MASK_VALUE, F32)], axis=1)
    block_id = np.where(np.stack(row_valid), np.stack(row_off), n_ro).transpose(0, 2, 1)
    pairs = block_id.reshape(3, NA_KROWS, NA_QROWS // 2, 2)
    uniq, pair_id = np.unique(pairs.reshape(-1, 2), axis=0, return_inverse=True)
    pair_id = pair_id.reshape(3, NA_KROWS, NA_QROWS // 2)
    pair_blocks = jnp.concatenate([jnp.take(blocks, jnp.asarray(uniq[:, 0]), axis=1),
                                   jnp.take(blocks, jnp.asarray(uniq[:, 1]), axis=1)], axis=-1)

    def tile_kernel(pb_ref, o_ref):
        for v in range(3):
            for k in range(NA_KROWS):
                for qp in range(NA_QROWS // 2):
                    o_ref[v, k * GRID_W:(k + 1) * GRID_W,
                          qp * 2 * GRID_W:(qp + 1) * 2 * GRID_W] = pb_ref[int(pair_id[v, k, qp])]

    return pl.pallas_call(
        tile_kernel,
        out_shape=jax.ShapeDtypeStruct((3, heads, nk, nq), F32),
        grid=(heads,),
        in_specs=[pl.BlockSpec((None,) + pair_blocks.shape[1:], lambda h: (h, 0, 0, 0))],
        out_specs=pl.BlockSpec((3, None, nk, nq), lambda h: (0, h, 0, 0)),
        compiler_params=_cparams(("parallel",)),
        name="na_table",
    )(pair_blocks)


def _na(qkv, table, *, bounded):
    B, S, _ = qkv.shape
    nq, nk = NA_QROWS * GRID_W, NA_KROWS * GRID_W
    nblk = S // nq
    hp = 2 * HEAD_DIM

    def variant(i):
        return jnp.where(i == 0, 0, jnp.where(i == nblk - 1, 2, 1))

    return pl.pallas_call(
        functools.partial(_na_kernel, nq=nq, nk=nk, bounded=bounded),
        out_shape=jax.ShapeDtypeStruct((B, S, SEG), BF16),
        grid=(2, B, nblk),
        in_specs=[pl.BlockSpec((None, nq, hp), lambda p, b, i: (b, i, 2 * SEG_NA_Q + p)),
                  pl.BlockSpec((None, S, hp), lambda p, b, i: (b, 0, 2 * SEG_NA_K + p)),
                  pl.BlockSpec((None, S, hp), lambda p, b, i: (b, 0, 2 * SEG_NA_V + p)),
                  pl.BlockSpec((None, 2) + table.shape[2:], lambda p, b, i: (variant(i), p, 0, 0))],
        out_specs=pl.BlockSpec((None, nq, hp), lambda p, b, i: (b, i, p)),
        compiler_params=_cparams(("parallel", "parallel", "arbitrary")),
        name="na_attn" if bounded else "na_attn_rowmax",
    )(qkv, qkv, qkv, table)


def _dil_fixed_shift_kernel(q_ref, k_ref, v_ref, shift_ref, o_ref, den_ref, *, tq, halo):
    j = pl.program_id(2)
    length = k_ref.shape[0]
    win = DIL_SUB + 2 * halo
    lane_h = lax.broadcasted_iota(jnp.int32, (1, SEG), 1) >> int(math.log2(HEAD_DIM))
    for u in range(tq // DIL_SUB):
        q0 = j * tq + u * DIL_SUB
        start = pl.multiple_of(jnp.clip(q0 - halo, 0, length - win), halo)
        kw = k_ref[pl.ds(start, win), :]
        vw = v_ref[pl.ds(start, win), :]
        q = q_ref[u * DIL_SUB:(u + 1) * DIL_SUB, :]
        qm = jnp.concatenate([jnp.where(lane_h == hd, q, jnp.zeros_like(q))
                              for hd in range(DIL_HEADS)], axis=0)
        qpos = q0 + lax.broadcasted_iota(jnp.int32, (DIL_SUB, win), 0)
        kpos = start + lax.broadcasted_iota(jnp.int32, (DIL_SUB, win), 1)
        bias = jnp.where(jnp.abs(kpos - qpos) <= halo, -shift_ref[...], MASK_VALUE)
        s = _dot_nt(qm, kw)
        p = jnp.exp2(s + jnp.concatenate([bias] * DIL_HEADS, axis=0))
        den = jnp.sum(p, axis=1, keepdims=True)
        pv = jnp.dot(p.astype(BF16), vw, preferred_element_type=F32)
        num_u = jnp.zeros((DIL_SUB, SEG), F32)
        den_u = jnp.zeros((DIL_SUB, SEG), F32)
        for hd in range(DIL_HEADS):
            rows = slice(hd * DIL_SUB, (hd + 1) * DIL_SUB)
            num_u = jnp.where(lane_h == hd, pv[rows], num_u)
            den_u = jnp.where(lane_h == hd, den[rows], den_u)
        o_ref[u * DIL_SUB:(u + 1) * DIL_SUB, :] = num_u
        den_ref[u * DIL_SUB:(u + 1) * DIL_SUB, :] = den_u


def _dil_kernel(q_ref, k_ref, v_ref, shift_ref, o_ref, lse_ref, *, tq, halo):
    del shift_ref
    j = pl.program_id(2)
    length = k_ref.shape[0]
    win = tq + 2 * halo
    start = pl.multiple_of(jnp.clip(j * tq - halo, 0, length - win), halo)
    kw = k_ref[pl.ds(start, win), :]
    vw = v_ref[pl.ds(start, win), :]
    q = q_ref[...]
    qpos = j * tq + lax.broadcasted_iota(jnp.int32, (tq, win), 0)
    kpos = start + lax.broadcasted_iota(jnp.int32, (tq, win), 1)
    bias = jnp.where(jnp.abs(kpos - qpos) <= halo, 0.0, MASK_VALUE)
    lane_h = lax.broadcasted_iota(jnp.int32, (1, SEG), 1) >> int(math.log2(HEAD_DIM))
    out = jnp.zeros((tq, SEG), F32)
    lse = jnp.zeros((tq, SEG), F32)
    for hd in range(DIL_HEADS):
        qm = jnp.where(lane_h == hd, q, jnp.zeros_like(q))
        s = _dot_nt(qm, kw) + bias
        m = jnp.max(s, axis=1, keepdims=True)
        p = jnp.exp2(s - m)
        l = jnp.sum(p, axis=1, keepdims=True)
        o = jnp.dot(p.astype(BF16), vw, preferred_element_type=F32) * (1.0 / l)
        out = jnp.where(lane_h == hd, o, out)
        lse = jnp.where(lane_h == hd, m + jnp.log(l) * LOG2E, lse)
    o_ref[...] = out
    lse_ref[...] = lse


def _dilated(src, first_seg, halo, shift, *, bounded):
    length = src.shape[-2]
    tq = min(DIL_TQ_FIXED_SHIFT, length) if bounded else DIL_TQ_ROWMAX
    assert length % tq == 0 and tq % DIL_SUB == 0
    if src.ndim == 3:
        (B, length, _), dil = src.shape, 1
        in_specs = [pl.BlockSpec((None, tq, SEG), lambda b, r, j: (b, j, first_seg)),
                    pl.BlockSpec((None, length, SEG), lambda b, r, j: (b, 0, first_seg + 1)),
                    pl.BlockSpec((None, length, SEG), lambda b, r, j: (b, 0, first_seg + 2))]
    else:
        B, dil, length, _ = src.shape
        in_specs = [pl.BlockSpec((None, None, tq, SEG), lambda b, r, j: (b, r, j, first_seg)),
                    pl.BlockSpec((None, None, length, SEG),
                                 lambda b, r, j: (b, r, 0, first_seg + 1)),
                    pl.BlockSpec((None, None, length, SEG),
                                 lambda b, r, j: (b, r, 0, first_seg + 2))]
    oshape = jax.ShapeDtypeStruct((B, dil, length, SEG), F32)
    ospec = pl.BlockSpec((None, None, tq, SEG), lambda b, r, j: (b, r, j, 0))
    return pl.pallas_call(
        functools.partial(_dil_fixed_shift_kernel if bounded else _dil_kernel, tq=tq, halo=halo),
        out_shape=(oshape, oshape),
        grid=(B, dil, length // tq),
        in_specs=in_specs + [_resident(shift.shape)],
        out_specs=(ospec, ospec),
        compiler_params=_cparams(("parallel", "parallel", "arbitrary")),
        name=f"dil_attn_{dil}" + ("" if bounded else "_rowmax"),
    )(src, src, src, shift)


def _outproj_kernel(x_ref, oa_ref, ob_ref, oc_ref, d0_ref, d1_ref, d2_ref,
                    l0_ref, l1_ref, l2_ref, w_ref, fg_ref, fwg_ref, fwu_ref, fwd_ref,
                    o_ref, cat_ref, *row_scratch, bounded, chunks):
    row_scratch = list(row_scratch)

    def token_rows(ref):
        dil = ref.shape[0]
        if dil == 1:
            return ref[0]
        buf = row_scratch.pop()
        for r in range(dil):
            slab = ref[r]
            for hf in range(SEG // LANES):
                buf[hf, pl.ds(r, ref.shape[1], stride=dil), :] = slab[:, hf * LANES:(hf + 1) * LANES]
        return jnp.concatenate([buf[hf] for hf in range(SEG // LANES)], axis=1)

    l0, l1, l2 = token_rows(l0_ref), token_rows(l1_ref), token_rows(l2_ref)
    if bounded:
        od = ((token_rows(d0_ref) + token_rows(d1_ref) + token_rows(d2_ref))
              * (1.0 / (l0 + l1 + l2)))
    else:
        m = jnp.maximum(jnp.maximum(l0, l1), l2)
        e0, e1, e2 = jnp.exp2(l0 - m), jnp.exp2(l1 - m), jnp.exp2(l2 - m)
        od = ((e0 * token_rows(d0_ref) + e1 * token_rows(d1_ref) + e2 * token_rows(d2_ref))
              * (1.0 / (e0 + e1 + e2)))
    cat_ref[:, 0 * SEG:1 * SEG] = oa_ref[...]
    cat_ref[:, 1 * SEG:2 * SEG] = ob_ref[...]
    cat_ref[:, 2 * SEG:3 * SEG] = oc_ref[...]
    cat_ref[:, 3 * SEG:4 * SEG] = od.astype(BF16)
    mixed = x_ref[...] + jnp.dot(cat_ref[...], w_ref[...], preferred_element_type=F32)
    o_ref[...] = _ffn_tile(mixed, fg_ref, fwg_ref, fwu_ref, fwd_ref, chunks)


def _outproj_ffn(x, oa, ob, oc, dils, w_out, fgain, fwg, fwu, fwd, *, bounded, tm=512):
    B, S, D = x.shape
    F = fwg.shape[1]
    xspec = pl.BlockSpec((None, tm, D), lambda b, i: (b, i, 0))
    sspec = pl.BlockSpec((None, tm, SEG), lambda b, i: (b, i, 0))
    outs = [d[0] for d in dils]
    lses = [d[1] for d in dils]
    dspecs = [pl.BlockSpec((None, a.shape[1], tm // a.shape[1], SEG), lambda b, i: (b, 0, i, 0))
              for a in outs + lses]
    n_scratch = sum(a.shape[1] > 1 for a in outs + lses)
    return pl.pallas_call(
        functools.partial(_outproj_kernel, bounded=bounded, chunks=_ffn_chunks(F)),
        out_shape=jax.ShapeDtypeStruct(x.shape, F32),
        grid=(B, S // tm),
        in_specs=[xspec] + [sspec] * 3 + dspecs
        + [_resident(w_out.shape), _resident((1, D)), _resident((D, F)), _resident((D, F)),
           _resident((F, D))],
        out_specs=xspec,
        scratch_shapes=[pltpu.VMEM((tm, 4 * SEG), BF16)]
        + [pltpu.VMEM((SEG // LANES, tm, LANES), F32)] * n_scratch,
        compiler_params=_cparams(("parallel", "parallel")),
        name="outproj_ffn" if bounded else "outproj_lse_ffn",
    )(x, oa, ob, oc, *outs, *lses, w_out, fgain.reshape(1, D), fwg, fwu, fwd)


def _rope_cs(pos, dim, theta):
    inv_freq = 1.0 / (theta ** (jnp.arange(0, dim, 2, dtype=F32) / dim))
    ang = pos.astype(F32)[:, None] * inv_freq[None, :]
    return jnp.cos(ang), jnp.sin(ang)


def _rope_tables(S):
    pos = jnp.arange(S, dtype=jnp.int32)

    def expand(parts_c, parts_s, reps):
        return (jnp.tile(jnp.concatenate(parts_c, axis=1), (1, reps)),
                jnp.tile(jnp.concatenate(parts_s, axis=1), (1, reps)))

    c, s = _rope_cs(pos, DIFF_QK_DIM // ROPE_FRACTION, ROPE_THETA)
    rest = DIFF_QK_DIM - 2 * c.shape[1]
    df = expand([c, c, jnp.ones((S, rest), F32)], [-s, s, jnp.zeros((S, rest), F32)],
                SEG // DIFF_QK_DIM)
    c, s = _rope_cs(pos, HEAD_DIM // ROPE_FRACTION, ROPE_THETA)
    rest = HEAD_DIM - 2 * c.shape[1]
    dl = expand([c, c, jnp.ones((S, rest), F32)], [-s, s, jnp.zeros((S, rest), F32)],
                SEG // HEAD_DIM)
    cr, sr = _rope_cs(pos // GRID_W, HEAD_DIM // 2, AXIAL_THETA)
    cc, sc = _rope_cs(pos % GRID_W, HEAD_DIM // 2, AXIAL_THETA)
    gq = expand([cr, cr, cc, cc], [-sr, sr, -sc, sc], SEG // HEAD_DIM)
    return df + gq + dl


def _group_matrices():
    lane = np.arange(SEG)
    mats = [(lane[:, None] // g == lane[None, :] // g).astype(np.float32) / g
            for g in (HEAD_DIM, DIFF_QK_DIM)]
    return jnp.asarray(np.stack(mats), BF16)


def _extend_w_in(w_in):
    kv0 = 7 * SEG
    hd = HEAD_DIM
    cols = np.concatenate([
        np.arange(0, kv0),
        kv0 + np.array([0, 0, 1, 1]).repeat(hd) * hd + np.tile(np.arange(hd), 4),
        kv0 + 2 * hd + np.array([0, 0, 1, 1]).repeat(hd) * hd + np.tile(np.arange(hd), 4),
        np.arange(kv0 + 4 * hd, w_in.shape[1]),
    ])
    return w_in[:, cols].astype(BF16)


def _gain_rows(na_q, na_k, df_q, df_k, gq_q, gq_k, dl_q, dl_k):
    def row(g, scale):
        return jnp.tile(g.astype(F32), SEG // g.shape[0]) * scale
    sc64 = HEAD_DIM ** -0.5 * LOG2E
    sc32 = DIFF_QK_DIM ** -0.5 * LOG2E
    return jnp.stack([row(na_q, sc64), row(na_k, 1.0), row(df_q, sc32), row(df_k, 1.0),
                      row(gq_q, sc64), row(gq_k, 1.0), row(dl_q, sc64), row(dl_k, 1.0)])


def kernel(x, ffn1_norm, ffn1_w_gate, ffn1_w_up, ffn1_w_down, mix_norm, w_in, w_out, na_q_norm, na_k_norm, na_rel_bias, diff_q_norm, diff_k_norm, diff_lambda_q1, diff_lambda_k1, diff_lambda_q2, diff_lambda_k2, diff_out_norm, gqa_q_norm, gqa_k_norm, dil_q_norm, dil_k_norm, ffn2_norm, ffn2_w_gate, ffn2_w_up, ffn2_w_down):
    B, S, D = x.shape
    depth = w_in.shape[0]
    rows = S // GRID_W
    tables = _rope_tables(S)
    gmats = _group_matrices()
    ones_row = jnp.ones((1, SEG), F32)
    zero_lamp = jnp.zeros((4, DIFF_QK_DIM), F32)
    zero_li = jnp.zeros((1, 1), F32)

    for l in range(depth):
        x = _ffn(x, ffn1_norm[l], ffn1_w_gate[l].astype(BF16), ffn1_w_up[l].astype(BF16),
                 ffn1_w_down[l].astype(BF16))

        gains = _gain_rows(na_q_norm[l], na_k_norm[l], diff_q_norm[l], diff_k_norm[l],
                           gqa_q_norm[l], gqa_k_norm[l], dil_q_norm[l], dil_k_norm[l])
        qkv, vt_df, vt_gq, *dl_classes = _inproj(x, mix_norm[l], _extend_w_in(w_in[l]), gmats,
                                                 gains, tables)

        na_bias = na_rel_bias[l]
        lamp = jnp.stack([diff_lambda_q1[l], diff_lambda_k1[l], diff_lambda_q2[l],
                          diff_lambda_k2[l]]).astype(F32)
        li = jnp.full((1, 1), 0.8 - 0.6 * math.exp(-0.3 * l), F32)
        ogain = jnp.tile(diff_out_norm[l].astype(F32), SEG // HEAD_DIM).reshape(1, SEG)
        dl_src = {1: (qkv, SEG_DL_Q)}
        dl_src.update({d: (a, 0) for d, a in zip(DILATIONS, dl_classes)})
        w_out_l = w_out[l].astype(BF16)
        ffn2 = (ffn2_norm[l], ffn2_w_gate[l].astype(BF16), ffn2_w_up[l].astype(BF16),
                ffn2_w_down[l].astype(BF16))

        def score_bound(qi, width):
            return width * jnp.max(jnp.abs(gains[qi])) * jnp.max(jnp.abs(gains[qi + 1]))
        bounds = {"na": score_bound(0, HEAD_DIM) + LOG2E * jnp.max(jnp.abs(na_rel_bias[l])),
                  "df": score_bound(2, DIFF_QK_DIM), "gq": score_bound(4, HEAD_DIM),
                  "dl": score_bound(6, HEAD_DIM)}
        shifts = {k: v.reshape(1, 1).astype(F32) for k, v in bounds.items()}

        def mix(bounded, bf16_exp, x=x, qkv=qkv, vt_df=vt_df, vt_gq=vt_gq, dl_src=dl_src,
                shifts=shifts, na_bias=na_bias, ogain=ogain, lamp=lamp, li=li, w_out_l=w_out_l,
                ffn2=ffn2):
            na_table = _na_table(na_bias, rows, shifts["na"][0, 0] if bounded else None)
            o_a = _na(qkv, na_table, bounded=bounded)
            o_b = _flash(qkv, vt_df, (SEG_DF_Q, SEG_DF_K), gmats[0], ogain, lamp, li,
                         shifts["df"], groups=2 * DIFF_HEADS, diff=True, bounded=bounded,
                         bf16_exp=bf16_exp)
            o_c = _flash(qkv, vt_gq, (SEG_GQ_Q, SEG_GQ_K), gmats[0], ones_row, zero_lamp,
                         zero_li, shifts["gq"], groups=GQA_Q_HEADS, diff=False, bounded=bounded,
                         bf16_exp=bf16_exp)
            dils = [_dilated(*dl_src[dil], (window // 2) // dil, shifts["dl"], bounded=bounded)
                    for window, dil in DIL_PAIRS]
            return _outproj_ffn(x, o_a, o_b, o_c, dils, w_out_l, *ffn2, bounded=bounded)

        fixed_ok = functools.reduce(jnp.maximum, bounds.values()) <= MAX_FIXED_SHIFT
        bf16_ok = jnp.maximum(bounds["df"], bounds["gq"]) <= MAX_BF16_EXP_SHIFT
        mode = jnp.where(fixed_ok, jnp.where(bf16_ok, 0, 1), 2)
        x = lax.switch(mode, [functools.partial(mix, True, True),
                              functools.partial(mix, True, False),
                              functools.partial(mix, False, False)])
    return x
```

```python
import functools
import math

import numpy as np
import jax
import jax.numpy as jnp
from jax import lax
from jax.experimental import pallas as pl
from jax.experimental.pallas import tpu as pltpu

F32 = jnp.float32
BF16 = jnp.bfloat16

GRID_W = 64
HEAD_DIM = 64
RMS_EPS = 1e-6
MASK_VALUE = -1e30
LOG2E = math.log2(math.e)

NA_HEADS = 4
NA_WIN_H = 8
NA_WIN_W = 16
DIFF_HEADS = 4
DIFF_QK_DIM = HEAD_DIM // 2
GQA_Q_HEADS = 4
GQA_KV_HEADS = 2
AXIAL_THETA = 10000.0
DIL_HEADS = 4
DIL_PAIRS = ((128, 1), (512, 4), (2048, 16))
DILATIONS = tuple(d for _, d in DIL_PAIRS if d > 1)
ROPE_THETA = 500000.0
ROPE_FRACTION = 4

LANES = 128
SEG = 256
(SEG_NA_Q, SEG_NA_K, SEG_NA_V, SEG_DF_Q, SEG_DF_K, SEG_DF_V,
 SEG_GQ_Q, SEG_GQ_K, SEG_GQ_V, SEG_DL_Q, SEG_DL_K, SEG_DL_V) = range(12)
N_SEG = 12
EXT_WIDTH = N_SEG * SEG

VMEM_LIMIT = 52 * 1024 * 1024

VT_ROWS = HEAD_DIM + 16
COL_BLOCK = 256
DIL_SUB = 128
DIL_TQ_FIXED_SHIFT = 1024
DIL_TQ_ROWMAX = 256
QK_AHEAD = 4
KV_UNROLL = 4
FLASH_COLS = 4096
MAX_FIXED_SHIFT = 40.0
MAX_BF16_EXP_SHIFT = 0.0

NA_QROWS = 8
NA_KROWS = 16


def _cparams(sem):
    return pltpu.CompilerParams(dimension_semantics=sem, vmem_limit_bytes=VMEM_LIMIT)


def _resident(shape):
    nd = len(shape)
    return pl.BlockSpec(shape, lambda *_: (0,) * nd, pipeline_mode=pl.Buffered(1))


def _rms_rows(x, gain):
    ms = jnp.mean(x * x, axis=-1, keepdims=True)
    return x * lax.rsqrt(ms + RMS_EPS) * gain


def _group_mean_sq(y, gmat):
    sq = y * y
    hi = sq.astype(BF16)
    lo = (sq - hi.astype(F32)).astype(BF16)
    return (jnp.dot(hi, gmat, preferred_element_type=F32)
            + jnp.dot(lo, gmat, preferred_element_type=F32))


def _dot_nt(a, b):
    return lax.dot_general(a, b, (((1,), (1,)), ((), ())), preferred_element_type=F32)


def _ffn_tile(x, g_ref, wg_ref, wu_ref, wd_ref, chunks):
    h = _rms_rows(x, g_ref[...]).astype(BF16)
    acc = None
    for c0, c1 in chunks:
        g = jnp.dot(h, wg_ref[:, c0:c1], preferred_element_type=F32)
        u = jnp.dot(h, wu_ref[:, c0:c1], preferred_element_type=F32)
        a = (g * (1.0 / (1.0 + jnp.exp(-g))) * u).astype(BF16)
        d = jnp.dot(a, wd_ref[c0:c1, :], preferred_element_type=F32)
        acc = d if acc is None else acc + d
    return x + 0.5 * acc


def _ffn_kernel(x_ref, g_ref, wg_ref, wu_ref, wd_ref, o_ref, *, chunks):
    o_ref[...] = _ffn_tile(x_ref[...], g_ref, wg_ref, wu_ref, wd_ref, chunks)


def _ffn_chunks(F, fc=512):
    return tuple((c, min(c + fc, F)) for c in range(0, F, fc))


def _ffn(x, gain, wg, wu, wd, *, tm=512):
    B, S, D = x.shape
    F = wg.shape[1]
    chunks = _ffn_chunks(F)
    xspec = pl.BlockSpec((None, tm, D), lambda b, i: (b, i, 0))
    return pl.pallas_call(
        functools.partial(_ffn_kernel, chunks=chunks),
        out_shape=jax.ShapeDtypeStruct(x.shape, F32),
        grid=(B, S // tm),
        in_specs=[xspec, _resident((1, D)), _resident((D, F)), _resident((D, F)),
                  _resident((F, D))],
        out_specs=xspec,
        compiler_params=_cparams(("parallel", "parallel")),
        name="ffn",
    )(x, gain.reshape(1, D), wg, wu, wd)


_SEG_PLAN = {
    SEG_NA_Q: (0, 0, None), SEG_NA_K: (1, 0, None),
    SEG_DF_Q: (2, 1, "df"), SEG_DF_K: (3, 1, "df"),
    SEG_GQ_Q: (4, 0, "gq"), SEG_GQ_K: (5, 0, "gq"),
    SEG_DL_Q: (6, 0, "dl"), SEG_DL_K: (7, 0, "dl"),
}
_ROPE_HALF = {"df": DIFF_QK_DIM // ROPE_FRACTION // 2, "gq": HEAD_DIM // 4,
              "dl": HEAD_DIM // ROPE_FRACTION // 2}


def _inproj_kernel(x_ref, gn_ref, w_ref, gmat_ref, gains_ref,
                   cdf_ref, sdf_ref, cgq_ref, sgq_ref, cdl_ref, sdl_ref,
                   o_ref, vtdf_ref, vtgq_ref, *dl_refs_and_scratch):
    *dl_refs, ys_ref = dl_refs_and_scratch
    tm = x_ref.shape[0]
    h = _rms_rows(x_ref[...], gn_ref[...]).astype(BF16)
    lane = lax.broadcasted_iota(jnp.int32, (1, SEG), 1)
    tables = {"df": (cdf_ref, sdf_ref), "gq": (cgq_ref, sgq_ref), "dl": (cdl_ref, sdl_ref)}
    vt_plan = {SEG_DF_V: (vtdf_ref, tuple(range(DIFF_HEADS))),
               SEG_GQ_V: (vtgq_ref, tuple(2 * j for j in range(GQA_KV_HEADS)))}
    for seg in range(N_SEG):
        y = jnp.dot(h, w_ref[:, seg * SEG:(seg + 1) * SEG], preferred_element_type=F32)
        if seg in vt_plan:
            vt_ref, lane_heads = vt_plan[seg]
            yt = y.T
            ones = jnp.ones((VT_ROWS - HEAD_DIM, yt.shape[1]), BF16)
            for j, lh in enumerate(lane_heads):
                vt_ref[j * VT_ROWS:j * VT_ROWS + HEAD_DIM, :] = (
                    yt[lh * HEAD_DIM:(lh + 1) * HEAD_DIM].astype(BF16))
                vt_ref[j * VT_ROWS + HEAD_DIM:(j + 1) * VT_ROWS, :] = ones
        plan = _SEG_PLAN.get(seg)
        if plan is not None:
            gi, mi, rope = plan
            ms = _group_mean_sq(y, gmat_ref[mi])
            y = y * lax.rsqrt(ms + RMS_EPS) * gains_ref[gi:gi + 1, :]
            if rope is not None:
                half = _ROPE_HALF[rope]
                c_ref, s_ref = tables[rope]
                first = (lane & (2 * half - 1)) < half
                partner = jnp.where(first, pltpu.roll(y, SEG - half, 1), pltpu.roll(y, half, 1))
                y = y * c_ref[...] + partner * s_ref[...]
        o_ref[:, seg * SEG:(seg + 1) * SEG] = y.astype(BF16)
        if SEG_DL_Q <= seg <= SEG_DL_V:
            for hf in range(SEG // LANES):
                ys_ref[hf] = y[:, hf * LANES:(hf + 1) * LANES]
            col = (seg - SEG_DL_Q) * SEG
            for dl_ref, dil in zip(dl_refs, DILATIONS):
                for r in range(dil):
                    for hf in range(SEG // LANES):
                        dl_ref[r, :, col + hf * LANES:col + (hf + 1) * LANES] = (
                            ys_ref[hf, pl.ds(r, tm // dil, stride=dil), :].astype(BF16))


def _inproj(x, gain, w_ext, gmats, gains, tables, *, tm=512):
    B, S, D = x.shape
    tspec = pl.BlockSpec((tm, SEG), lambda i, b: (i, 0))
    vt_rows = (DIFF_HEADS * VT_ROWS, GQA_KV_HEADS * VT_ROWS)
    return pl.pallas_call(
        _inproj_kernel,
        out_shape=(jax.ShapeDtypeStruct((B, S, EXT_WIDTH), BF16),)
        + tuple(jax.ShapeDtypeStruct((B, S // tm, r, tm), BF16) for r in vt_rows)
        + tuple(jax.ShapeDtypeStruct((B, d, S // d, 3 * SEG), BF16) for d in DILATIONS),
        grid=(S // tm, B),
        in_specs=[pl.BlockSpec((None, tm, D), lambda i, b: (b, i, 0)),
                  _resident((1, D)), _resident((D, EXT_WIDTH)),
                  _resident(gmats.shape), _resident(gains.shape)] + [tspec] * 6,
        out_specs=(pl.BlockSpec((None, tm, EXT_WIDTH), lambda i, b: (b, i, 0)),)
        + tuple(pl.BlockSpec((None, None, r, tm), lambda i, b: (b, i, 0, 0)) for r in vt_rows)
        + tuple(pl.BlockSpec((None, d, tm // d, 3 * SEG), lambda i, b: (b, 0, i, 0))
                for d in DILATIONS),
        scratch_shapes=[pltpu.VMEM((SEG // LANES, tm, LANES), F32)],
        compiler_params=_cparams(("parallel", "parallel")),
        name="inproj",
    )(x, gain.reshape(1, D), w_ext, gmats, gains, *tables)


def _flash_kernel(q_ref, k_ref, vt_ref, gmat_ref, ogain_ref, lamp_ref, li_ref, shift_ref, o_ref,
                  qmt_ref, m_ref, acc_ref, sbuf_ref, *, groups, tq, tk, diff, vheads, bounded,
                  bf16_exp):
    gw = SEG // groups
    qt = q_ref[...].astype(F32).T
    row_g = lax.broadcasted_iota(jnp.int32, (SEG, 1), 0) >> int(math.log2(gw))
    for r in range(groups):
        qmt_ref[:, r * tq:(r + 1) * tq] = jnp.where(row_g == r, qt, 0.0).astype(BF16)
    m_ref[...] = jnp.full(m_ref.shape, MASK_VALUE, F32)
    acc_ref[...] = jnp.zeros(acc_ref.shape, F32)
    ncol = groups * tq // COL_BLOCK

    nkv = k_ref.shape[0] // tk
    ahead = sbuf_ref.shape[0]

    def k_block(kv):
        return k_ref[pl.ds(pl.multiple_of(kv * tk, tk), tk), :]

    def qk(k, c):
        return jnp.dot(k, qmt_ref[:, c * COL_BLOCK:(c + 1) * COL_BLOCK],
                       preferred_element_type=F32)

    k0 = k_block(0)
    for c in range(ahead):
        sbuf_ref[c] = qk(k0, c)

    shift = shift_ref[...]

    def body(it, carry):
        pend = [sbuf_ref[c] for c in range(ahead)]
        for u in range(KV_UNROLL):
            kv = it * KV_UNROLL + u
            k = k_block(kv)
            k_next = k_block(jnp.minimum(kv + 1, nkv - 1))
            vt = vt_ref[kv]
            for c in range(ncol):
                hv = (c * COL_BLOCK // tq) * vheads // groups
                cs = slice(c * COL_BLOCK, (c + 1) * COL_BLOCK)
                s = pend.pop(0)
                if c + ahead < ncol:
                    pend.append(qk(k, c + ahead))
                elif u + 1 < KV_UNROLL:
                    pend.append(qk(k_next, c + ahead - ncol))
                else:
                    sbuf_ref[c + ahead - ncol] = qk(k_next, c + ahead - ncol)
                vth = vt[hv * VT_ROWS:(hv + 1) * VT_ROWS]
                if bounded:
                    p = (jnp.exp2((s - shift).astype(BF16)) if bf16_exp
                         else jnp.exp2(s - shift).astype(BF16))
                    acc_ref[:, cs] = acc_ref[:, cs] + jnp.dot(vth, p, preferred_element_type=F32)
                else:
                    m_prev = m_ref[:, cs]
                    m_new = jnp.maximum(m_prev, jnp.max(s, axis=0, keepdims=True))
                    alpha = jnp.exp2(m_prev - m_new)
                    p = jnp.exp2(s - m_new).astype(BF16)
                    acc_ref[:, cs] = (acc_ref[:, cs] * alpha
                                      + jnp.dot(vth, p, preferred_element_type=F32))
                    m_ref[:, cs] = m_new
        return carry

    lax.fori_loop(0, nkv // KV_UNROLL, body, 0)

    acc = acc_ref[...]
    norm = acc[0:HEAD_DIM] * (1.0 / acc[HEAD_DIM:HEAD_DIM + 1])
    if diff:
        lp = lamp_ref[...]
        lam = (jnp.exp(jnp.sum(lp[0:1] * lp[1:2], axis=1, keepdims=True))
               - jnp.exp(jnp.sum(lp[2:3] * lp[3:4], axis=1, keepdims=True)) + li_ref[...])
        pieces = [norm[:, 2 * hd * tq:(2 * hd + 1) * tq]
                  - lam * norm[:, (2 * hd + 1) * tq:(2 * hd + 2) * tq] for hd in range(groups // 2)]
    else:
        pieces = [norm[:, r * tq:(r + 1) * tq] for r in range(groups)]
    out = jnp.concatenate(pieces, axis=0).T
    if diff:
        ms = _group_mean_sq(out, gmat_ref[...])
        out = out * lax.rsqrt(ms + RMS_EPS) * ogain_ref[...] * (1.0 - li_ref[...])
    o_ref[...] = out.astype(BF16)


def _flash(qkv, vt, segs, gmat, ogain, lamp, li, shift, *, groups, diff, bounded, bf16_exp,
           cols=FLASH_COLS):
    B, S, _ = qkv.shape
    _, nkv, vrows, tk = vt.shape
    assert nkv % KV_UNROLL == 0 and S % (cols // groups) == 0
    tq = cols // groups
    qs, ks = segs
    return pl.pallas_call(
        functools.partial(_flash_kernel, groups=groups, tq=tq, tk=tk, diff=diff,
                          vheads=vrows // VT_ROWS, bounded=bounded, bf16_exp=bf16_exp),
        out_shape=jax.ShapeDtypeStruct((B, S, SEG), BF16),
        grid=(B, S // tq),
        in_specs=[pl.BlockSpec((None, tq, SEG), lambda b, i: (b, i, qs)),
                  pl.BlockSpec((None, S, SEG), lambda b, i: (b, 0, ks)),
                  pl.BlockSpec((None, nkv, vrows, tk), lambda b, i: (b, 0, 0, 0)),
                  _resident(gmat.shape), _resident(ogain.shape), _resident(lamp.shape),
                  _resident(li.shape), _resident(shift.shape)],
        out_specs=pl.BlockSpec((None, tq, SEG), lambda b, i: (b, i, 0)),
        scratch_shapes=[pltpu.VMEM((SEG, cols), BF16), pltpu.VMEM((1, cols), F32),
                        pltpu.VMEM((VT_ROWS, cols), F32),
                        pltpu.VMEM((QK_AHEAD, tk, COL_BLOCK), F32)],
        compiler_params=_cparams(("parallel", "parallel")),
        name=(("diff_attn" if diff else "gqa_attn")
              + ("" if bf16_exp else "_f32exp" if bounded else "_runmax")),
    )(qkv, qkv, vt, gmat, ogain, lamp, li, shift)


def _na_kernel(q_ref, k_ref, v_ref, tab_ref, o_ref, *, nq, nk, bounded):
    i = pl.program_id(2)
    seq = k_ref.shape[0]
    start = jnp.clip(i * nq - (nk - nq) // 2, 0, seq - nk)
    start = pl.multiple_of(start, (nk - nq) // 2)
    kw = k_ref[pl.ds(start, nk), :]
    vw = v_ref[pl.ds(start, nk), :]
    q = q_ref[...]
    if bounded:
        qt = q.astype(F32).T
        vt = vw.astype(F32).T
        row_h = (lax.broadcasted_iota(jnp.int32, (2 * HEAD_DIM, 1), 0)
                 >> int(math.log2(HEAD_DIM)))
        ones = jnp.ones((VT_ROWS - HEAD_DIM, nk), BF16)
        outs = []
        scores = [jnp.dot(kw, jnp.where(row_h == j, qt, 0.0).astype(BF16),
                          preferred_element_type=F32) for j in range(2)]
        for j in range(2):
            p = jnp.exp2(scores[j] + tab_ref[j]).astype(BF16)
            vth = jnp.concatenate(
                [vt[j * HEAD_DIM:(j + 1) * HEAD_DIM].astype(BF16), ones], axis=0)
            pv = jnp.dot(vth, p, preferred_element_type=F32)
            outs.append(pv[0:HEAD_DIM] * (1.0 / pv[HEAD_DIM:HEAD_DIM + 1]))
        o_ref[...] = jnp.concatenate(outs, axis=0).T.astype(BF16)
        return
    lane_h = lax.broadcasted_iota(jnp.int32, (1, 2 * HEAD_DIM), 1) >> int(math.log2(HEAD_DIM))
    outs = []
    for j in range(2):
        qm = jnp.where(lane_h == j, q, jnp.zeros_like(q))
        s = _dot_nt(qm, kw) + tab_ref[j]
        m = jnp.max(s, axis=1, keepdims=True)
        p = jnp.exp2(s - m)
        l = jnp.sum(p, axis=1, keepdims=True)
        outs.append(jnp.dot(p.astype(BF16), vw, preferred_element_type=F32) * (1.0 / l))
    o_ref[...] = jnp.where(lane_h == 0, outs[0], outs[1]).astype(BF16)


def _na_table(rel_bias, rows, shift=None):
    kh = min(NA_WIN_H, rows)
    n_ro, n_co = 2 * NA_WIN_H - 1, 2 * NA_WIN_W - 1
    c = np.arange(GRID_W)[:, None]
    kc = np.arange(GRID_W)[None, :]
    cs = np.clip(c - NA_WIN_W // 2, 0, GRID_W - NA_WIN_W)
    col_valid = (kc >= cs) & (kc < cs + NA_WIN_W)
    col_sel = (kc - c + (NA_WIN_W - 1))[..., None] == np.arange(n_co)
    row_sel, row_valid, row_off = [], [], []
    for r0 in (0, NA_QROWS, rows - NA_QROWS):
        kr0 = int(np.clip(r0 - (NA_KROWS - NA_QROWS) // 2, 0, rows - NA_KROWS))
        r = r0 + np.arange(NA_QROWS)[:, None]
        kr = kr0 + np.arange(NA_KROWS)[None, :]
        rs = np.clip(r - kh // 2, 0, rows - kh)
        row_valid.append((kr >= rs) & (kr < rs + kh))
        row_off.append(kr - r + (NA_WIN_H - 1))
        row_sel.append(row_off[-1][..., None] == np.arange(n_ro))
    row_sel = jnp.asarray(np.stack(row_sel), F32)
    valid = (np.stack(row_valid)[:, None, :, None, :, None]
             & col_valid[None, None, None, :, None, :])
    hi = lax.Precision.HIGHEST
    cols = jnp.einsum("hab,cdb->hacd", rel_bias.astype(F32), jnp.asarray(col_sel, F32),
                      precision=hi)
    nq, nk = NA_QROWS * GRID_W, NA_KROWS * GRID_W
    if shift is None:
        full = jnp.einsum("vqka,hacd->vhqckd", row_sel, cols, precision=hi)
        table = jnp.where(valid, full * LOG2E, MASK_VALUE)
        return table.reshape(3, rel_bias.shape[0], nq, nk)
    heads = rel_bias.shape[0]
    blocks = jnp.where(col_valid[None, None], cols * LOG2E - shift, MASK_VALUE)
    blocks = jnp.concatenate([blocks.transpose(0, 1, 3, 2),
                              jnp.full((heads, 1, GRID_W, GRID_W), MASK_VALUE, F32)], axis=1)
    block_id = np.where(np.stack(row_valid), np.stack(row_off), n_ro).transpose(0, 2, 1)
    pairs = block_id.reshape(3, NA_KROWS, NA_QROWS // 2, 2)
    uniq, pair_id = np.unique(pairs.reshape(-1, 2), axis=0, return_inverse=True)
    pair_id = pair_id.reshape(3, NA_KROWS, NA_QROWS // 2)
    pair_blocks = jnp.concatenate([jnp.take(blocks, jnp.asarray(uniq[:, 0]), axis=1),
                                   jnp.take(blocks, jnp.asarray(uniq[:, 1]), axis=1)], axis=-1)

    def tile_kernel(pb_ref, o_ref):
        for v in range(3):
            for k in range(NA_KROWS):
                for qp in range(NA_QROWS // 2):
                    o_ref[v, k * GRID_W:(k + 1) * GRID_W,
                          qp * 2 * GRID_W:(qp + 1) * 2 * GRID_W] = pb_ref[int(pair_id[v, k, qp])]

    return pl.pallas_call(
        tile_kernel,
        out_shape=jax.ShapeDtypeStruct((3, heads, nk, nq), F32),
        grid=(heads,),
        in_specs=[pl.BlockSpec((None,) + pair_blocks.shape[1:], lambda h: (h, 0, 0, 0))],
        out_specs=pl.BlockSpec((3, None, nk, nq), lambda h: (0, h, 0, 0)),
        compiler_params=_cparams(("parallel",)),
        name="na_table",
    )(pair_blocks)


def _na(qkv, table, *, bounded):
    B, S, _ = qkv.shape
    nq, nk = NA_QROWS * GRID_W, NA_KROWS * GRID_W
    nblk = S // nq
    hp = 2 * HEAD_DIM

    def variant(i):
        return jnp.where(i == 0, 0, jnp.where(i == nblk - 1, 2, 1))

    return pl.pallas_call(
        functools.partial(_na_kernel, nq=nq, nk=nk, bounded=bounded),
        out_shape=jax.ShapeDtypeStruct((B, S, SEG), BF16),
        grid=(2, B, nblk),
        in_specs=[pl.BlockSpec((None, nq, hp), lambda p, b, i: (b, i, 2 * SEG_NA_Q + p)),
                  pl.BlockSpec((None, S, hp), lambda p, b, i: (b, 0, 2 * SEG_NA_K + p)),
                  pl.BlockSpec((None, S, hp), lambda p, b, i: (b, 0, 2 * SEG_NA_V + p)),
                  pl.BlockSpec((None, 2) + table.shape[2:], lambda p, b, i: (variant(i), p, 0, 0))],
        out_specs=pl.BlockSpec((None, nq, hp), lambda p, b, i: (b, i, p)),
        compiler_params=_cparams(("parallel", "parallel", "arbitrary")),
        name="na_attn" if bounded else "na_attn_rowmax",
    )(qkv, qkv, qkv, table)


def _dil_fixed_shift_kernel(q_ref, k_ref, v_ref, shift_ref, o_ref, den_ref, *, tq, halo):
    j = pl.program_id(2)
    length = k_ref.shape[0]
    win = DIL_SUB + 2 * halo
    lane_h = lax.broadcasted_iota(jnp.int32, (1, SEG), 1) >> int(math.log2(HEAD_DIM))
    for u in range(tq // DIL_SUB):
        q0 = j * tq + u * DIL_SUB
        start = pl.multiple_of(jnp.clip(q0 - halo, 0, length - win), halo)
        kw = k_ref[pl.ds(start, win), :]
        vw = v_ref[pl.ds(start, win), :]
        q = q_ref[u * DIL_SUB:(u + 1) * DIL_SUB, :]
        qm = jnp.concatenate([jnp.where(lane_h == hd, q, jnp.zeros_like(q))
                              for hd in range(DIL_HEADS)], axis=0)
        qpos = q0 + lax.broadcasted_iota(jnp.int32, (DIL_SUB, win), 0)
        kpos = start + lax.broadcasted_iota(jnp.int32, (DIL_SUB, win), 1)
        bias = jnp.where(jnp.abs(kpos - qpos) <= halo, -shift_ref[...], MASK_VALUE)
        s = _dot_nt(qm, kw)
        p = jnp.exp2(s + jnp.concatenate([bias] * DIL_HEADS, axis=0))
        den = jnp.sum(p, axis=1, keepdims=True)
        pv = jnp.dot(p.astype(BF16), vw, preferred_element_type=F32)
        num_u = jnp.zeros((DIL_SUB, SEG), F32)
        den_u = jnp.zeros((DIL_SUB, SEG), F32)
        for hd in range(DIL_HEADS):
            rows = slice(hd * DIL_SUB, (hd + 1) * DIL_SUB)
            num_u = jnp.where(lane_h == hd, pv[rows], num_u)
            den_u = jnp.where(lane_h == hd, den[rows], den_u)
        o_ref[u * DIL_SUB:(u + 1) * DIL_SUB, :] = num_u
        den_ref[u * DIL_SUB:(u + 1) * DIL_SUB, :] = den_u


def _dil_kernel(q_ref, k_ref, v_ref, shift_ref, o_ref, lse_ref, *, tq, halo):
    del shift_ref
    j = pl.program_id(2)
    length = k_ref.shape[0]
    win = tq + 2 * halo
    start = pl.multiple_of(jnp.clip(j * tq - halo, 0, length - win), halo)
    kw = k_ref[pl.ds(start, win), :]
    vw = v_ref[pl.ds(start, win), :]
    q = q_ref[...]
    qpos = j * tq + lax.broadcasted_iota(jnp.int32, (tq, win), 0)
    kpos = start + lax.broadcasted_iota(jnp.int32, (tq, win), 1)
    bias = jnp.where(jnp.abs(kpos - qpos) <= halo, 0.0, MASK_VALUE)
    lane_h = lax.broadcasted_iota(jnp.int32, (1, SEG), 1) >> int(math.log2(HEAD_DIM))
    out = jnp.zeros((tq, SEG), F32)
    lse = jnp.zeros((tq, SEG), F32)
    for hd in range(DIL_HEADS):
        qm = jnp.where(lane_h == hd, q, jnp.zeros_like(q))
        s = _dot_nt(qm, kw) + bias
        m = jnp.max(s, axis=1, keepdims=True)
        p = jnp.exp2(s - m)
        l = jnp.sum(p, axis=1, keepdims=True)
        o = jnp.dot(p.astype(BF16), vw, preferred_element_type=F32) * (1.0 / l)
        out = jnp.where(lane_h == hd, o, out)
        lse = jnp.where(lane_h == hd, m + jnp.log(l) * LOG2E, lse)
    o_ref[...] = out
    lse_ref[...] = lse


def _dilated(src, first_seg, halo, shift, *, bounded):
    length = src.shape[-2]
    tq = min(DIL_TQ_FIXED_SHIFT, length) if bounded else DIL_TQ_ROWMAX
    assert length % tq == 0 and tq % DIL_SUB == 0
    if src.ndim == 3:
        (B, length, _), dil = src.shape, 1
        in_specs = [pl.BlockSpec((None, tq, SEG), lambda b, r, j: (b, j, first_seg)),
                    pl.BlockSpec((None, length, SEG), lambda b, r, j: (b, 0, first_seg + 1)),
                    pl.BlockSpec((None, length, SEG), lambda b, r, j: (b, 0, first_seg + 2))]
    else:
        B, dil, length, _ = src.shape
        in_specs = [pl.BlockSpec((None, None, tq, SEG), lambda b, r, j: (b, r, j, first_seg)),
                    pl.BlockSpec((None, None, length, SEG),
                                 lambda b, r, j: (b, r, 0, first_seg + 1)),
                    pl.BlockSpec((None, None, length, SEG),
                                 lambda b, r, j: (b, r, 0, first_seg + 2))]
    oshape = jax.ShapeDtypeStruct((B, dil, length, SEG), F32)
    ospec = pl.BlockSpec((None, None, tq, SEG), lambda b, r, j: (b, r, j, 0))
    return pl.pallas_call(
        functools.partial(_dil_fixed_shift_kernel if bounded else _dil_kernel, tq=tq, halo=halo),
        out_shape=(oshape, oshape),
        grid=(B, dil, length // tq),
        in_specs=in_specs + [_resident(shift.shape)],
        out_specs=(ospec, ospec),
        compiler_params=_cparams(("parallel", "parallel", "arbitrary")),
        name=f"dil_attn_{dil}" + ("" if bounded else "_rowmax"),
    )(src, src, src, shift)


def _outproj_kernel(x_ref, oa_ref, ob_ref, oc_ref, d0_ref, d1_ref, d2_ref,
                    l0_ref, l1_ref, l2_ref, w_ref, fg_ref, fwg_ref, fwu_ref, fwd_ref,
                    o_ref, cat_ref, *row_scratch, bounded, chunks):
    row_scratch = list(row_scratch)

    def token_rows(ref):
        dil = ref.shape[0]
        if dil == 1:
            return ref[0]
        buf = row_scratch.pop()
        for r in range(dil):
            slab = ref[r]
            for hf in range(SEG // LANES):
                buf[hf, pl.ds(r, ref.shape[1], stride=dil), :] = slab[:, hf * LANES:(hf + 1) * LANES]
        return jnp.concatenate([buf[hf] for hf in range(SEG // LANES)], axis=1)

    l0, l1, l2 = token_rows(l0_ref), token_rows(l1_ref), token_rows(l2_ref)
    if bounded:
        od = ((token_rows(d0_ref) + token_rows(d1_ref) + token_rows(d2_ref))
              * (1.0 / (l0 + l1 + l2)))
    else:
        m = jnp.maximum(jnp.maximum(l0, l1), l2)
        e0, e1, e2 = jnp.exp2(l0 - m), jnp.exp2(l1 - m), jnp.exp2(l2 - m)
        od = ((e0 * token_rows(d0_ref) + e1 * token_rows(d1_ref) + e2 * token_rows(d2_ref))
              * (1.0 / (e0 + e1 + e2)))
    cat_ref[:, 0 * SEG:1 * SEG] = oa_ref[...]
    cat_ref[:, 1 * SEG:2 * SEG] = ob_ref[...]
    cat_ref[:, 2 * SEG:3 * SEG] = oc_ref[...]
    cat_ref[:, 3 * SEG:4 * SEG] = od.astype(BF16)
    mixed = x_ref[...] + jnp.dot(cat_ref[...], w_ref[...], preferred_element_type=F32)
    o_ref[...] = _ffn_tile(mixed, fg_ref, fwg_ref, fwu_ref, fwd_ref, chunks)


def _outproj_ffn(x, oa, ob, oc, dils, w_out, fgain, fwg, fwu, fwd, *, bounded, tm=512):
    B, S, D = x.shape
    F = fwg.shape[1]
    xspec = pl.BlockSpec((None, tm, D), lambda b, i: (b, i, 0))
    sspec = pl.BlockSpec((None, tm, SEG), lambda b, i: (b, i, 0))
    outs = [d[0] for d in dils]
    lses = [d[1] for d in dils]
    dspecs = [pl.BlockSpec((None, a.shape[1], tm // a.shape[1], SEG), lambda b, i: (b, 0, i, 0))
              for a in outs + lses]
    n_scratch = sum(a.shape[1] > 1 for a in outs + lses)
    return pl.pallas_call(
        functools.partial(_outproj_kernel, bounded=bounded, chunks=_ffn_chunks(F)),
        out_shape=jax.ShapeDtypeStruct(x.shape, F32),
        grid=(B, S // tm),
        in_specs=[xspec] + [sspec] * 3 + dspecs
        + [_resident(w_out.shape), _resident((1, D)), _resident((D, F)), _resident((D, F)),
           _resident((F, D))],
        out_specs=xspec,
        scratch_shapes=[pltpu.VMEM((tm, 4 * SEG), BF16)]
        + [pltpu.VMEM((SEG // LANES, tm, LANES), F32)] * n_scratch,
        compiler_params=_cparams(("parallel", "parallel")),
        name="outproj_ffn" if bounded else "outproj_lse_ffn",
    )(x, oa, ob, oc, *outs, *lses, w_out, fgain.reshape(1, D), fwg, fwu, fwd)


def _rope_cs(pos, dim, theta):
    inv_freq = 1.0 / (theta ** (jnp.arange(0, dim, 2, dtype=F32) / dim))
    ang = pos.astype(F32)[:, None] * inv_freq[None, :]
    return jnp.cos(ang), jnp.sin(ang)


def _rope_tables(S):
    pos = jnp.arange(S, dtype=jnp.int32)

    def expand(parts_c, parts_s, reps):
        return (jnp.tile(jnp.concatenate(parts_c, axis=1), (1, reps)),
                jnp.tile(jnp.concatenate(parts_s, axis=1), (1, reps)))

    c, s = _rope_cs(pos, DIFF_QK_DIM // ROPE_FRACTION, ROPE_THETA)
    rest = DIFF_QK_DIM - 2 * c.shape[1]
    df = expand([c, c, jnp.ones((S, rest), F32)], [-s, s, jnp.zeros((S, rest), F32)],
                SEG // DIFF_QK_DIM)
    c, s = _rope_cs(pos, HEAD_DIM // ROPE_FRACTION, ROPE_THETA)
    rest = HEAD_DIM - 2 * c.shape[1]
    dl = expand([c, c, jnp.ones((S, rest), F32)], [-s, s, jnp.zeros((S, rest), F32)],
                SEG // HEAD_DIM)
    cr, sr = _rope_cs(pos // GRID_W, HEAD_DIM // 2, AXIAL_THETA)
    cc, sc = _rope_cs(pos % GRID_W, HEAD_DIM // 2, AXIAL_THETA)
    gq = expand([cr, cr, cc, cc], [-sr, sr, -sc, sc], SEG // HEAD_DIM)
    return df + gq + dl


def _group_matrices():
    lane = np.arange(SEG)
    mats = [(lane[:, None] // g == lane[None, :] // g).astype(np.float32) / g
            for g in (HEAD_DIM, DIFF_QK_DIM)]
    return jnp.asarray(np.stack(mats), BF16)


def _extend_w_in(w_in):
    kv0 = 7 * SEG
    hd = HEAD_DIM
    cols = np.concatenate([
        np.arange(0, kv0),
        kv0 + np.array([0, 0, 1, 1]).repeat(hd) * hd + np.tile(np.arange(hd), 4),
        kv0 + 2 * hd + np.array([0, 0, 1, 1]).repeat(hd) * hd + np.tile(np.arange(hd), 4),
        np.arange(kv0 + 4 * hd, w_in.shape[1]),
    ])
    return w_in[:, cols].astype(BF16)


def _gain_rows(na_q, na_k, df_q, df_k, gq_q, gq_k, dl_q, dl_k):
    def row(g, scale):
        return jnp.tile(g.astype(F32), SEG // g.shape[0]) * scale
    sc64 = HEAD_DIM ** -0.5 * LOG2E
    sc32 = DIFF_QK_DIM ** -0.5 * LOG2E
    return jnp.stack([row(na_q, sc64), row(na_k, 1.0), row(df_q, sc32), row(df_k, 1.0),
                      row(gq_q, sc64), row(gq_k, 1.0), row(dl_q, sc64), row(dl_k, 1.0)])


def kernel(x, ffn1_norm, ffn1_w_gate, ffn1_w_up, ffn1_w_down, mix_norm, w_in, w_out, na_q_norm, na_k_norm, na_rel_bias, diff_q_norm, diff_k_norm, diff_lambda_q1, diff_lambda_k1, diff_lambda_q2, diff_lambda_k2, diff_out_norm, gqa_q_norm, gqa_k_norm, dil_q_norm, dil_k_norm, ffn2_norm, ffn2_w_gate, ffn2_w_up, ffn2_w_down):
    B, S, D = x.shape
    depth = w_in.shape[0]
    rows = S // GRID_W
    tables = _rope_tables(S)
    gmats = _group_matrices()
    ones_row = jnp.ones((1, SEG), F32)
    zero_lamp = jnp.zeros((4, DIFF_QK_DIM), F32)
    zero_li = jnp.zeros((1, 1), F32)

    for l in range(depth):
        x = _ffn(x, ffn1_norm[l], ffn1_w_gate[l].astype(BF16), ffn1_w_up[l].astype(BF16),
                 ffn1_w_down[l].astype(BF16))

        gains = _gain_rows(na_q_norm[l], na_k_norm[l], diff_q_norm[l], diff_k_norm[l],
                           gqa_q_norm[l], gqa_k_norm[l], dil_q_norm[l], dil_k_norm[l])
        qkv, vt_df, vt_gq, *dl_classes = _inproj(x, mix_norm[l], _extend_w_in(w_in[l]), gmats,
                                                 gains, tables)

        na_bias = na_rel_bias[l]
        lamp = jnp.stack([diff_lambda_q1[l], diff_lambda_k1[l], diff_lambda_q2[l],
                          diff_lambda_k2[l]]).astype(F32)
        li = jnp.full((1, 1), 0.8 - 0.6 * math.exp(-0.3 * l), F32)
        ogain = jnp.tile(diff_out_norm[l].astype(F32), SEG // HEAD_DIM).reshape(1, SEG)
        dl_src = {1: (qkv, SEG_DL_Q)}
        dl_src.update({d: (a, 0) for d, a in zip(DILATIONS, dl_classes)})
        w_out_l = w_out[l].astype(BF16)
        ffn2 = (ffn2_norm[l], ffn2_w_gate[l].astype(BF16), ffn2_w_up[l].astype(BF16),
                ffn2_w_down[l].astype(BF16))

        def score_bound(qi, width):
            return width * jnp.max(jnp.abs(gains[qi])) * jnp.max(jnp.abs(gains[qi + 1]))
        bounds = {"na": score_bound(0, HEAD_DIM) + LOG2E * jnp.max(jnp.abs(na_rel_bias[l])),
                  "df": score_bound(2, DIFF_QK_DIM), "gq": score_bound(4, HEAD_DIM),
                  "dl": score_bound(6, HEAD_DIM)}
        shifts = {k: v.reshape(1, 1).astype(F32) for k, v in bounds.items()}

        def mix(bounded, bf16_exp, x=x, qkv=qkv, vt_df=vt_df, vt_gq=vt_gq, dl_src=dl_src,
                shifts=shifts, na_bias=na_bias, ogain=ogain, lamp=lamp, li=li, w_out_l=w_out_l,
                ffn2=ffn2):
            na_table = _na_table(na_bias, rows, shifts["na"][0, 0] if bounded else None)
            o_a = _na(qkv, na_table, bounded=bounded)
            o_b = _flash(qkv, vt_df, (SEG_DF_Q, SEG_DF_K), gmats[0], ogain, lamp, li,
                         shifts["df"], groups=2 * DIFF_HEADS, diff=True, bounded=bounded,
                         bf16_exp=bf16_exp)
            o_c = _flash(qkv, vt_gq, (SEG_GQ_Q, SEG_GQ_K), gmats[0], ones_row, zero_lamp,
                         zero_li, shifts["gq"], groups=GQA_Q_HEADS, diff=False, bounded=bounded,
                         bf16_exp=bf16_exp)
            dils = [_dilated(*dl_src[dil], (window // 2) // dil, shifts["dl"], bounded=bounded)
                    for window, dil in DIL_PAIRS]
            return _outproj_ffn(x, o_a, o_b, o_c, dils, w_out_l, *ffn2, bounded=bounded)

        fixed_ok = functools.reduce(jnp.maximum, bounds.values()) <= MAX_FIXED_SHIFT
        bf16_ok = jnp.maximum(bounds["df"], bounds["gq"]) <= MAX_BF16_EXP_SHIFT
        mode = jnp.where(fixed_ok, jnp.where(bf16_ok, 0, 1), 2)
        x = lax.switch(mode, [functools.partial(mix, True, True),
                              functools.partial(mix, True, False),
                              functools.partial(mix, False, False)])
    return x
```

```python
import functools
import math

import numpy as np
import jax
import jax.numpy as jnp
from jax import lax
from jax.experimental import pallas as pl
from jax.experimental.pallas import tpu as pltpu

F32 = jnp.float32
BF16 = jnp.bfloat16

GRID_W = 64
HEAD_DIM = 64
RMS_EPS = 1e-6
MASK_VALUE = -1e30
LOG2E = math.log2(math.e)

NA_HEADS = 4
NA_WIN_H = 8
NA_WIN_W = 16
DIFF_HEADS = 4
DIFF_QK_DIM = HEAD_DIM // 2
GQA_Q_HEADS = 4
GQA_KV_HEADS = 2
AXIAL_THETA = 10000.0
DIL_HEADS = 4
DIL_PAIRS = ((128, 1), (512, 4), (2048, 16))
DILATIONS = tuple(d for _, d in DIL_PAIRS if d > 1)
ROPE_THETA = 500000.0
ROPE_FRACTION = 4

LANES = 128
SEG = 256
(SEG_NA_Q, SEG_NA_K, SEG_NA_V, SEG_DF_Q, SEG_DF_K, SEG_DF_V,
 SEG_GQ_Q, SEG_GQ_K, SEG_GQ_V, SEG_DL_Q, SEG_DL_K, SEG_DL_V) = range(12)
N_SEG = 12
EXT_WIDTH = N_SEG * SEG

VMEM_LIMIT = 52 * 1024 * 1024

VT_ROWS = HEAD_DIM + 16
COL_BLOCK = 256
DIL_SUB = 128
DIL_TQ_FIXED_SHIFT = 1024
DIL_TQ_ROWMAX = 256
QK_AHEAD = 4
KV_UNROLL = 4
FLASH_COLS = 4096
MAX_FIXED_SHIFT = 40.0
MAX_BF16_EXP_SHIFT = 16.0

NA_QROWS = 8
NA_KROWS = 16


def _cparams(sem):
    return pltpu.CompilerParams(dimension_semantics=sem, vmem_limit_bytes=VMEM_LIMIT)


def _resident(shape):
    nd = len(shape)
    return pl.BlockSpec(shape, lambda *_: (0,) * nd, pipeline_mode=pl.Buffered(1))


def _rms_rows(x, gain):
    ms = jnp.mean(x * x, axis=-1, keepdims=True)
    return x * lax.rsqrt(ms + RMS_EPS) * gain


def _group_mean_sq(y, gmat):
    sq = y * y
    hi = sq.astype(BF16)
    lo = (sq - hi.astype(F32)).astype(BF16)
    return (jnp.dot(hi, gmat, preferred_element_type=F32)
            + jnp.dot(lo, gmat, preferred_element_type=F32))


def _dot_nt(a, b):
    return lax.dot_general(a, b, (((1,), (1,)), ((), ())), preferred_element_type=F32)


def _ffn_tile(x, g_ref, wg_ref, wu_ref, wd_ref, chunks):
    h = _rms_rows(x, g_ref[...]).astype(BF16)
    acc = None
    for c0, c1 in chunks:
        g = jnp.dot(h, wg_ref[:, c0:c1], preferred_element_type=F32)
        u = jnp.dot(h, wu_ref[:, c0:c1], preferred_element_type=F32)
        a = (g * (1.0 / (1.0 + jnp.exp(-g))) * u).astype(BF16)
        d = jnp.dot(a, wd_ref[c0:c1, :], preferred_element_type=F32)
        acc = d if acc is None else acc + d
    return x + 0.5 * acc


def _ffn_kernel(x_ref, g_ref, wg_ref, wu_ref, wd_ref, o_ref, *, chunks):
    o_ref[...] = _ffn_tile(x_ref[...], g_ref, wg_ref, wu_ref, wd_ref, chunks)


def _ffn_chunks(F, fc=512):
    return tuple((c, min(c + fc, F)) for c in range(0, F, fc))


def _ffn(x, gain, wg, wu, wd, *, tm=512):
    B, S, D = x.shape
    F = wg.shape[1]
    chunks = _ffn_chunks(F)
    xspec = pl.BlockSpec((None, tm, D), lambda b, i: (b, i, 0))
    return pl.pallas_call(
        functools.partial(_ffn_kernel, chunks=chunks),
        out_shape=jax.ShapeDtypeStruct(x.shape, F32),
        grid=(B, S // tm),
        in_specs=[xspec, _resident((1, D)), _resident((D, F)), _resident((D, F)),
                  _resident((F, D))],
        out_specs=xspec,
        compiler_params=_cparams(("parallel", "parallel")),
        name="ffn",
    )(x, gain.reshape(1, D), wg, wu, wd)


_SEG_PLAN = {
    SEG_NA_Q: (0, 0, None), SEG_NA_K: (1, 0, None),
    SEG_DF_Q: (2, 1, "df"), SEG_DF_K: (3, 1, "df"),
    SEG_GQ_Q: (4, 0, "gq"), SEG_GQ_K: (5, 0, "gq"),
    SEG_DL_Q: (6, 0, "dl"), SEG_DL_K: (7, 0, "dl"),
}
_ROPE_HALF = {"df": DIFF_QK_DIM // ROPE_FRACTION // 2, "gq": HEAD_DIM // 4,
              "dl": HEAD_DIM // ROPE_FRACTION // 2}


def _inproj_kernel(x_ref, gn_ref, w_ref, gmat_ref, gains_ref,
                   cdf_ref, sdf_ref, cgq_ref, sgq_ref, cdl_ref, sdl_ref,
                   fg_ref, fwg_ref, fwu_ref, fwd_ref,
                   xo_ref, o_ref, vtdf_ref, vtgq_ref, *dl_refs_and_scratch, chunks):
    *dl_refs, ys_ref = dl_refs_and_scratch
    tm = x_ref.shape[0]
    x1 = _ffn_tile(x_ref[...], fg_ref, fwg_ref, fwu_ref, fwd_ref, chunks)
    xo_ref[...] = x1
    h = _rms_rows(x1, gn_ref[...]).astype(BF16)
    lane = lax.broadcasted_iota(jnp.int32, (1, SEG), 1)
    tables = {"df": (cdf_ref, sdf_ref), "gq": (cgq_ref, sgq_ref), "dl": (cdl_ref, sdl_ref)}
    vt_plan = {SEG_DF_V: (vtdf_ref, tuple(range(DIFF_HEADS))),
               SEG_GQ_V: (vtgq_ref, tuple(2 * j for j in range(GQA_KV_HEADS)))}
    for seg in range(N_SEG):
        y = jnp.dot(h, w_ref[:, seg * SEG:(seg + 1) * SEG], preferred_element_type=F32)
        if seg in vt_plan:
            vt_ref, lane_heads = vt_plan[seg]
            yt = y.T
            ones = jnp.ones((VT_ROWS - HEAD_DIM, yt.shape[1]), BF16)
            for j, lh in enumerate(lane_heads):
                vt_ref[j * VT_ROWS:j * VT_ROWS + HEAD_DIM, :] = (
                    yt[lh * HEAD_DIM:(lh + 1) * HEAD_DIM].astype(BF16))
                vt_ref[j * VT_ROWS + HEAD_DIM:(j + 1) * VT_ROWS, :] = ones
        plan = _SEG_PLAN.get(seg)
        if plan is not None:
            gi, mi, rope = plan
            ms = _group_mean_sq(y, gmat_ref[mi])
            y = y * lax.rsqrt(ms + RMS_EPS) * gains_ref[gi:gi + 1, :]
            if rope is not None:
                half = _ROPE_HALF[rope]
                c_ref, s_ref = tables[rope]
                first = (lane & (2 * half - 1)) < half
                partner = jnp.where(first, pltpu.roll(y, SEG - half, 1), pltpu.roll(y, half, 1))
                y = y * c_ref[...] + partner * s_ref[...]
        o_ref[:, seg * SEG:(seg + 1) * SEG] = y.astype(BF16)
        if SEG_DL_Q <= seg <= SEG_DL_V:
            for hf in range(SEG // LANES):
                ys_ref[hf] = y[:, hf * LANES:(hf + 1) * LANES]
            col = (seg - SEG_DL_Q) * SEG
            for dl_ref, dil in zip(dl_refs, DILATIONS):
                for r in range(dil):
                    for hf in range(SEG // LANES):
                        dl_ref[r, :, col + hf * LANES:col + (hf + 1) * LANES] = (
                            ys_ref[hf, pl.ds(r, tm // dil, stride=dil), :].astype(BF16))


def _ffn_inproj(x, fgain, fwg, fwu, fwd, gain, w_ext, gmats, gains, tables, *, tm=512):
    B, S, D = x.shape
    F = fwg.shape[1]
    tspec = pl.BlockSpec((tm, SEG), lambda i, b: (i, 0))
    vt_rows = (DIFF_HEADS * VT_ROWS, GQA_KV_HEADS * VT_ROWS)
    xspec = pl.BlockSpec((None, tm, D), lambda i, b: (b, i, 0))
    return pl.pallas_call(
        functools.partial(_inproj_kernel, chunks=_ffn_chunks(F)),
        out_shape=(jax.ShapeDtypeStruct(x.shape, F32),
                   jax.ShapeDtypeStruct((B, S, EXT_WIDTH), BF16),)
        + tuple(jax.ShapeDtypeStruct((B, S // tm, r, tm), BF16) for r in vt_rows)
        + tuple(jax.ShapeDtypeStruct((B, d, S // d, 3 * SEG), BF16) for d in DILATIONS),
        grid=(S // tm, B),
        in_specs=[xspec, _resident((1, D)), _resident((D, EXT_WIDTH)),
                  _resident(gmats.shape), _resident(gains.shape)] + [tspec] * 6
        + [_resident((1, D)), _resident((D, F)), _resident((D, F)), _resident((F, D))],
        out_specs=(xspec, pl.BlockSpec((None, tm, EXT_WIDTH), lambda i, b: (b, i, 0)),)
        + tuple(pl.BlockSpec((None, None, r, tm), lambda i, b: (b, i, 0, 0)) for r in vt_rows)
        + tuple(pl.BlockSpec((None, d, tm // d, 3 * SEG), lambda i, b: (b, 0, i, 0))
                for d in DILATIONS),
        scratch_shapes=[pltpu.VMEM((SEG // LANES, tm, LANES), F32)],
        compiler_params=_cparams(("parallel", "parallel")),
        name="ffn_inproj",
    )(x, gain.reshape(1, D), w_ext, gmats, gains, *tables, fgain.reshape(1, D), fwg, fwu, fwd)


def _flash_kernel(q_ref, k_ref, vt_ref, gmat_ref, ogain_ref, lamp_ref, li_ref, shift_ref, o_ref,
                  qmt_ref, m_ref, acc_ref, sbuf_ref, *, groups, tq, tk, diff, vheads, bounded,
                  bf16_exp):
    gw = SEG // groups
    qt = q_ref[...].astype(F32).T
    row_g = lax.broadcasted_iota(jnp.int32, (SEG, 1), 0) >> int(math.log2(gw))
    for r in range(groups):
        qmt_ref[:, r * tq:(r + 1) * tq] = jnp.where(row_g == r, qt, 0.0).astype(BF16)
    m_ref[...] = jnp.full(m_ref.shape, MASK_VALUE, F32)
    acc_ref[...] = jnp.zeros(acc_ref.shape, F32)
    ncol = groups * tq // COL_BLOCK

    nkv = k_ref.shape[0] // tk
    ahead = sbuf_ref.shape[0]

    def k_block(kv):
        return k_ref[pl.ds(pl.multiple_of(kv * tk, tk), tk), :]

    def qk(k, c):
        return jnp.dot(k, qmt_ref[:, c * COL_BLOCK:(c + 1) * COL_BLOCK],
                       preferred_element_type=F32)

    k0 = k_block(0)
    for c in range(ahead):
        sbuf_ref[c] = qk(k0, c)

    shift = shift_ref[...]

    def body(it, carry):
        pend = [sbuf_ref[c] for c in range(ahead)]
        for u in range(KV_UNROLL):
            kv = it * KV_UNROLL + u
            k = k_block(kv)
            k_next = k_block(jnp.minimum(kv + 1, nkv - 1))
            vt = vt_ref[kv]
            for c in range(ncol):
                hv = (c * COL_BLOCK // tq) * vheads // groups
                cs = slice(c * COL_BLOCK, (c + 1) * COL_BLOCK)
                s = pend.pop(0)
                if c + ahead < ncol:
                    pend.append(qk(k, c + ahead))
                elif u + 1 < KV_UNROLL:
                    pend.append(qk(k_next, c + ahead - ncol))
                else:
                    sbuf_ref[c + ahead - ncol] = qk(k_next, c + ahead - ncol)
                vth = vt[hv * VT_ROWS:(hv + 1) * VT_ROWS]
                if bounded:
                    p = (jnp.exp2((s - shift).astype(BF16)) if bf16_exp
                         else jnp.exp2(s - shift).astype(BF16))
                    acc_ref[:, cs] = acc_ref[:, cs] + jnp.dot(vth, p, preferred_element_type=F32)
                else:
                    m_prev = m_ref[:, cs]
                    m_new = jnp.maximum(m_prev, jnp.max(s, axis=0, keepdims=True))
                    alpha = jnp.exp2(m_prev - m_new)
                    p = jnp.exp2(s - m_new).astype(BF16)
                    acc_ref[:, cs] = (acc_ref[:, cs] * alpha
                                      + jnp.dot(vth, p, preferred_element_type=F32))
                    m_ref[:, cs] = m_new
        return carry

    lax.fori_loop(0, nkv // KV_UNROLL, body, 0)

    acc = acc_ref[...]
    norm = acc[0:HEAD_DIM] * (1.0 / acc[HEAD_DIM:HEAD_DIM + 1])
    if diff:
        lp = lamp_ref[...]
        lam = (jnp.exp(jnp.sum(lp[0:1] * lp[1:2], axis=1, keepdims=True))
               - jnp.exp(jnp.sum(lp[2:3] * lp[3:4], axis=1, keepdims=True)) + li_ref[...])
        pieces = [norm[:, 2 * hd * tq:(2 * hd + 1) * tq]
                  - lam * norm[:, (2 * hd + 1) * tq:(2 * hd + 2) * tq] for hd in range(groups // 2)]
    else:
        pieces = [norm[:, r * tq:(r + 1) * tq] for r in range(groups)]
    out = jnp.concatenate(pieces, axis=0).T
    if diff:
        ms = _group_mean_sq(out, gmat_ref[...])
        out = out * lax.rsqrt(ms + RMS_EPS) * ogain_ref[...] * (1.0 - li_ref[...])
    o_ref[...] = out.astype(BF16)


def _flash(qkv, vt, segs, gmat, ogain, lamp, li, shift, *, groups, diff, bounded, bf16_exp,
           cols=FLASH_COLS):
    B, S, _ = qkv.shape
    _, nkv, vrows, tk = vt.shape
    assert nkv % KV_UNROLL == 0 and S % (cols // groups) == 0
    tq = cols // groups
    qs, ks = segs
    return pl.pallas_call(
        functools.partial(_flash_kernel, groups=groups, tq=tq, tk=tk, diff=diff,
                          vheads=vrows // VT_ROWS, bounded=bounded, bf16_exp=bf16_exp),
        out_shape=jax.ShapeDtypeStruct((B, S, SEG), BF16),
        grid=(B, S // tq),
        in_specs=[pl.BlockSpec((None, tq, SEG), lambda b, i: (b, i, qs)),
                  pl.BlockSpec((None, S, SEG), lambda b, i: (b, 0, ks)),
                  pl.BlockSpec((None, nkv, vrows, tk), lambda b, i: (b, 0, 0, 0)),
                  _resident(gmat.shape), _resident(ogain.shape), _resident(lamp.shape),
                  _resident(li.shape), _resident(shift.shape)],
        out_specs=pl.BlockSpec((None, tq, SEG), lambda b, i: (b, i, 0)),
        scratch_shapes=[pltpu.VMEM((SEG, cols), BF16), pltpu.VMEM((1, cols), F32),
                        pltpu.VMEM((VT_ROWS, cols), F32),
                        pltpu.VMEM((QK_AHEAD, tk, COL_BLOCK), F32)],
        compiler_params=_cparams(("parallel", "parallel")),
        name=(("diff_attn" if diff else "gqa_attn")
              + ("" if bf16_exp else "_f32exp" if bounded else "_runmax")),
    )(qkv, qkv, vt, gmat, ogain, lamp, li, shift)


def _na_kernel(q_ref, k_ref, v_ref, tab_ref, o_ref, *, nq, nk, bounded):
    i = pl.program_id(2)
    seq = k_ref.shape[0]
    start = jnp.clip(i * nq - (nk - nq) // 2, 0, seq - nk)
    start = pl.multiple_of(start, (nk - nq) // 2)
    kw = k_ref[pl.ds(start, nk), :]
    vw = v_ref[pl.ds(start, nk), :]
    q = q_ref[...]
    if bounded:
        qt = q.astype(F32).T
        vt = vw.astype(F32).T
        row_h = (lax.broadcasted_iota(jnp.int32, (2 * HEAD_DIM, 1), 0)
                 >> int(math.log2(HEAD_DIM)))
        ones = jnp.ones((VT_ROWS - HEAD_DIM, nk), BF16)
        outs = []
        scores = [jnp.dot(kw, jnp.where(row_h == j, qt, 0.0).astype(BF16),
                          preferred_element_type=F32) for j in range(2)]
        for j in range(2):
            p = jnp.exp2(scores[j] + tab_ref[j]).astype(BF16)
            vth = jnp.concatenate(
                [vt[j * HEAD_DIM:(j + 1) * HEAD_DIM].astype(BF16), ones], axis=0)
            pv = jnp.dot(vth, p, preferred_element_type=F32)
            outs.append(pv[0:HEAD_DIM] * (1.0 / pv[HEAD_DIM:HEAD_DIM + 1]))
        o_ref[...] = jnp.concatenate(outs, axis=0).T.astype(BF16)
        return
    lane_h = lax.broadcasted_iota(jnp.int32, (1, 2 * HEAD_DIM), 1) >> int(math.log2(HEAD_DIM))
    outs = []
    for j in range(2):
        qm = jnp.where(lane_h == j, q, jnp.zeros_like(q))
        s = _dot_nt(qm, kw) + tab_ref[j]
        m = jnp.max(s, axis=1, keepdims=True)
        p = jnp.exp2(s - m)
        l = jnp.sum(p, axis=1, keepdims=True)
        outs.append(jnp.dot(p.astype(BF16), vw, preferred_element_type=F32) * (1.0 / l))
    o_ref[...] = jnp.where(lane_h == 0, outs[0], outs[1]).astype(BF16)


def _na_table(rel_bias, rows, shift=None):
    kh = min(NA_WIN_H, rows)
    n_ro, n_co = 2 * NA_WIN_H - 1, 2 * NA_WIN_W - 1
    c = np.arange(GRID_W)[:, None]
    kc = np.arange(GRID_W)[None, :]
    cs = np.clip(c - NA_WIN_W // 2, 0, GRID_W - NA_WIN_W)
    col_valid = (kc >= cs) & (kc < cs + NA_WIN_W)
    col_sel = (kc - c + (NA_WIN_W - 1))[..., None] == np.arange(n_co)
    row_sel, row_valid, row_off = [], [], []
    for r0 in (0, NA_QROWS, rows - NA_QROWS):
        kr0 = int(np.clip(r0 - (NA_KROWS - NA_QROWS) // 2, 0, rows - NA_KROWS))
        r = r0 + np.arange(NA_QROWS)[:, None]
        kr = kr0 + np.arange(NA_KROWS)[None, :]
        rs = np.clip(r - kh // 2, 0, rows - kh)
        row_valid.append((kr >= rs) & (kr < rs + kh))
        row_off.append(kr - r + (NA_WIN_H - 1))
        row_sel.append(row_off[-1][..., None] == np.arange(n_ro))
    row_sel = jnp.asarray(np.stack(row_sel), F32)
    valid = (np.stack(row_valid)[:, None, :, None, :, None]
             & col_valid[None, None, None, :, None, :])
    hi = lax.Precision.HIGHEST
    cols = jnp.einsum("hab,cdb->hacd", rel_bias.astype(F32), jnp.asarray(col_sel, F32),
                      precision=hi)
    nq, nk = NA_QROWS * GRID_W, NA_KROWS * GRID_W
    if shift is None:
        full = jnp.einsum("vqka,hacd->vhqckd", row_sel, cols, precision=hi)
        table = jnp.where(valid, full * LOG2E, MASK_VALUE)
        return table.reshape(3, rel_bias.shape[0], nq, nk)
    heads = rel_bias.shape[0]
    blocks = jnp.where(col_valid[None, None], cols * LOG2E - shift, MASK_VALUE)
    blocks = jnp.concatenate([blocks.transpose(0, 1, 3, 2),
                              jnp.full((heads, 1, GRID_W, GRID_W), MASK_VALUE, F32)], axis=1)
    block_id = np.where(np.stack(row_valid), np.stack(row_off), n_ro).transpose(0, 2, 1)
    pairs = block_id.reshape(3, NA_KROWS, NA_QROWS // 2, 2)
    uniq, pair_id = np.unique(pairs.reshape(-1, 2), axis=0, return_inverse=True)
    pair_id = pair_id.reshape(3, NA_KROWS, NA_QROWS // 2)
    pair_blocks = jnp.concatenate([jnp.take(blocks, jnp.asarray(uniq[:, 0]), axis=1),
                                   jnp.take(blocks, jnp.asarray(uniq[:, 1]), axis=1)], axis=-1)

    def tile_kernel(pb_ref, o_ref):
        for v in range(3):
            for k in range(NA_KROWS):
                for qp in range(NA_QROWS // 2):
                    o_ref[v, k * GRID_W:(k + 1) * GRID_W,
                          qp * 2 * GRID_W:(qp + 1) * 2 * GRID_W] = pb_ref[int(pair_id[v, k, qp])]

    return pl.pallas_call(
        tile_kernel,
        out_shape=jax.ShapeDtypeStruct((3, heads, nk, nq), F32),
        grid=(heads,),
        in_specs=[pl.BlockSpec((None,) + pair_blocks.shape[1:], lambda h: (h, 0, 0, 0))],
        out_specs=pl.BlockSpec((3, None, nk, nq), lambda h: (0, h, 0, 0)),
        compiler_params=_cparams(("parallel",)),
        name="na_table",
    )(pair_blocks)


def _na(qkv, table, *, bounded):
    B, S, _ = qkv.shape
    nq, nk = NA_QROWS * GRID_W, NA_KROWS * GRID_W
    nblk = S // nq
    hp = 2 * HEAD_DIM

    def variant(i):
        return jnp.where(i == 0, 0, jnp.where(i == nblk - 1, 2, 1))

    return pl.pallas_call(
        functools.partial(_na_kernel, nq=nq, nk=nk, bounded=bounded),
        out_shape=jax.ShapeDtypeStruct((B, S, SEG), BF16),
        grid=(2, B, nblk),
        in_specs=[pl.BlockSpec((None, nq, hp), lambda p, b, i: (b, i, 2 * SEG_NA_Q + p)),
                  pl.BlockSpec((None, S, hp), lambda p, b, i: (b, 0, 2 * SEG_NA_K + p)),
                  pl.BlockSpec((None, S, hp), lambda p, b, i: (b, 0, 2 * SEG_NA_V + p)),
                  pl.BlockSpec((None, 2) + table.shape[2:], lambda p, b, i: (variant(i), p, 0, 0))],
        out_specs=pl.BlockSpec((None, nq, hp), lambda p, b, i: (b, i, p)),
        compiler_params=_cparams(("parallel", "parallel", "arbitrary")),
        name="na_attn" if bounded else "na_attn_rowmax",
    )(qkv, qkv, qkv, table)


def _dil_fixed_shift_kernel(q_ref, k_ref, v_ref, shift_ref, o_ref, den_ref, *, tq, halo):
    j = pl.program_id(2)
    length = k_ref.shape[0]
    win = DIL_SUB + 2 * halo
    lane_h = lax.broadcasted_iota(jnp.int32, (1, SEG), 1) >> int(math.log2(HEAD_DIM))
    for u in range(tq // DIL_SUB):
        q0 = j * tq + u * DIL_SUB
        start = pl.multiple_of(jnp.clip(q0 - halo, 0, length - win), halo)
        kw = k_ref[pl.ds(start, win), :]
        vw = v_ref[pl.ds(start, win), :]
        q = q_ref[u * DIL_SUB:(u + 1) * DIL_SUB, :]
        qm = jnp.concatenate([jnp.where(lane_h == hd, q, jnp.zeros_like(q))
                              for hd in range(DIL_HEADS)], axis=0)
        qpos = q0 + lax.broadcasted_iota(jnp.int32, (DIL_SUB, win), 0)
        kpos = start + lax.broadcasted_iota(jnp.int32, (DIL_SUB, win), 1)
        bias = jnp.where(jnp.abs(kpos - qpos) <= halo, -shift_ref[...], MASK_VALUE)
        s = _dot_nt(qm, kw)
        p = jnp.exp2(s + jnp.concatenate([bias] * DIL_HEADS, axis=0))
        den = jnp.sum(p, axis=1, keepdims=True)
        pv = jnp.dot(p.astype(BF16), vw, preferred_element_type=F32)
        num_u = jnp.zeros((DIL_SUB, SEG), F32)
        den_u = jnp.zeros((DIL_SUB, SEG), F32)
        for hd in range(DIL_HEADS):
            rows = slice(hd * DIL_SUB, (hd + 1) * DIL_SUB)
            num_u = jnp.where(lane_h == hd, pv[rows], num_u)
            den_u = jnp.where(lane_h == hd, den[rows], den_u)
        o_ref[u * DIL_SUB:(u + 1) * DIL_SUB, :] = num_u
        den_ref[u * DIL_SUB:(u + 1) * DIL_SUB, :] = den_u


def _dil_kernel(q_ref, k_ref, v_ref, shift_ref, o_ref, lse_ref, *, tq, halo):
    del shift_ref
    j = pl.program_id(2)
    length = k_ref.shape[0]
    win = tq + 2 * halo
    start = pl.multiple_of(jnp.clip(j * tq - halo, 0, length - win), halo)
    kw = k_ref[pl.ds(start, win), :]
    vw = v_ref[pl.ds(start, win), :]
    q = q_ref[...]
    qpos = j * tq + lax.broadcasted_iota(jnp.int32, (tq, win), 0)
    kpos = start + lax.broadcasted_iota(jnp.int32, (tq, win), 1)
    bias = jnp.where(jnp.abs(kpos - qpos) <= halo, 0.0, MASK_VALUE)
    lane_h = lax.broadcasted_iota(jnp.int32, (1, SEG), 1) >> int(math.log2(HEAD_DIM))
    out = jnp.zeros((tq, SEG), F32)
    lse = jnp.zeros((tq, SEG), F32)
    for hd in range(DIL_HEADS):
        qm = jnp.where(lane_h == hd, q, jnp.zeros_like(q))
        s = _dot_nt(qm, kw) + bias
        m = jnp.max(s, axis=1, keepdims=True)
        p = jnp.exp2(s - m)
        l = jnp.sum(p, axis=1, keepdims=True)
        o = jnp.dot(p.astype(BF16), vw, preferred_element_type=F32) * (1.0 / l)
        out = jnp.where(lane_h == hd, o, out)
        lse = jnp.where(lane_h == hd, m + jnp.log(l) * LOG2E, lse)
    o_ref[...] = out
    lse_ref[...] = lse


def _dilated(src, first_seg, halo, shift, *, bounded):
    length = src.shape[-2]
    tq = min(DIL_TQ_FIXED_SHIFT, length) if bounded else DIL_TQ_ROWMAX
    assert length % tq == 0 and tq % DIL_SUB == 0
    if src.ndim == 3:
        (B, length, _), dil = src.shape, 1
        in_specs = [pl.BlockSpec((None, tq, SEG), lambda b, r, j: (b, j, first_seg)),
                    pl.BlockSpec((None, length, SEG), lambda b, r, j: (b, 0, first_seg + 1)),
                    pl.BlockSpec((None, length, SEG), lambda b, r, j: (b, 0, first_seg + 2))]
    else:
        B, dil, length, _ = src.shape
        in_specs = [pl.BlockSpec((None, None, tq, SEG), lambda b, r, j: (b, r, j, first_seg)),
                    pl.BlockSpec((None, None, length, SEG),
                                 lambda b, r, j: (b, r, 0, first_seg + 1)),
                    pl.BlockSpec((None, None, length, SEG),
                                 lambda b, r, j: (b, r, 0, first_seg + 2))]
    oshape = jax.ShapeDtypeStruct((B, dil, length, SEG), F32)
    ospec = pl.BlockSpec((None, None, tq, SEG), lambda b, r, j: (b, r, j, 0))
    return pl.pallas_call(
        functools.partial(_dil_fixed_shift_kernel if bounded else _dil_kernel, tq=tq, halo=halo),
        out_shape=(oshape, oshape),
        grid=(B, dil, length // tq),
        in_specs=in_specs + [_resident(shift.shape)],
        out_specs=(ospec, ospec),
        compiler_params=_cparams(("parallel", "parallel", "arbitrary")),
        name=f"dil_attn_{dil}" + ("" if bounded else "_rowmax"),
    )(src, src, src, shift)


def _outproj_kernel(x_ref, oa_ref, ob_ref, oc_ref, d0_ref, d1_ref, d2_ref,
                    l0_ref, l1_ref, l2_ref, w_ref, fg_ref, fwg_ref, fwu_ref, fwd_ref,
                    o_ref, cat_ref, *row_scratch, bounded, chunks):
    row_scratch = list(row_scratch)

    def token_rows(ref):
        dil = ref.shape[0]
        if dil == 1:
            return ref[0]
        buf = row_scratch.pop()
        for r in range(dil):
            slab = ref[r]
            for hf in range(SEG // LANES):
                buf[hf, pl.ds(r, ref.shape[1], stride=dil), :] = slab[:, hf * LANES:(hf + 1) * LANES]
        return jnp.concatenate([buf[hf] for hf in range(SEG // LANES)], axis=1)

    l0, l1, l2 = token_rows(l0_ref), token_rows(l1_ref), token_rows(l2_ref)
    if bounded:
        od = ((token_rows(d0_ref) + token_rows(d1_ref) + token_rows(d2_ref))
              * (1.0 / (l0 + l1 + l2)))
    else:
        m = jnp.maximum(jnp.maximum(l0, l1), l2)
        e0, e1, e2 = jnp.exp2(l0 - m), jnp.exp2(l1 - m), jnp.exp2(l2 - m)
        od = ((e0 * token_rows(d0_ref) + e1 * token_rows(d1_ref) + e2 * token_rows(d2_ref))
              * (1.0 / (e0 + e1 + e2)))
    cat_ref[:, 0 * SEG:1 * SEG] = oa_ref[...]
    cat_ref[:, 1 * SEG:2 * SEG] = ob_ref[...]
    cat_ref[:, 2 * SEG:3 * SEG] = oc_ref[...]
    cat_ref[:, 3 * SEG:4 * SEG] = od.astype(BF16)
    mixed = x_ref[...] + jnp.dot(cat_ref[...], w_ref[...], preferred_element_type=F32)
    o_ref[...] = _ffn_tile(mixed, fg_ref, fwg_ref, fwu_ref, fwd_ref, chunks)


def _outproj_ffn(x, oa, ob, oc, dils, w_out, fgain, fwg, fwu, fwd, *, bounded, tm=512):
    B, S, D = x.shape
    F = fwg.shape[1]
    xspec = pl.BlockSpec((None, tm, D), lambda b, i: (b, i, 0))
    sspec = pl.BlockSpec((None, tm, SEG), lambda b, i: (b, i, 0))
    outs = [d[0] for d in dils]
    lses = [d[1] for d in dils]
    dspecs = [pl.BlockSpec((None, a.shape[1], tm // a.shape[1], SEG), lambda b, i: (b, 0, i, 0))
              for a in outs + lses]
    n_scratch = sum(a.shape[1] > 1 for a in outs + lses)
    return pl.pallas_call(
        functools.partial(_outproj_kernel, bounded=bounded, chunks=_ffn_chunks(F)),
        out_shape=jax.ShapeDtypeStruct(x.shape, F32),
        grid=(B, S // tm),
        in_specs=[xspec] + [sspec] * 3 + dspecs
        + [_resident(w_out.shape), _resident((1, D)), _resident((D, F)), _resident((D, F)),
           _resident((F, D))],
        out_specs=xspec,
        scratch_shapes=[pltpu.VMEM((tm, 4 * SEG), BF16)]
        + [pltpu.VMEM((SEG // LANES, tm, LANES), F32)] * n_scratch,
        compiler_params=_cparams(("parallel", "parallel")),
        name="outproj_ffn" if bounded else "outproj_lse_ffn",
    )(x, oa, ob, oc, *outs, *lses, w_out, fgain.reshape(1, D), fwg, fwu, fwd)


def _rope_cs(pos, dim, theta):
    inv_freq = 1.0 / (theta ** (jnp.arange(0, dim, 2, dtype=F32) / dim))
    ang = pos.astype(F32)[:, None] * inv_freq[None, :]
    return jnp.cos(ang), jnp.sin(ang)


def _rope_tables(S):
    pos = jnp.arange(S, dtype=jnp.int32)

    def expand(parts_c, parts_s, reps):
        return (jnp.tile(jnp.concatenate(parts_c, axis=1), (1, reps)),
                jnp.tile(jnp.concatenate(parts_s, axis=1), (1, reps)))

    c, s = _rope_cs(pos, DIFF_QK_DIM // ROPE_FRACTION, ROPE_THETA)
    rest = DIFF_QK_DIM - 2 * c.shape[1]
    df = expand([c, c, jnp.ones((S, rest), F32)], [-s, s, jnp.zeros((S, rest), F32)],
                SEG // DIFF_QK_DIM)
    c, s = _rope_cs(pos, HEAD_DIM // ROPE_FRACTION, ROPE_THETA)
    rest = HEAD_DIM - 2 * c.shape[1]
    dl = expand([c, c, jnp.ones((S, rest), F32)], [-s, s, jnp.zeros((S, rest), F32)],
                SEG // HEAD_DIM)
    cr, sr = _rope_cs(pos // GRID_W, HEAD_DIM // 2, AXIAL_THETA)
    cc, sc = _rope_cs(pos % GRID_W, HEAD_DIM // 2, AXIAL_THETA)
    gq = expand([cr, cr, cc, cc], [-sr, sr, -sc, sc], SEG // HEAD_DIM)
    return df + gq + dl


def _group_matrices():
    lane = np.arange(SEG)
    mats = [(lane[:, None] // g == lane[None, :] // g).astype(np.float32) / g
            for g in (HEAD_DIM, DIFF_QK_DIM)]
    return jnp.asarray(np.stack(mats), BF16)


def _extend_w_in(w_in):
    kv0 = 7 * SEG
    hd = HEAD_DIM
    cols = np.concatenate([
        np.arange(0, kv0),
        kv0 + np.array([0, 0, 1, 1]).repeat(hd) * hd + np.tile(np.arange(hd), 4),
        kv0 + 2 * hd + np.array([0, 0, 1, 1]).repeat(hd) * hd + np.tile(np.arange(hd), 4),
        np.arange(kv0 + 4 * hd, w_in.shape[1]),
    ])
    return w_in[:, cols].astype(BF16)


def _gain_rows(na_q, na_k, df_q, df_k, gq_q, gq_k, dl_q, dl_k):
    def row(g, scale):
        return jnp.tile(g.astype(F32), SEG // g.shape[0]) * scale
    sc64 = HEAD_DIM ** -0.5 * LOG2E
    sc32 = DIFF_QK_DIM ** -0.5 * LOG2E
    return jnp.stack([row(na_q, sc64), row(na_k, 1.0), row(df_q, sc32), row(df_k, 1.0),
                      row(gq_q, sc64), row(gq_k, 1.0), row(dl_q, sc64), row(dl_k, 1.0)])


def kernel(x, ffn1_norm, ffn1_w_gate, ffn1_w_up, ffn1_w_down, mix_norm, w_in, w_out, na_q_norm, na_k_norm, na_rel_bias, diff_q_norm, diff_k_norm, diff_lambda_q1, diff_lambda_k1, diff_lambda_q2, diff_lambda_k2, diff_out_norm, gqa_q_norm, gqa_k_norm, dil_q_norm, dil_k_norm, ffn2_norm, ffn2_w_gate, ffn2_w_up, ffn2_w_down):
    B, S, D = x.shape
    depth = w_in.shape[0]
    rows = S // GRID_W
    tables = _rope_tables(S)
    gmats = _group_matrices()
    ones_row = jnp.ones((1, SEG), F32)
    zero_lamp = jnp.zeros((4, DIFF_QK_DIM), F32)
    zero_li = jnp.zeros((1, 1), F32)

    for l in range(depth):
        gains = _gain_rows(na_q_norm[l], na_k_norm[l], diff_q_norm[l], diff_k_norm[l],
                           gqa_q_norm[l], gqa_k_norm[l], dil_q_norm[l], dil_k_norm[l])
        x, qkv, vt_df, vt_gq, *dl_classes = _ffn_inproj(
            x, ffn1_norm[l], ffn1_w_gate[l].astype(BF16), ffn1_w_up[l].astype(BF16),
            ffn1_w_down[l].astype(BF16), mix_norm[l], _extend_w_in(w_in[l]), gmats, gains, tables)

        na_bias = na_rel_bias[l]
        lamp = jnp.stack([diff_lambda_q1[l], diff_lambda_k1[l], diff_lambda_q2[l],
                          diff_lambda_k2[l]]).astype(F32)
        li = jnp.full((1, 1), 0.8 - 0.6 * math.exp(-0.3 * l), F32)
        ogain = jnp.tile(diff_out_norm[l].astype(F32), SEG // HEAD_DIM).reshape(1, SEG)
        dl_src = {1: (qkv, SEG_DL_Q)}
        dl_src.update({d: (a, 0) for d, a in zip(DILATIONS, dl_classes)})
        w_out_l = w_out[l].astype(BF16)
        ffn2 = (ffn2_norm[l], ffn2_w_gate[l].astype(BF16), ffn2_w_up[l].astype(BF16),
                ffn2_w_down[l].astype(BF16))

        def score_bound(qi, width):
            return width * jnp.max(jnp.abs(gains[qi])) * jnp.max(jnp.abs(gains[qi + 1]))
        bounds = {"na": score_bound(0, HEAD_DIM) + LOG2E * jnp.max(jnp.abs(na_rel_bias[l])),
                  "df": score_bound(2, DIFF_QK_DIM), "gq": score_bound(4, HEAD_DIM),
                  "dl": score_bound(6, HEAD_DIM)}
        shifts = {k: v.reshape(1, 1).astype(F32) for k, v in bounds.items()}

        def mix(bounded, bf16_exp, x=x, qkv=qkv, vt_df=vt_df, vt_gq=vt_gq, dl_src=dl_src,
                shifts=shifts, na_bias=na_bias, ogain=ogain, lamp=lamp, li=li, w_out_l=w_out_l,
                ffn2=ffn2):
            na_table = _na_table(na_bias, rows, shifts["na"][0, 0] if bounded else None)
            o_a = _na(qkv, na_table, bounded=bounded)
            o_b = _flash(qkv, vt_df, (SEG_DF_Q, SEG_DF_K), gmats[0], ogain, lamp, li,
                         shifts["df"], groups=2 * DIFF_HEADS, diff=True, bounded=bounded,
                         bf16_exp=bf16_exp)
            o_c = _flash(qkv, vt_gq, (SEG_GQ_Q, SEG_GQ_K), gmats[0], ones_row, zero_lamp,
                         zero_li, shifts["gq"], groups=GQA_Q_HEADS, diff=False, bounded=bounded,
                         bf16_exp=bf16_exp)
            dils = [_dilated(*dl_src[dil], (window // 2) // dil, shifts["dl"], bounded=bounded)
                    for window, dil in DIL_PAIRS]
            return _outproj_ffn(x, o_a, o_b, o_c, dils, w_out_l, *ffn2, bounded=bounded)

        fixed_ok = functools.reduce(jnp.maximum, bounds.values()) <= MAX_FIXED_SHIFT
        bf16_ok = jnp.maximum(bounds["df"], bounds["gq"]) <= MAX_BF16_EXP_SHIFT
        mode = jnp.where(fixed_ok, jnp.where(bf16_ok, 0, 1), 2)
        x = lax.switch(mode, [functools.partial(mix, True, True),
                              functools.partial(mix, True, False),
                              functools.partial(mix, False, False)])
    return x
```
